```python
import jax
import jax.numpy as jnp
from jax import lax
import numpy as np

D_MODEL = 1024
BATCH = 32
SEQ = 256
DEPTH = 4
DEC_BATCH = 2
DEC_SEQ = 2048
PAST_LEN = 256

GRID_W = 64
N_MIXERS = 3
N_LAYERS_A = (DEPTH + 2) // 3
N_LAYERS_B = (DEPTH + 1) // 3
N_LAYERS_C = DEPTH // 3

MLSTM_HEADS = 4
MLSTM_QK = D_MODEL // 2
MLSTM_V = D_MODEL
MLSTM_DK = MLSTM_QK // MLSTM_HEADS
MLSTM_DV = MLSTM_V // MLSTM_HEADS
MLSTM_CHUNK = 64

GLA_HEADS = 4
GLA_QK = D_MODEL // 2
GLA_V = D_MODEL
GLA_DK = GLA_QK // GLA_HEADS
GLA_DV = GLA_V // GLA_HEADS
GLA_GATE_RANK = 16
GLA_TAU = 16.0
GLA_CHUNK = 32

RWKV_N = 64
RWKV_HEADS = D_MODEL // RWKV_N
RWKV_DECAY_RANK = 64
RWKV_A_RANK = 64
RWKV_GATE_RANK = 128
RWKV_GN_EPS = 64e-5

MOE_GROUPS = 4
MOE_PER_GROUP = 4
MOE_EXPERTS = MOE_GROUPS * MOE_PER_GROUP
MOE_TOP_K = 2
MOE_HIDDEN = 512

NORM_EPS = 1e-6
F32 = jnp.float32

kernel_name = 'hybrid_mlstm_gla_rwkv7_hmoe_diffusion_step'


def rmsnorm(x, g):
    xf = x.astype(F32)
    y = xf * lax.rsqrt(jnp.mean(xf * xf, axis=-1, keepdims=True) + NORM_EPS)
    return (y * g.astype(F32)).astype(x.dtype)


def to_heads(x, n_heads):
    *lead, t, ch = x.shape
    return jnp.swapaxes(x.reshape(*lead, t, n_heads, ch // n_heads), -2, -3)


def from_heads(x):
    *lead, h, t, d = x.shape
    return jnp.swapaxes(x, -2, -3).reshape(*lead, t, h * d)


def head_norm(x, g, center, eps):
    xf = x.astype(F32)
    if center:
        xf = xf - jnp.mean(xf, axis=-1, keepdims=True)
    xf = xf * lax.rsqrt(jnp.mean(xf * xf, axis=-1, keepdims=True) + eps)
    return from_heads(xf) * g.astype(F32)


def modulation(cond, ada_w, ada_b):
    mod = jax.nn.silu(cond) @ ada_w + ada_b
    return jnp.split(mod[..., None, :], 6, axis=-1)


def grid_conv(x, w, rows):
    b, t, ch = x.shape
    xg = x.reshape(b, rows, t // rows, ch)
    y = lax.conv_general_dilated(xg, w[:, :, None, :], window_strides=(1, 1), padding='SAME',
                                 dimension_numbers=('NHWC', 'HWIO', 'NHWC'), feature_group_count=ch)
    return y.reshape(b, t, ch)


def _causal_mask(n):
    return jnp.tril(jnp.ones((n, n), dtype=bool))


def _chunked_scan(step, carry0, xs, chunk):
    def split(z):
        b, h, t = z.shape[:3]
        return jnp.moveaxis(z.reshape(b, h, t // chunk, chunk, *z.shape[3:]), 2, 0)
    carry, ys = lax.scan(step, carry0, tuple(split(z) for z in xs))
    ys = jnp.moveaxis(ys, 0, 2)
    return ys.reshape(ys.shape[0], ys.shape[1], -1, *ys.shape[4:]), carry


def _bidirectional(step, chunk, shared, per_dir, init):
    init = tuple(s.astype(F32) for s in init)
    flip = lambda z: jnp.flip(z, axis=2)
    y_f, s_f = _chunked_scan(step, tuple(s[:, 0] for s in init),
                             shared + tuple(p[0] for p in per_dir), chunk)
    y_b, s_b = _chunked_scan(step, tuple(s[:, 1] for s in init),
                             tuple(flip(z) for z in shared) + tuple(flip(p[1]) for p in per_dir), chunk)
    return y_f + flip(y_b), tuple(jnp.stack([a, b], axis=1) for a, b in zip(s_f, s_b))


def _mlstm_chunk_step(carry, inp):
    c_st, n_st, m_st = carry
    q, k, v, ig, lf = inp
    mask = _causal_mask(q.shape[2])
    b = jnp.cumsum(lf, axis=-1)
    log_d = jnp.where(mask, b[..., :, None] - b[..., None, :] + ig[..., None, :], -jnp.inf)
    log_inter = b + m_st[..., None]
    m_t = jnp.maximum(log_inter, jnp.max(log_d, axis=-1))
    s = jnp.einsum('bhtd,bhsd->bhts', q, k) * jnp.exp(log_d - m_t[..., None])
    w_inter = jnp.exp(log_inter - m_t)
    num = jnp.einsum('bhts,bhsv->bhtv', s, v) + w_inter[..., None] * jnp.einsum('bhtd,bhdv->bhtv', q, c_st)
    den = jnp.sum(s, axis=-1) + w_inter * jnp.einsum('bhtd,bhd->bht', q, n_st)
    h = num / jnp.maximum(jnp.abs(den), jnp.exp(-m_t))[..., None]
    b_last = b[..., -1]
    log_w = b_last[..., None] - b + ig
    m_new = jnp.maximum(b_last + m_st, jnp.max(log_w, axis=-1))
    w_s = jnp.exp(log_w - m_new[..., None])
    decay = jnp.exp(b_last + m_st - m_new)
    c_new = decay[..., None, None] * c_st + jnp.einsum('bhsd,bhsv->bhdv', w_s[..., None] * k, v)
    n_new = decay[..., None] * n_st + jnp.einsum('bhs,bhsd->bhd', w_s, k)
    return (c_new, n_new, m_new), h


def _gla_chunk_step(carry, inp):
    (s_st,) = carry
    q, k, v, lg = inp
    mask = _causal_mask(q.shape[2])
    g = jnp.cumsum(lg, axis=2)
    diff = g[:, :, :, None, :] - g[:, :, None, :, :]
    decay = jnp.exp(jnp.where(mask[:, :, None], diff, -jnp.inf))
    att = jnp.einsum('bhtsd,bhtd,bhsd->bhts', decay, q, k)
    o = jnp.einsum('bhts,bhsv->bhtv', att, v) + jnp.einsum('bhtd,bhdv->bhtv', q * jnp.exp(g), s_st)
    g_last = g[:, :, -1]
    s_new = jnp.exp(g_last)[..., None] * s_st + jnp.einsum(
        'bhsd,bhsv->bhdv', k * jnp.exp(g_last[:, :, None, :] - g), v)
    return (s_new,), o


def _rwkv_step(carry, inp):
    (s_st,) = carry
    r, k, v, kk, a, w = (z[:, :, 0] for z in inp)
    s_st = (s_st * w[:, :, None, :]
            - jnp.einsum('bhvk,bhk->bhv', s_st, kk)[..., None] * (kk * a)[:, :, None, :]
            + v[..., None] * k[:, :, None, :])
    return (s_st,), jnp.einsum('bhvk,bhk->bhv', s_st, r)[:, :, None]


def mlstm_mixer(h, rows, init, w_in, conv_w, w_gate, b_gate, norm_g, w_out):
    qk, v, o = jnp.split(h @ w_in, [2 * MLSTM_QK, 2 * MLSTM_QK + MLSTM_V], axis=-1)
    q, k = jnp.split(jax.nn.silu(grid_conv(qk, conv_w, rows)), 2, axis=-1)
    q = to_heads(q, MLSTM_HEADS).astype(F32) * MLSTM_DK ** -0.5
    k = to_heads(k, MLSTM_HEADS).astype(F32)
    v = to_heads(v, MLSTM_HEADS).astype(F32)
    gates = (jnp.einsum('btd,zdg->zbgt', h, w_gate) + b_gate[:, None, :, None]).astype(F32)
    ig = gates[:, :, :MLSTM_HEADS]
    lf = jax.nn.log_sigmoid(gates[:, :, MLSTM_HEADS:])
    y, state = _bidirectional(_mlstm_chunk_step, MLSTM_CHUNK, (q, k, v), (ig, lf), init)
    y = jax.nn.sigmoid(o.astype(F32)) * head_norm(y, norm_g, True, NORM_EPS)
    return y.astype(h.dtype) @ w_out, tuple(s.astype(h.dtype) for s in state)


def gla_mixer(h, rows, init, w_in, conv_w, w_a1, w_a2, b_a, norm_g, w_out):
    qkv, r = jnp.split(h @ w_in, [2 * GLA_QK + GLA_V], axis=-1)
    q, k, v = jnp.split(jax.nn.silu(grid_conv(qkv, conv_w, rows)), [GLA_QK, 2 * GLA_QK], axis=-1)
    q = to_heads(q, GLA_HEADS).astype(F32) * GLA_DK ** -0.5
    k = to_heads(k, GLA_HEADS).astype(F32)
    v = to_heads(v, GLA_HEADS).astype(F32)
    z = jnp.einsum('zbtr,zrk->zbtk', jnp.einsum('btd,zdr->zbtr', h, w_a1), w_a2) + b_a[:, None, None, :]
    lg = to_heads(jax.nn.log_sigmoid(z.astype(F32)) / GLA_TAU, GLA_HEADS)
    y, state = _bidirectional(_gla_chunk_step, GLA_CHUNK, (q, k, v), (lg,), init)
    y = head_norm(y, norm_g, False, NORM_EPS) * jax.nn.silu(r.astype(F32))
    return y.astype(h.dtype) @ w_out, tuple(s.astype(h.dtype) for s in state)


def rwkv_mixer(h, init, mu, w_rkv, w0, w1, w2, a0, a1, a2, g1, g2, k_k, k_a, r_k, ln_g, ln_b, w_out):
    zero = jnp.zeros_like(h[:, :1])
    x_prev = jnp.concatenate([zero, h[:, :-1]], axis=1)
    x_next = jnp.concatenate([h[:, 1:], zero], axis=1)
    xx = 0.5 * (x_prev + x_next) - h
    xr, xw, xk, xv, xa, xg = h[None] + xx[None] * mu[:, None, None, :]
    r, k, v = jnp.einsum('zbtd,zde->zbte', jnp.stack([xr, xk, xv]), w_rkv)
    lora_w = jnp.einsum('zbtr,zrd->zbtd', jnp.tanh(jnp.einsum('btd,zdr->zbtr', xw, w1)), w2)
    w_log = -jax.nn.softplus(-(w0[:, None, None, :] + lora_w).astype(F32)) - 0.5
    decay = to_heads(jnp.exp(-jnp.exp(w_log)), RWKV_HEADS)
    a = to_heads(jax.nn.sigmoid((a0 + (xa @ a1) @ a2).astype(F32)), RWKV_HEADS)
    g = (jax.nn.sigmoid(xg @ g1) @ g2).astype(F32)
    kk = to_heads((k * k_k).astype(F32), RWKV_HEADS)
    kk = kk / jnp.maximum(jnp.sqrt(jnp.sum(kk * kk, axis=-1, keepdims=True)), 1e-12)
    r = to_heads(r.astype(F32), RWKV_HEADS)
    v = to_heads(v.astype(F32), RWKV_HEADS)
    k = to_heads(k.astype(F32), RWKV_HEADS) * (1 + (a - 1) * k_a.astype(F32).reshape(RWKV_HEADS, 1, RWKV_N))
    y, state = _bidirectional(_rwkv_step, 1, (r, k, v, kk, a), (decay,), init)
    y = head_norm(y, ln_g, True, RWKV_GN_EPS) + ln_b.astype(F32)
    bonus = jnp.sum(r * k * r_k.astype(F32)[:, None, :], axis=-1, keepdims=True) * v
    y = (y + from_heads(bonus)) * g
    return y.astype(h.dtype) @ w_out, tuple(s.astype(h.dtype) for s in state)


def hier_moe(h, w_group, b_group, w_expert, b_expert, w_gate, w_up, w_down):
    b, t, d = h.shape
    tok = h.reshape(b * t, d)
    glog = (tok @ w_group + b_group).astype(F32)
    gsel = jnp.argmax(glog, axis=-1)
    gw = jnp.take_along_axis(jax.nn.softmax(glog, axis=-1), gsel[:, None], axis=-1)
    elog = (tok @ w_expert + b_expert).astype(F32).reshape(-1, MOE_GROUPS, MOE_PER_GROUP)
    elog = jnp.take_along_axis(elog, gsel[:, None, None], axis=1)[:, 0]
    top_p, top_i = lax.top_k(jax.nn.softmax(elog, axis=-1), MOE_TOP_K)
    top_p = top_p / jnp.sum(top_p, axis=-1, keepdims=True) * gw
    combine = jnp.sum(jax.nn.one_hot(gsel[:, None] * MOE_PER_GROUP + top_i, MOE_EXPERTS, dtype=F32)
                      * top_p[..., None], axis=1)
    hid = jax.nn.silu(jnp.einsum('nd,edf->nef', tok, w_gate)) * jnp.einsum('nd,edf->nef', tok, w_up)
    y = jnp.einsum('nef,efd->nd', hid * combine[..., None].astype(h.dtype), w_down)
    return y.reshape(b, t, d)


def run_trunk(x, cond, rows, init_a, init_b, init_c, shared_p, mlstm_p, gla_p, rwkv_p, final_norm_g):
    (ada_w, ada_b, norm1_g, norm2_g, w_group, b_group, w_expert, b_expert,
     w_gate, w_up, w_down) = shared_p
    new_a, new_b, new_c = [], [], []
    for i in range(DEPTH):
        j = i // N_MIXERS
        shift1, scale1, gate1, shift2, scale2, gate2 = modulation(cond, ada_w[i], ada_b[i])
        hmix = rmsnorm(x, norm1_g[i]) * (1 + scale1) + shift1
        if i % N_MIXERS == 0:
            y, st = mlstm_mixer(hmix, rows, tuple(s[:, j] for s in init_a), *(w[j] for w in mlstm_p))
            new_a.append(st)
        elif i % N_MIXERS == 1:
            y, st = gla_mixer(hmix, rows, tuple(s[:, j] for s in init_b), *(w[j] for w in gla_p))
            new_b.append(st)
        else:
            y, st = rwkv_mixer(hmix, tuple(s[:, j] for s in init_c), *(w[j] for w in rwkv_p))
            new_c.append(st)
        x = x + gate1 * y
        hff = rmsnorm(x, norm2_g[i]) * (1 + scale2) + shift2
        x = x + gate2 * hier_moe(hff, w_group[i], b_group[i], w_expert[i], b_expert[i],
                                 w_gate[i], w_up[i], w_down[i])

    def stack(per_layer):
        return tuple(jnp.stack(parts, axis=1) for parts in zip(*per_layer))
    return rmsnorm(x, final_norm_g), stack(new_a), stack(new_b), stack(new_c)


def setup_inputs(seed: int = 0) -> dict:
    key = jax.random.key(seed)
    keys = iter(jax.random.split(key, 64))

    def nrm(shape, scale):
        return jax.random.normal(next(keys), shape, F32) * scale

    def uni(shape, lo, hi):
        return jax.random.uniform(next(keys), shape, F32, lo, hi)

    D = D_MODEL
    NA, NB, NC = N_LAYERS_A, N_LAYERS_B, N_LAYERS_C
    H = MLSTM_HEADS
    return {
        'x_prompt': nrm((BATCH, SEQ, D), 1.0),
        'x_sample': nrm((DEC_BATCH, DEC_SEQ, D), 1.0),
        'state_mlstm_C': nrm((DEC_BATCH, NA, 2, H, MLSTM_DK, MLSTM_DV), 0.1),
        'state_mlstm_n': nrm((DEC_BATCH, NA, 2, H, MLSTM_DK), 0.1),
        'state_mlstm_m': nrm((DEC_BATCH, NA, 2, H), 0.5),
        'state_gla_S': nrm((DEC_BATCH, NB, 2, GLA_HEADS, GLA_DK, GLA_DV), 0.1),
        'state_rwkv_S': nrm((DEC_BATCH, NC, 2, RWKV_HEADS, RWKV_N, RWKV_N), 0.1),
        'c': nrm((DEC_BATCH, D), 1.0),
        'c_ctx': nrm((D,), 1.0),
        'ada_w': nrm((DEPTH, D, 6 * D), 0.5 * D ** -0.5),
        'ada_b': nrm((DEPTH, 6 * D), 0.02),
        'norm1_g': 1.0 + nrm((DEPTH, D), 0.02),
        'norm2_g': 1.0 + nrm((DEPTH, D), 0.02),
        'moe_w_group': nrm((DEPTH, D, MOE_GROUPS), D ** -0.5),
        'moe_b_group': nrm((DEPTH, MOE_GROUPS), 0.01),
        'moe_w_expert': nrm((DEPTH, D, MOE_EXPERTS), D ** -0.5),
        'moe_b_expert': nrm((DEPTH, MOE_EXPERTS), 0.01),
        'moe_w_gate': nrm((DEPTH, MOE_EXPERTS, D, MOE_HIDDEN), D ** -0.5),
        'moe_w_up': nrm((DEPTH, MOE_EXPERTS, D, MOE_HIDDEN), D ** -0.5),
        'moe_w_down': nrm((DEPTH, MOE_EXPERTS, MOE_HIDDEN, D), MOE_HIDDEN ** -0.5),
        'final_norm_g': 1.0 + nrm((D,), 0.02),
        'mlstm_w_in': nrm((NA, D, 2 * MLSTM_QK + 2 * MLSTM_V), D ** -0.5),
        'mlstm_conv_w': nrm((NA, 3, 3, 2 * MLSTM_QK), 1.0 / 3.0),
        'mlstm_w_gate': nrm((NA, 2, D, 2 * H), 0.1 * D ** -0.5),
        'mlstm_b_gate': jnp.concatenate([nrm((NA, 2, H), 0.1), uni((NA, 2, H), 3.0, 6.0)], axis=-1),
        'mlstm_norm_g': 1.0 + nrm((NA, MLSTM_V), 0.02),
        'mlstm_w_out': nrm((NA, MLSTM_V, D), MLSTM_V ** -0.5),
        'gla_w_in': nrm((NB, D, 2 * GLA_QK + 2 * GLA_V), D ** -0.5),
        'gla_conv_w': nrm((NB, 3, 3, 2 * GLA_QK + GLA_V), 1.0 / 3.0),
        'gla_w_a1': nrm((NB, 2, D, GLA_GATE_RANK), D ** -0.5),
        'gla_w_a2': nrm((NB, 2, GLA_GATE_RANK, GLA_QK), GLA_GATE_RANK ** -0.5),
        'gla_b_a': nrm((NB, 2, GLA_QK), 0.1),
        'gla_norm_g': 1.0 + nrm((NB, GLA_V), 0.02),
        'gla_w_out': nrm((NB, GLA_V, D), GLA_V ** -0.5),
        'rwkv_mu': uni((NC, 6, D), 0.0, 1.0),
        'rwkv_w_rkv': nrm((NC, 3, D, D), D ** -0.5),
        'rwkv_w0': uni((NC, 2, D), -6.0, -1.0),
        'rwkv_w1': nrm((NC, 2, D, RWKV_DECAY_RANK), D ** -0.5),
        'rwkv_w2': nrm((NC, 2, RWKV_DECAY_RANK, D), 0.1 * RWKV_DECAY_RANK ** -0.5),
        'rwkv_a0': nrm((NC, D), 0.1),
        'rwkv_a1': nrm((NC, D, RWKV_A_RANK), D ** -0.5),
        'rwkv_a2': nrm((NC, RWKV_A_RANK, D), 0.1 * RWKV_A_RANK ** -0.5),
        'rwkv_g1': nrm((NC, D, RWKV_GATE_RANK), D ** -0.5),
        'rwkv_g2': nrm((NC, RWKV_GATE_RANK, D), RWKV_GATE_RANK ** -0.5),
        'rwkv_k_k': 0.85 + nrm((NC, D), 0.02),
        'rwkv_k_a': 1.0 + nrm((NC, D), 0.02),
        'rwkv_r_k': nrm((NC, RWKV_HEADS, RWKV_N), 0.1),
        'rwkv_ln_g': 1.0 + nrm((NC, D), 0.02),
        'rwkv_ln_b': nrm((NC, D), 0.02),
        'rwkv_w_out': nrm((NC, D, D), D ** -0.5),
    }


def reference(x_prompt, x_sample, state_mlstm_C, state_mlstm_n, state_mlstm_m, state_gla_S, state_rwkv_S,
              c, c_ctx, ada_w, ada_b, norm1_g, norm2_g, moe_w_group, moe_b_group, moe_w_expert,
              moe_b_expert, moe_w_gate, moe_w_up, moe_w_down, final_norm_g,
              mlstm_w_in, mlstm_conv_w, mlstm_w_gate, mlstm_b_gate, mlstm_norm_g, mlstm_w_out,
              gla_w_in, gla_conv_w, gla_w_a1, gla_w_a2, gla_b_a, gla_norm_g, gla_w_out,
              rwkv_mu, rwkv_w_rkv, rwkv_w0, rwkv_w1, rwkv_w2, rwkv_a0, rwkv_a1, rwkv_a2,
              rwkv_g1, rwkv_g2, rwkv_k_k, rwkv_k_a, rwkv_r_k, rwkv_ln_g, rwkv_ln_b, rwkv_w_out):
    shared_p = (ada_w, ada_b, norm1_g, norm2_g, moe_w_group, moe_b_group, moe_w_expert, moe_b_expert,
                moe_w_gate, moe_w_up, moe_w_down)
    mlstm_p = (mlstm_w_in, mlstm_conv_w, mlstm_w_gate, mlstm_b_gate, mlstm_norm_g, mlstm_w_out)
    gla_p = (gla_w_in, gla_conv_w, gla_w_a1, gla_w_a2, gla_b_a, gla_norm_g, gla_w_out)
    rwkv_p = (rwkv_mu, rwkv_w_rkv, rwkv_w0, rwkv_w1, rwkv_w2, rwkv_a0, rwkv_a1, rwkv_a2,
              rwkv_g1, rwkv_g2, rwkv_k_k, rwkv_k_a, rwkv_r_k, rwkv_ln_g, rwkv_ln_b, rwkv_w_out)

    def zeros_ctx(s):
        return jnp.zeros((x_prompt.shape[0],) + s.shape[1:], F32)

    y_prompt, st_a, st_b, st_c = run_trunk(
        x_prompt, c_ctx, 1,
        tuple(zeros_ctx(s) for s in (state_mlstm_C, state_mlstm_n, state_mlstm_m)),
        (zeros_ctx(state_gla_S),), (zeros_ctx(state_rwkv_S),),
        shared_p, mlstm_p, gla_p, rwkv_p, final_norm_g)
    y_sample = run_trunk(
        x_sample, c, x_sample.shape[1] // GRID_W,
        (state_mlstm_C, state_mlstm_n, state_mlstm_m), (state_gla_S,), (state_rwkv_S,),
        shared_p, mlstm_p, gla_p, rwkv_p, final_norm_g)[0]
    new_mlstm_C, new_mlstm_n, new_mlstm_m = st_a
    (new_gla_S,) = st_b
    (new_rwkv_S,) = st_c
    return (y_prompt, y_sample, new_mlstm_C, new_mlstm_n, new_mlstm_m, new_gla_S, new_rwkv_S)
```

```python
import functools

import jax
import jax.numpy as jnp
from jax import lax
from jax.experimental import pallas as pl
from jax.experimental.pallas import tpu as pltpu

F32 = jnp.float32
BF16 = jnp.bfloat16
HIGHEST = lax.Precision.HIGHEST

D_MODEL = 1024
GRID_W = 64
NORM_EPS = 1e-6

MLSTM_HEADS = 4
MLSTM_DK = 128
MLSTM_DV = 256
MLSTM_CHUNK = 64

GLA_HEADS = 4
GLA_DK = 128
GLA_DV = 256
GLA_TAU = 16.0
GLA_CHUNK = 32

RWKV_N = 64
RWKV_HEADS = 16
RWKV_PAIRS = RWKV_HEADS // 2
RWKV_GN_EPS = 64e-5
RWKV_CHUNK = 32

MOE_GROUPS = 4
MOE_PER_GROUP = 4
MOE_EXPERTS = 16
MOE_HIDDEN = 512
ROUTER_EXPERT_LANE0 = 8

LANES = 128
VMEM_LIMIT_BYTES = 56 * 1024 * 1024


def _call(kernel, *, grid, in_specs, out_specs, out_shape, scratch=(), sem, name):
    return pl.pallas_call(
        kernel, grid=grid, in_specs=in_specs, out_specs=out_specs, out_shape=out_shape,
        scratch_shapes=list(scratch), name=name,
        compiler_params=pltpu.CompilerParams(dimension_semantics=sem, vmem_limit_bytes=VMEM_LIMIT_BYTES))


def _sigmoid(x):
    return 1.0 / (1.0 + jnp.exp(-x))


def _silu(x):
    return x * _sigmoid(x)


def _logsig(x):
    return jnp.minimum(x, 0.0) - jnp.log1p(jnp.exp(-jnp.abs(x)))


def _dot(a, b):
    return jnp.dot(a.astype(BF16), b.astype(BF16), preferred_element_type=F32)


def _dot_nt(a, b):
    return lax.dot_general(a.astype(BF16), b.astype(BF16), (((1,), (1,)), ((), ())), preferred_element_type=F32)


def _dot_tn(a, b):
    return lax.dot_general(a.astype(BF16), b.astype(BF16), (((0,), (0,)), ((), ())), preferred_element_type=F32)


def _dot_f32(a, b):
    return jnp.dot(a, b, precision=HIGHEST, preferred_element_type=F32)


def _tri(n, kind, block=None):
    r = lax.broadcasted_iota(jnp.int32, (n, n), 0)
    c = lax.broadcasted_iota(jnp.int32, (n, n), 1)
    m = {"lower": c <= r, "upper": c >= r, "all": c >= 0}[kind]
    if block is not None:
        sh = block.bit_length() - 1
        m = jnp.logical_and(m, jnp.right_shift(r, sh) == jnp.right_shift(c, sh))
    return jnp.where(m, 1.0, 0.0).astype(F32)


def _mm(x, w, *, tm, tn, xlead=(), wlead=(), extras=(), epilogue=None, out_dtype=F32, name):
    m, k = x.shape[-2:]
    n = w.shape[-1]
    tm, tn = min(tm, m), min(tn, n)

    def kern(x_ref, w_ref, *rest):
        acc = _dot(x_ref[...], w_ref[...])
        if epilogue is not None:
            acc = epilogue(acc, *[r[...] for r in rest[:-1]])
        rest[-1][...] = acc.astype(out_dtype)

    def extra_spec(arr):
        if arr.ndim == 2:
            return pl.BlockSpec((tm, tn), lambda i, j: (i, j))
        tiles_per_row = (m // arr.shape[0]) // tm
        return pl.BlockSpec((1, 1, tn), lambda i, j: (i // tiles_per_row, 0, j))

    in_specs = [
        pl.BlockSpec((None,) * len(xlead) + (tm, k), lambda i, j: tuple(xlead) + (i, 0)),
        pl.BlockSpec((None,) * len(wlead) + (k, tn), lambda i, j: tuple(wlead) + (0, j)),
    ] + [extra_spec(a) for a in extras]
    return _call(kern, grid=(m // tm, n // tn), in_specs=in_specs,
                 out_specs=pl.BlockSpec((tm, tn), lambda i, j: (i, j)),
                 out_shape=jax.ShapeDtypeStruct((m, n), out_dtype),
                 sem=("parallel", "parallel"), name=name)(x, w, *extras)


def _residual_epilogue(acc, x, gate):
    return x + gate[0] * acc


def _ada_kernel(c_ref, w_ref, b_ref, o_ref):
    o_ref[0] = _dot(_silu(c_ref[...]), w_ref[0]) + b_ref[0]


def _ada(cond8, ada_w, ada_b):
    depth, d, n = ada_w.shape
    tn = 1536
    return _call(_ada_kernel, grid=(depth, n // tn),
                 in_specs=[pl.BlockSpec((8, d), lambda l, j: (0, 0)),
                           pl.BlockSpec((1, d, tn), lambda l, j: (l, 0, j)),
                           pl.BlockSpec((1, 1, tn), lambda l, j: (l, 0, j))],
                 out_specs=pl.BlockSpec((1, 8, tn), lambda l, j: (l, 0, j)),
                 out_shape=jax.ShapeDtypeStruct((depth, 8, n), F32),
                 sem=("parallel", "parallel"), name="ada")(cond8, ada_w, ada_b.reshape(depth, 1, n))


def _normmod_kernel(x_ref, g_ref, sc_ref, sh_ref, o_ref):
    x = x_ref[...]
    y = x * lax.rsqrt(jnp.mean(x * x, axis=-1, keepdims=True) + NORM_EPS)
    o_ref[...] = (y * g_ref[...]) * (1.0 + sc_ref[0]) + sh_ref[0]


def _normmod(x, g, scale, shift, *, tm=512):
    m, d = x.shape
    tm = min(tm, m)
    tiles_per_row = (m // scale.shape[0]) // tm
    mod = pl.BlockSpec((1, 1, d), lambda i: (i // tiles_per_row, 0, 0))
    return _call(_normmod_kernel, grid=(m // tm,),
                 in_specs=[pl.BlockSpec((tm, d), lambda i: (i, 0)), pl.BlockSpec((1, d), lambda i: (0, 0)), mod, mod],
                 out_specs=pl.BlockSpec((tm, d), lambda i: (i, 0)),
                 out_shape=jax.ShapeDtypeStruct((m, d), F32), sem=("parallel",), name="normmod")(
                     x, g.reshape(1, d), scale, shift)


def _rmsnorm_kernel(x_ref, g_ref, o_ref):
    x = x_ref[...]
    o_ref[...] = x * lax.rsqrt(jnp.mean(x * x, axis=-1, keepdims=True) + NORM_EPS) * g_ref[...]


def _rmsnorm(x, g, *, tm=512):
    m, d = x.shape
    tm = min(tm, m)
    return _call(_rmsnorm_kernel, grid=(m // tm,),
                 in_specs=[pl.BlockSpec((tm, d), lambda i: (i, 0)), pl.BlockSpec((1, d), lambda i: (0, 0))],
                 out_specs=pl.BlockSpec((tm, d), lambda i: (i, 0)),
                 out_shape=jax.ShapeDtypeStruct((m, d), F32), sem=("parallel",), name="final_norm")(
                     x, g.reshape(1, d))


def _conv_kernel(x_ref, w_ref, o_ref, *, rows, width):
    x = x_ref[0]
    t_len = x.shape[0]
    t = lax.broadcasted_iota(jnp.int32, x.shape, 0)
    col = jnp.bitwise_and(t, width - 1)
    row = jnp.right_shift(t, width.bit_length() - 1)
    acc = jnp.zeros_like(x)
    for dr in (-1, 0, 1):
        if rows == 1 and dr != 0:
            continue
        for dc in (-1, 0, 1):
            off = dr * width + dc
            xs = x if off == 0 else pltpu.roll(x, (-off) % t_len, axis=0)
            ok = jnp.where(jnp.logical_and(col + dc >= 0, col + dc < width), 1.0, 0.0)
            if dr != 0:
                ok = ok * jnp.where(jnp.logical_and(row + dr >= 0, row + dr < rows), 1.0, 0.0)
            tap = (dr + 1) * 3 + (dc + 1)
            acc = acc + (xs * ok) * w_ref[tap:tap + 1, :]
    o_ref[0] = _silu(acc)


def _conv_silu(big, conv_w, *, rows, width, channels, tc=256):
    b, t_len, _ = big.shape
    assert width & (width - 1) == 0 and rows * width == t_len
    return _call(functools.partial(_conv_kernel, rows=rows, width=width), grid=(b, channels // tc),
                 in_specs=[pl.BlockSpec((1, t_len, tc), lambda i, j: (i, 0, j)),
                           pl.BlockSpec((9, tc), lambda i, j: (0, j))],
                 out_specs=pl.BlockSpec((1, t_len, tc), lambda i, j: (i, 0, j)),
                 out_shape=jax.ShapeDtypeStruct((b, t_len, channels), F32),
                 sem=("parallel", "parallel"), name="conv_silu")(big, conv_w.reshape(9, channels))


def _mlstm_gate_kernel(h_ref, wi_ref, wf_ref, bi_ref, bf_ref, gb_ref, gw_ref, ga_ref):
    nh, L = MLSTM_HEADS, MLSTM_CHUNK
    h = h_ref[0]
    tg = h.shape[0]
    ig = _dot(h, wi_ref[...]) + bi_ref[...]
    lf = _logsig(_dot(h, wf_ref[...]) + bf_ref[...])
    lane = lax.broadcasted_iota(jnp.int32, ig.shape, 1)
    b = jnp.where(lane < nh, _dot_f32(_tri(tg, "lower", L), lf), _dot_f32(_tri(tg, "upper", L), lf))
    b_last = _dot_f32(_tri(tg, "all", L), lf)
    gb_ref[0] = b
    gw_ref[0] = b_last - b + ig
    ga_ref[0] = ig - b


def _mlstm_gates(h3, w_gate, b_gate, *, tg=256):
    b, t_len, d = h3.shape
    nh = MLSTM_HEADS
    tg = min(tg, t_len)
    wi = _pad_cols(jnp.concatenate([w_gate[0, :, :nh], w_gate[1, :, :nh]], axis=1), LANES)
    wf = _pad_cols(jnp.concatenate([w_gate[0, :, nh:], w_gate[1, :, nh:]], axis=1), LANES)
    bi = _pad_cols(jnp.concatenate([b_gate[0, :nh], b_gate[1, :nh]]).reshape(1, 2 * nh), LANES)
    bf = _pad_cols(jnp.concatenate([b_gate[0, nh:], b_gate[1, nh:]]).reshape(1, 2 * nh), LANES)
    full = lambda shape: pl.BlockSpec(shape, lambda i, c: (0,) * len(shape))
    tok = pl.BlockSpec((1, tg, LANES), lambda i, c: (i, c, 0))
    return _call(_mlstm_gate_kernel, grid=(b, t_len // tg),
                 in_specs=[pl.BlockSpec((1, tg, d), lambda i, c: (i, c, 0)),
                           full((d, LANES)), full((d, LANES)), full((1, LANES)), full((1, LANES))],
                 out_specs=[tok, tok, tok], out_shape=[jax.ShapeDtypeStruct((b, t_len, LANES), F32)] * 3,
                 sem=("parallel", "parallel"), name="mlstm_gates")(h3, wi, wf, bi, bf)


def _mlstm_scan_kernel(qkf_ref, qkb_ref, vf_ref, vb_ref, gbf_ref, gbb_ref, gwf_ref, gwb_ref, gaf_ref, gab_ref,
                       c0_ref, n0_ref, m0_ref, yf_ref, yb_ref, cn_ref, nn_ref, mn_ref, c_s, n_s, m_s):
    nh, L, dk, dv = MLSTM_HEADS, MLSTM_CHUNK, MLSTM_DK, MLSTM_DV
    c = pl.program_id(1)

    @pl.when(c == 0)
    def _():
        c_s[...] = c0_ref[0]
        n_s[...] = n0_ref[0]
        m_s[...] = m0_ref[0]

    ti = lax.broadcasted_iota(jnp.int32, (L, L), 0)
    si = lax.broadcasted_iota(jnp.int32, (L, L), 1)
    lane = lax.broadcasted_iota(jnp.int32, (L, LANES), 1)
    ones = jnp.ones((L, LANES), F32)
    dirs = ((qkf_ref, vf_ref, gbf_ref, gwf_ref, gaf_ref, yf_ref), (qkb_ref, vb_ref, gbb_ref, gwb_ref, gab_ref, yb_ref))
    for d, (qk_ref, v_ref, gb_ref, gw_ref, ga_ref, y_ref) in enumerate(dirs):
        mask = (si <= ti) if d == 0 else (si >= ti)
        last = L - 1 if d == 0 else 0
        for h in range(nh):
            g = d * nh + h
            q = qk_ref[0, :, h * dk:(h + 1) * dk] * (dk ** -0.5)
            k = qk_ref[0, :, (nh + h) * dk:(nh + h + 1) * dk]
            v = v_ref[0, :, h * dv:(h + 1) * dv]
            a_row = lax.dot_general(ones, jnp.where(lane == g, ga_ref[0], 0.0), (((1,), (1,)), ((), ())),
                                    precision=HIGHEST, preferred_element_type=F32)
            b_col = gb_ref[0, :, g:g + 1]
            wl_col = gw_ref[0, :, g:g + 1]
            c_st, n_st, m_st = c_s[d, h], n_s[d, h], m_s[d, h]
            log_d = jnp.where(mask, b_col + a_row, -jnp.inf)
            m_t = jnp.maximum(b_col + m_st, jnp.max(log_d, axis=-1, keepdims=True))
            s = _dot_nt(q, k) * jnp.exp(log_d - m_t)
            w_inter = jnp.exp(b_col + m_st - m_t)
            num = _dot(s, v) + w_inter * _dot(q, c_st)
            den = jnp.sum(s, axis=-1, keepdims=True) + w_inter * jnp.sum(q * n_st, axis=-1, keepdims=True)
            y_ref[0, :, h * dv:(h + 1) * dv] = num / jnp.maximum(jnp.abs(den), jnp.exp(-m_t))
            b_last = b_col[last:last + 1, :]
            m_new = jnp.maximum(b_last + m_st, jnp.max(wl_col, axis=0, keepdims=True))
            ks = jnp.exp(wl_col - m_new) * k
            decay = jnp.exp(b_last + m_st - m_new)
            c_s[d, h] = decay * c_st + _dot_tn(ks, v)
            n_s[d, h] = decay * n_st + jnp.sum(ks, axis=0, keepdims=True)
            m_s[d, h] = m_new

    @pl.when(c == pl.num_programs(1) - 1)
    def _():
        cn_ref[0] = c_s[...]
        nn_ref[0] = n_s[...]
        mn_ref[0] = m_s[...]


def _mlstm_scan(qk, big, gb, gw, ga, c0, n0, m0):
    b, t_len, _ = qk.shape
    nh, L, dk, dv = MLSTM_HEADS, MLSTM_CHUNK, MLSTM_DK, MLSTM_DV
    nc = t_len // L
    fwd = lambda i, c: (i, c, 0)
    bwd = lambda i, c: (i, nc - 1 - c, 0)
    state = lambda shape: pl.BlockSpec((1,) + shape, lambda i, c: (i,) + (0,) * len(shape))
    st_shapes = [(2, nh, dk, dv), (2, nh, 1, dk), (2, nh, 1, 1)]
    return _call(
        _mlstm_scan_kernel, grid=(b, nc),
        in_specs=[pl.BlockSpec((1, L, 2 * nh * dk), fwd), pl.BlockSpec((1, L, 2 * nh * dk), bwd),
                  pl.BlockSpec((1, L, nh * dv), lambda i, c: (i, c, 1)),
                  pl.BlockSpec((1, L, nh * dv), lambda i, c: (i, nc - 1 - c, 1)),
                  pl.BlockSpec((1, L, LANES), fwd), pl.BlockSpec((1, L, LANES), bwd),
                  pl.BlockSpec((1, L, LANES), fwd), pl.BlockSpec((1, L, LANES), bwd),
                  pl.BlockSpec((1, L, LANES), fwd), pl.BlockSpec((1, L, LANES), bwd)] + [state(s) for s in st_shapes],
        out_specs=[pl.BlockSpec((1, L, nh * dv), fwd), pl.BlockSpec((1, L, nh * dv), bwd)] + [state(s) for s in st_shapes],
        out_shape=[jax.ShapeDtypeStruct((b, t_len, nh * dv), F32)] * 2 + [jax.ShapeDtypeStruct((b,) + s, F32) for s in st_shapes],
        scratch=[pltpu.VMEM(s, F32) for s in st_shapes],
        sem=("parallel", "arbitrary"), name="mlstm_scan")(qk, qk, big, big, gb, gb, gw, gw, ga, ga, c0, n0, m0)


def _mlstm_post_kernel(yf_ref, yb_ref, o_ref, g_ref, out_ref):
    dv = MLSTM_DV
    for h in range(MLSTM_HEADS):
        sl = slice(h * dv, (h + 1) * dv)
        y = yf_ref[:, sl] + yb_ref[:, sl]
        y = y - jnp.mean(y, axis=-1, keepdims=True)
        y = y * lax.rsqrt(jnp.mean(y * y, axis=-1, keepdims=True) + NORM_EPS)
        out_ref[:, sl] = _sigmoid(o_ref[:, sl]) * (y * g_ref[:, sl])


def _mlstm_post(yf, yb, big, norm_g, *, tm=512):
    m, d = yf.shape
    tm = min(tm, m)
    row = lambda i: (i, 0)
    return _call(_mlstm_post_kernel, grid=(m // tm,),
                 in_specs=[pl.BlockSpec((tm, d), row), pl.BlockSpec((tm, d), row),
                           pl.BlockSpec((tm, d), lambda i: (i, 2)), pl.BlockSpec((1, d), lambda i: (0, 0))],
                 out_specs=pl.BlockSpec((tm, d), row), out_shape=jax.ShapeDtypeStruct((m, d), F32),
                 sem=("parallel",), name="mlstm_post")(yf, yb, big, norm_g.reshape(1, d))


def _mlstm_mixer(h, b, rows, init, w_in, conv_w, w_gate, b_gate, norm_g, w_out, x, gate1):
    m, d = h.shape
    t_len = m // b
    nh, dk = MLSTM_HEADS, MLSTM_DK
    big = _mm(h, w_in, tm=512, tn=1024, name="mlstm_in")
    big3 = big.reshape(b, t_len, 3 * d)
    qk = _conv_silu(big3, conv_w, rows=rows, width=t_len // rows, channels=2 * nh * dk)
    gb, gw, ga = _mlstm_gates(h.reshape(b, t_len, d), w_gate, b_gate)
    c0, n0, m0 = init
    yf, yb, cn, nn, mn = _mlstm_scan(qk, big3, gb, gw, ga, c0, n0.reshape(b, 2, nh, 1, dk), m0.reshape(b, 2, nh, 1, 1))
    y = _mlstm_post(yf.reshape(m, d), yb.reshape(m, d), big, norm_g)
    x = _mm(y, w_out, tm=512, tn=1024, extras=[x, gate1],
            epilogue=_residual_epilogue, name="mlstm_out")
    return x, (cn, nn.reshape(b, 2, nh, dk), mn.reshape(b, 2, nh))


def _gla_gate_kernel(h_ref, w1_ref, w2_ref, b_ref, gf_ref, gb_ref):
    half = GLA_HEADS * GLA_DK
    h = h_ref[0]
    z = _dot(_dot(h, w1_ref[...]), w2_ref[...]) + b_ref[...]
    lg = _logsig(z) * (1.0 / GLA_TAU)
    tg = h.shape[0]
    gf_ref[0] = _dot_f32(_tri(tg, "lower", GLA_CHUNK), lg[:, :half])
    gb_ref[0] = _dot_f32(_tri(tg, "upper", GLA_CHUNK), lg[:, half:])


def _gla_gates(h3, w_a1, w_a2, b_a, *, tg=256):
    b, t_len, d = h3.shape
    rank, half = w_a1.shape[-1], w_a2.shape[-1]
    w1 = jnp.zeros((d, LANES), F32).at[:, :rank].set(w_a1[0]).at[:, rank:2 * rank].set(w_a1[1])
    w2 = jnp.zeros((LANES, 2 * half), F32).at[:rank, :half].set(w_a2[0]).at[rank:2 * rank, half:].set(w_a2[1])
    bias = jnp.concatenate([b_a[0], b_a[1]]).reshape(1, 2 * half)
    tg = min(tg, t_len)
    full = lambda shape: pl.BlockSpec(shape, lambda i, c: (0,) * len(shape))
    tok = lambda n: pl.BlockSpec((1, tg, n), lambda i, c: (i, c, 0))
    return _call(_gla_gate_kernel, grid=(b, t_len // tg),
                 in_specs=[tok(d), full((d, LANES)), full((LANES, 2 * half)), full((1, 2 * half))],
                 out_specs=[tok(half), tok(half)],
                 out_shape=[jax.ShapeDtypeStruct((b, t_len, half), F32)] * 2,
                 sem=("parallel", "parallel"), name="gla_gates")(h3, w1, w2, bias)


def _gla_scan_kernel(xf_ref, xb_ref, gf_ref, gb_ref, s0_ref, yf_ref, yb_ref, sn_ref, s_s):
    nh, L, dk, dv = GLA_HEADS, GLA_CHUNK, GLA_DK, GLA_DV
    c = pl.program_id(1)

    @pl.when(c == 0)
    def _():
        s_s[...] = s0_ref[0]

    tcol = lax.broadcasted_iota(jnp.int32, (L, 1), 0)
    eye = jnp.where(lax.broadcasted_iota(jnp.int32, (dk, dk), 0) == lax.broadcasted_iota(jnp.int32, (dk, dk), 1), 1.0, 0.0)
    for d, (x_ref, g_ref, y_ref) in enumerate(((xf_ref, gf_ref, yf_ref), (xb_ref, gb_ref, yb_ref))):
        last = L - 1 if d == 0 else 0
        for h in range(nh):
            q = x_ref[0, :, h * dk:(h + 1) * dk] * (dk ** -0.5)
            k = x_ref[0, :, (nh + h) * dk:(nh + h + 1) * dk]
            v = x_ref[0, :, 2 * nh * dk + h * dv:2 * nh * dk + (h + 1) * dv]
            g = g_ref[0, :, h * dk:(h + 1) * dk]
            s_st = s_s[d, h]
            o = _dot(q * jnp.exp(g), s_st)
            for s in range(L):
                dec = jnp.exp(jnp.minimum(g - g[s:s + 1, :], 0.0))
                col = jnp.sum(q * k[s:s + 1, :] * dec, axis=-1, keepdims=True)
                col = jnp.where((tcol >= s) if d == 0 else (tcol <= s), col, 0.0)
                o = o + col * v[s:s + 1, :]
            y_ref[0, :, h * dv:(h + 1) * dv] = o
            g_last = g[last:last + 1, :]
            decay_col = jnp.sum(eye * jnp.exp(g_last), axis=-1, keepdims=True)
            s_s[d, h] = decay_col * s_st + _dot_tn(k * jnp.exp(g_last - g), v)

    @pl.when(c == pl.num_programs(1) - 1)
    def _():
        sn_ref[0] = s_s[...]


def _gla_scan(qkv, gf, gb, s0):
    b, t_len, width = qkv.shape
    nh, L, dk, dv = GLA_HEADS, GLA_CHUNK, GLA_DK, GLA_DV
    nc = t_len // L
    fwd = lambda i, c: (i, c, 0)
    bwd = lambda i, c: (i, nc - 1 - c, 0)
    st = (2, nh, dk, dv)
    state = pl.BlockSpec((1,) + st, lambda i, c: (i, 0, 0, 0, 0))
    return _call(
        _gla_scan_kernel, grid=(b, nc),
        in_specs=[pl.BlockSpec((1, L, width), fwd), pl.BlockSpec((1, L, width), bwd),
                  pl.BlockSpec((1, L, nh * dk), fwd), pl.BlockSpec((1, L, nh * dk), bwd), state],
        out_specs=[pl.BlockSpec((1, L, nh * dv), fwd), pl.BlockSpec((1, L, nh * dv), bwd), state],
        out_shape=[jax.ShapeDtypeStruct((b, t_len, nh * dv), F32)] * 2 + [jax.ShapeDtypeStruct((b,) + st, F32)],
        scratch=[pltpu.VMEM(st, F32)], sem=("parallel", "arbitrary"), name="gla_scan")(qkv, qkv, gf, gb, s0)


def _gla_post_kernel(yf_ref, yb_ref, r_ref, g_ref, out_ref):
    dv = GLA_DV
    for h in range(GLA_HEADS):
        sl = slice(h * dv, (h + 1) * dv)
        y = yf_ref[:, sl] + yb_ref[:, sl]
        y = y * lax.rsqrt(jnp.mean(y * y, axis=-1, keepdims=True) + NORM_EPS)
        out_ref[:, sl] = (y * g_ref[:, sl]) * _silu(r_ref[:, sl])


def _gla_post(yf, yb, big, norm_g, *, tm=512):
    m, d = yf.shape
    tm = min(tm, m)
    row = lambda i: (i, 0)
    return _call(_gla_post_kernel, grid=(m // tm,),
                 in_specs=[pl.BlockSpec((tm, d), row), pl.BlockSpec((tm, d), row),
                           pl.BlockSpec((tm, d), lambda i: (i, 2)), pl.BlockSpec((1, d), lambda i: (0, 0))],
                 out_specs=pl.BlockSpec((tm, d), row), out_shape=jax.ShapeDtypeStruct((m, d), F32),
                 sem=("parallel",), name="gla_post")(yf, yb, big, norm_g.reshape(1, d))


def _gla_mixer(h, b, rows, init, w_in, conv_w, w_a1, w_a2, b_a, norm_g, w_out, x, gate1):
    m, d = h.shape
    t_len = m // b
    big = _mm(h, w_in, tm=512, tn=1024, name="gla_in")
    big3 = big.reshape(b, t_len, 3 * d)
    qkv = _conv_silu(big3, conv_w, rows=rows, width=t_len // rows, channels=2 * d)
    gf, gb = _gla_gates(h.reshape(b, t_len, d), w_a1, w_a2, b_a)
    yf, yb, sn = _gla_scan(qkv, gf, gb, init[0])
    y = _gla_post(yf.reshape(m, d), yb.reshape(m, d), big, norm_g)
    x = _mm(y, w_out, tm=512, tn=1024, extras=[x, gate1],
            epilogue=_residual_epilogue, name="gla_out")
    return x, (sn,)


def _rwkv_mix_kernel(h_ref, hp_ref, hn_ref, mu_ref, o_ref):
    i = pl.program_id(1)
    h = h_ref[0]
    tm = h.shape[0]
    prev_row = jnp.where(i > 0, hp_ref[0, 7:8, :], 0.0)
    next_row = jnp.where(i < pl.num_programs(1) - 1, hn_ref[0, 0:1, :], 0.0)
    row = lax.broadcasted_iota(jnp.int32, (tm, 1), 0)
    x_prev = jnp.where(row == 0, prev_row, pltpu.roll(h, 1, axis=0))
    x_next = jnp.where(row == tm - 1, next_row, pltpu.roll(h, tm - 1, axis=0))
    xx = 0.5 * (x_prev + x_next) - h
    for j in range(6):
        o_ref[j, 0] = (h + xx * mu_ref[j:j + 1, :]).astype(BF16)


def _rwkv_mix(h3, mu, *, tm=256):
    b, t_len, d = h3.shape
    tm = min(tm, t_len)
    nt, n8 = t_len // tm, t_len // 8
    return _call(_rwkv_mix_kernel, grid=(b, nt),
                 in_specs=[pl.BlockSpec((1, tm, d), lambda i, j: (i, j, 0)),
                           pl.BlockSpec((1, 8, d), lambda i, j: (i, jnp.maximum(j * (tm // 8) - 1, 0), 0)),
                           pl.BlockSpec((1, 8, d), lambda i, j: (i, jnp.minimum((j + 1) * (tm // 8), n8 - 1), 0)),
                           pl.BlockSpec((6, d), lambda i, j: (0, 0))],
                 out_specs=pl.BlockSpec((6, 1, tm, d), lambda i, j: (0, i, j, 0)),
                 out_shape=jax.ShapeDtypeStruct((6, b, t_len, d), BF16),
                 sem=("parallel", "parallel"), name="rwkv_mix")(h3, h3, h3, mu)


def _segsum(x):
    lo = lax.broadcasted_iota(jnp.int32, x.shape, 1) < RWKV_N
    s0 = jnp.sum(jnp.where(lo, x, 0.0), axis=-1, keepdims=True)
    s1 = jnp.sum(jnp.where(lo, 0.0, x), axis=-1, keepdims=True)
    return jnp.where(lo, s0, s1)


def _rwkv_prep_kernel(r_ref, k_ref, v_ref, lora_ref, ap_ref, w0_ref, a0_ref, kk_w_ref, ka_w_ref, rk_w_ref,
                      kk_ref, kb_ref, k2_ref, bonus_ref, lw_ref):
    d = D_MODEL
    for j in range(d // LANES):
        sl = slice(j * LANES, (j + 1) * LANES)
        r, k, v = r_ref[:, sl], k_ref[:, sl], v_ref[:, sl]
        a = _sigmoid(a0_ref[:, sl] + ap_ref[:, sl])
        kk = k * kk_w_ref[:, sl]
        kk = kk / jnp.maximum(jnp.sqrt(_segsum(kk * kk)), 1e-12)
        k2 = k * (1.0 + (a - 1.0) * ka_w_ref[:, sl])
        kk_ref[:, sl] = kk
        kb_ref[:, sl] = kk * a
        k2_ref[:, sl] = k2
        bonus_ref[:, sl] = _segsum(r * k2 * rk_w_ref[:, sl]) * v
        for z in range(2):
            lw_ref[z, :, sl] = -jnp.exp(_logsig(w0_ref[z:z + 1, sl] + lora_ref[:, z * d + j * LANES:z * d + (j + 1) * LANES]) - 0.5)


def _rwkv_prep(r, k, v, lora, a_pre, w0, a0, k_k, k_a, r_k, *, tm=256):
    m, d = r.shape
    tm = min(tm, m)
    row = lambda i: (i, 0)
    tok = pl.BlockSpec((tm, d), row)
    par = pl.BlockSpec((1, d), lambda i: (0, 0))
    return _call(_rwkv_prep_kernel, grid=(m // tm,),
                 in_specs=[tok, tok, tok, pl.BlockSpec((tm, 2 * d), row), tok,
                           pl.BlockSpec((2, d), lambda i: (0, 0)), par, par, par, par],
                 out_specs=[tok, tok, tok, tok, pl.BlockSpec((2, tm, d), lambda i: (0, i, 0))],
                 out_shape=[jax.ShapeDtypeStruct((m, d), F32)] * 4 + [jax.ShapeDtypeStruct((2, m, d), F32)],
                 sem=("parallel",), name="rwkv_prep")(
                     r, k, v, lora, a_pre, w0, a0.reshape(1, d), k_k.reshape(1, d), k_a.reshape(1, d), r_k.reshape(1, d))


def _rwkv_scan_kernel(rf, kf, vf, kkf, kbf, lwf, rb, kb_, vb, kkb, kbb, lwb, s0_ref, yf_ref, yb_ref, sn_ref, s_s):
    L, pairs = RWKV_CHUNK, RWKV_PAIRS
    c = pl.program_id(1)

    @pl.when(c == 0)
    def _():
        s_s[...] = s0_ref[0]

    ti = lax.broadcasted_iota(jnp.int32, (L, L), 0)
    ui = lax.broadcasted_iota(jnp.int32, (L, L), 1)
    lo = lax.broadcasted_iota(jnp.int32, (L, LANES), 1) < RWKV_N
    sq_r = lax.broadcasted_iota(jnp.int32, (LANES, LANES), 0) < RWKV_N
    sq_c = lax.broadcasted_iota(jnp.int32, (LANES, LANES), 1) < RWKV_N
    same_head = sq_r == sq_c
    dirs = ((rf, kf, vf, kkf, kbf, lwf, yf_ref), (rb, kb_, vb, kkb, kbb, lwb, yb_ref))
    for d, (r_ref, k_ref, v_ref, kk_ref, kb_ref, lw_ref, y_ref) in enumerate(dirs):
        strict = (ui < ti) if d == 0 else (ui > ti)
        incl = (ui <= ti) if d == 0 else (ui >= ti)
        tri = _tri(L, "lower" if d == 0 else "upper")
        last = L - 1 if d == 0 else 0
        order = range(L) if d == 0 else range(L - 1, -1, -1)
        for p in range(pairs):
            sl = slice(p * LANES, (p + 1) * LANES)
            r, k, v, kk, kb = r_ref[0, :, sl], k_ref[0, :, sl], v_ref[0, :, sl], kk_ref[0, :, sl], kb_ref[0, :, sl]
            lw = lw_ref[0, 0, :, sl]
            s_st = s_s[d, p]
            cum = _dot_f32(tri, lw)
            c_last = cum[last:last + 1, :]
            e_neg = jnp.exp(-cum)
            e_end = jnp.exp(c_last - cum)
            rt, kt = r * jnp.exp(cum), kk * jnp.exp(cum - lw)
            kh, bh = k * e_neg, kb * e_neg
            m_kb, m_rb, kkv, rkv = [], [], [], []
            for half in (lo, jnp.logical_not(lo)):
                ktj, rtj = jnp.where(half, kt, 0.0), jnp.where(half, rt, 0.0)
                m_kb.append(jnp.where(strict, _dot_nt(ktj, bh), 0.0))
                m_rb.append(jnp.where(incl, _dot_nt(rtj, bh), 0.0))
                kkv.append(_dot(jnp.where(strict, _dot_nt(ktj, kh), 0.0), v))
                rkv.append(_dot(jnp.where(incl, _dot_nt(rtj, kh), 0.0), v))
            w1, w2 = kt, jnp.where(lo, kkv[0], kkv[1])
            for u in order:
                coef = jnp.where(lo, m_kb[0][:, u:u + 1], m_kb[1][:, u:u + 1])
                w1 = w1 - coef * w1[u:u + 1, :]
                w2 = w2 - coef * w2[u:u + 1, :]
            sa = _dot_nt(w1, s_st) + w2
            y = _dot_nt(rt, s_st) + jnp.where(lo, rkv[0] - _dot(m_rb[0], sa), rkv[1] - _dot(m_rb[1], sa))
            y_ref[0, :, sl] = y
            s_new = s_st * jnp.exp(c_last) + _dot_tn(v, k * e_end) - _dot_tn(sa, kb * e_end)
            s_s[d, p] = jnp.where(same_head, s_new, 0.0)

    @pl.when(c == pl.num_programs(1) - 1)
    def _():
        sn_ref[0] = s_s[...]


def _rwkv_scan(r, k2, v, kk, kb, lw, s0):
    b, t_len, d = r.shape
    L, pairs = RWKV_CHUNK, RWKV_PAIRS
    nc = t_len // L
    fwd = pl.BlockSpec((1, L, d), lambda i, c: (i, c, 0))
    bwd = pl.BlockSpec((1, L, d), lambda i, c: (i, nc - 1 - c, 0))
    st = (2, pairs, LANES, LANES)
    state = pl.BlockSpec((1,) + st, lambda i, c: (i, 0, 0, 0, 0))
    return _call(
        _rwkv_scan_kernel, grid=(b, nc),
        in_specs=[fwd] * 5 + [pl.BlockSpec((1, 1, L, d), lambda i, c: (0, i, c, 0))]
        + [bwd] * 5 + [pl.BlockSpec((1, 1, L, d), lambda i, c: (1, i, nc - 1 - c, 0)), state],
        out_specs=[fwd, bwd, state],
        out_shape=[jax.ShapeDtypeStruct((b, t_len, d), F32)] * 2 + [jax.ShapeDtypeStruct((b,) + st, F32)],
        scratch=[pltpu.VMEM(st, F32)], sem=("parallel", "arbitrary"), name="rwkv_scan")(
            r, k2, v, kk, kb, lw, r, k2, v, kk, kb, lw, s0)


def _rwkv_post_kernel(yf_ref, yb_ref, bonus_ref, g_ref, lng_ref, lnb_ref, out_ref):
    for j in range(D_MODEL // LANES):
        sl = slice(j * LANES, (j + 1) * LANES)
        y = yf_ref[:, sl] + yb_ref[:, sl]
        y = y - _segsum(y) * (1.0 / RWKV_N)
        y = y * lax.rsqrt(_segsum(y * y) * (1.0 / RWKV_N) + RWKV_GN_EPS)
        out_ref[:, sl] = (y * lng_ref[:, sl] + lnb_ref[:, sl] + bonus_ref[:, sl]) * g_ref[:, sl]


def _rwkv_post(yf, yb, bonus, g, ln_g, ln_b, *, tm=256):
    m, d = yf.shape
    tm = min(tm, m)
    tok = pl.BlockSpec((tm, d), lambda i: (i, 0))
    par = pl.BlockSpec((1, d), lambda i: (0, 0))
    return _call(_rwkv_post_kernel, grid=(m // tm,), in_specs=[tok, tok, tok, tok, par, par], out_specs=tok,
                 out_shape=jax.ShapeDtypeStruct((m, d), F32), sem=("parallel",), name="rwkv_post")(
                     yf, yb, bonus, g, ln_g.reshape(1, d), ln_b.reshape(1, d))


def _pad_cols(w, n):
    return jnp.zeros(w.shape[:-1] + (n,), w.dtype).at[..., :w.shape[-1]].set(w)


def _pad_rows(w, n):
    return jnp.zeros((n,) + w.shape[1:], w.dtype).at[:w.shape[0]].set(w)


def _rwkv_mixer(h, b, init, j, mu, w_rkv, w0, w1, w2, a0, a1, a2, g1, g2, k_k, k_a, r_k, ln_g, ln_b, w_out, x, gate1):
    m, d = h.shape
    t_len = m // b
    xmix = _rwkv_mix(h.reshape(b, t_len, d), mu[j]).reshape(6, m, d)
    proj = lambda idx, w, wlead, name, **kw: _mm(xmix, w, xlead=(idx,), wlead=wlead, tm=512, tn=1024, name=name, **kw)
    r = proj(0, w_rkv, (j, 0), "rwkv_r")
    k = proj(2, w_rkv, (j, 1), "rwkv_k")
    v = proj(3, w_rkv, (j, 2), "rwkv_v")
    rank_w = w1.shape[-1]
    w1cat = jnp.concatenate([w1[j, 0], w1[j, 1]], axis=1)
    w2bd = jnp.zeros((2 * rank_w, 2 * d), F32).at[:rank_w, :d].set(w2[j, 0]).at[rank_w:, d:].set(w2[j, 1])
    tw = proj(1, w1cat, (), "rwkv_w1", epilogue=jnp.tanh)
    lora = _mm(tw, w2bd, tm=512, tn=1024, name="rwkv_w2")
    a_pre = _mm(proj(4, _pad_cols(a1[j], LANES), (), "rwkv_a1"), _pad_rows(a2[j], LANES), tm=512, tn=1024, name="rwkv_a2")
    g = _mm(proj(5, g1[j], (), "rwkv_g1", epilogue=_sigmoid), g2[j], tm=512, tn=1024, name="rwkv_g2")
    kk, kb, k2, bonus, lw = _rwkv_prep(r, k, v, lora, a_pre, w0[j], a0[j], k_k[j], k_a[j], r_k[j].reshape(d))
    s3 = lambda z: z.reshape(b, t_len, d)
    yf, yb, sn = _rwkv_scan(s3(r), s3(k2), s3(v), s3(kk), s3(kb), lw.reshape(2, b, t_len, d), init)
    y = _rwkv_post(yf.reshape(m, d), yb.reshape(m, d), bonus, g, ln_g[j], ln_b[j])
    x = _mm(y, w_out, wlead=(j,), tm=512, tn=1024, extras=[x, gate1],
            epilogue=_residual_epilogue, name="rwkv_out")
    return x, sn


def _rwkv_state_to_pairs(s):
    b = s.shape[0]
    n = RWKV_N
    out = jnp.zeros((b, 2, RWKV_PAIRS, 2 * n, 2 * n), F32)
    return out.at[..., :n, :n].set(s[:, :, 0::2]).at[..., n:, n:].set(s[:, :, 1::2])


def _rwkv_state_from_pairs(sp):
    n = RWKV_N
    b = sp.shape[0]
    return jnp.stack([sp[..., :n, :n], sp[..., n:, n:]], axis=3).reshape(b, 2, RWKV_HEADS, n, n)


def _router_kernel(x_ref, w_ref, b_ref, o_ref):
    e0, per = ROUTER_EXPERT_LANE0, MOE_PER_GROUP
    logits = _dot_f32(x_ref[...], w_ref[...]) + b_ref[...]
    lane = lax.broadcasted_iota(jnp.int32, logits.shape, 1).astype(F32)
    big = float(LANES)
    rmax = lambda z: jnp.max(z, axis=-1, keepdims=True)
    rsum = lambda z: jnp.sum(z, axis=-1, keepdims=True)
    first = lambda m: jnp.min(jnp.where(m, lane, big), axis=-1, keepdims=True)
    is_g = lane < MOE_GROUPS
    gmax = rmax(jnp.where(is_g, logits, -jnp.inf))
    gsel = first(jnp.logical_and(is_g, logits >= gmax))
    gw = 1.0 / rsum(jnp.where(is_g, jnp.exp(logits - gmax), 0.0))
    in_grp = jnp.logical_and(lane >= e0 + gsel * per, lane < e0 + (gsel + 1) * per)
    emax = rmax(jnp.where(in_grp, logits, -jnp.inf))
    p = jnp.where(in_grp, jnp.exp(logits - emax), 0.0)
    p = p / rsum(p)
    p1 = rmax(jnp.where(in_grp, p, -1.0))
    i1 = first(jnp.logical_and(in_grp, p >= p1))
    rest = jnp.logical_and(in_grp, lane != i1)
    p2 = rmax(jnp.where(rest, p, -1.0))
    i2 = first(jnp.logical_and(rest, p >= p2))
    tot = p1 + p2
    o_ref[...] = jnp.where(lane == i1, p1 / tot * gw, jnp.where(lane == i2, p2 / tot * gw, 0.0))


def _router(x, w_group, b_group, w_expert, b_expert, *, tm=512):
    m, d = x.shape
    tm = min(tm, m)
    e0 = ROUTER_EXPERT_LANE0
    w = jnp.zeros((d, LANES), F32).at[:, :MOE_GROUPS].set(w_group).at[:, e0:e0 + MOE_EXPERTS].set(w_expert)
    bias = jnp.zeros((1, LANES), F32).at[0, :MOE_GROUPS].set(b_group).at[0, e0:e0 + MOE_EXPERTS].set(b_expert)
    return _call(_router_kernel, grid=(m // tm,),
                 in_specs=[pl.BlockSpec((tm, d), lambda i: (i, 0)), pl.BlockSpec((d, LANES), lambda i: (0, 0)),
                           pl.BlockSpec((1, LANES), lambda i: (0, 0))],
                 out_specs=pl.BlockSpec((tm, LANES), lambda i: (i, 0)),
                 out_shape=jax.ShapeDtypeStruct((m, LANES), F32), sem=("parallel",), name="moe_router")(x, w, bias)


def _moe_ffn_kernel(h_ref, comb_ref, wg_ref, wu_ref, wd_ref, x_ref, gate_ref, o_ref, acc):
    e = pl.program_id(1)

    @pl.when(e == 0)
    def _():
        acc[...] = jnp.zeros_like(acc)

    h = h_ref[...].astype(BF16)
    comb = comb_ref[...]
    lane = lax.broadcasted_iota(jnp.int32, comb.shape, 1)
    ce = jnp.sum(jnp.where(lane == e + ROUTER_EXPERT_LANE0, comb, 0.0), axis=-1, keepdims=True)
    hid = _silu(_dot(h, wg_ref[0, 0])) * _dot(h, wu_ref[0, 0])
    acc[...] += _dot(hid * ce, wd_ref[0, 0])

    @pl.when(e == pl.num_programs(1) - 1)
    def _():
        o_ref[...] = x_ref[...] + gate_ref[0] * acc[...]


def _moe_ffn(h, comb, w_gate, w_up, w_down, layer, x, gate2, *, tm=512):
    m, d = h.shape
    tm = min(tm, m)
    f = w_gate.shape[-1]
    tiles_per_row = (m // gate2.shape[0]) // tm
    tok = pl.BlockSpec((tm, d), lambda i, e: (i, 0))
    return _call(_moe_ffn_kernel, grid=(m // tm, MOE_EXPERTS),
                 in_specs=[tok, pl.BlockSpec((tm, LANES), lambda i, e: (i, 0)),
                           pl.BlockSpec((1, 1, d, f), lambda i, e: (layer, e, 0, 0)),
                           pl.BlockSpec((1, 1, d, f), lambda i, e: (layer, e, 0, 0)),
                           pl.BlockSpec((1, 1, f, d), lambda i, e: (layer, e, 0, 0)),
                           tok, pl.BlockSpec((1, 1, d), lambda i, e: (i // tiles_per_row, 0, 0))],
                 out_specs=tok, out_shape=jax.ShapeDtypeStruct((m, d), F32),
                 scratch=[pltpu.VMEM((tm, d), F32)], sem=("parallel", "arbitrary"), name="moe_ffn")(
                     h, comb, w_gate, w_up, w_down, x, gate2)


def _run_trunk(x3, mods, rows, init_a, init_b, init_c, p):
    b, t_len, d = x3.shape
    m = b * t_len
    x = x3.reshape(m, d)
    new_a, new_b, new_c = [], [], []
    for i in range(p["ada_w"].shape[0]):
        j = i // 3
        shift1, scale1, gate1, shift2, scale2, gate2 = mods[i]
        h = _normmod(x, p["norm1_g"][i], scale1, shift1)
        if i % 3 == 0:
            x, st = _mlstm_mixer(h, b, rows, tuple(s[:, j] for s in init_a), p["mlstm_w_in"][j], p["mlstm_conv_w"][j],
                                 p["mlstm_w_gate"][j], p["mlstm_b_gate"][j], p["mlstm_norm_g"][j], p["mlstm_w_out"][j],
                                 x, gate1)
            new_a.append(st)
        elif i % 3 == 1:
            x, st = _gla_mixer(h, b, rows, tuple(s[:, j] for s in init_b), p["gla_w_in"][j], p["gla_conv_w"][j],
                               p["gla_w_a1"][j], p["gla_w_a2"][j], p["gla_b_a"][j], p["gla_norm_g"][j], p["gla_w_out"][j],
                               x, gate1)
            new_b.append(st)
        else:
            x, st = _rwkv_mixer(h, b, _rwkv_state_to_pairs(init_c[0][:, j]), j, p["rwkv_mu"], p["rwkv_w_rkv"], p["rwkv_w0"],
                                p["rwkv_w1"], p["rwkv_w2"], p["rwkv_a0"], p["rwkv_a1"], p["rwkv_a2"], p["rwkv_g1"],
                                p["rwkv_g2"], p["rwkv_k_k"], p["rwkv_k_a"], p["rwkv_r_k"], p["rwkv_ln_g"], p["rwkv_ln_b"],
                                p["rwkv_w_out"], x, gate1)
            new_c.append((_rwkv_state_from_pairs(st),))
        hff = _normmod(x, p["norm2_g"][i], scale2, shift2)
        comb = _router(hff, p["moe_w_group"][i], p["moe_b_group"][i], p["moe_w_expert"][i], p["moe_b_expert"][i])
        x = _moe_ffn(hff, comb, p["moe_w_gate"], p["moe_w_up"], p["moe_w_down"], i, x, gate2)
    stack = lambda per_layer: tuple(jnp.stack(parts, axis=1) for parts in zip(*per_layer))
    y = _rmsnorm(x, p["final_norm_g"]).reshape(b, t_len, d)
    return y, stack(new_a), stack(new_b), stack(new_c)


def kernel(x_prompt, x_sample, state_mlstm_C, state_mlstm_n, state_mlstm_m, state_gla_S, state_rwkv_S, c, c_ctx, ada_w, ada_b, norm1_g, norm2_g, moe_w_group, moe_b_group, moe_w_expert, moe_b_expert, moe_w_gate, moe_w_up, moe_w_down, final_norm_g, mlstm_w_in, mlstm_conv_w, mlstm_w_gate, mlstm_b_gate, mlstm_norm_g, mlstm_w_out, gla_w_in, gla_conv_w, gla_w_a1, gla_w_a2, gla_b_a, gla_norm_g, gla_w_out, rwkv_mu, rwkv_w_rkv, rwkv_w0, rwkv_w1, rwkv_w2, rwkv_a0, rwkv_a1, rwkv_a2, rwkv_g1, rwkv_g2, rwkv_k_k, rwkv_k_a, rwkv_r_k, rwkv_ln_g, rwkv_ln_b, rwkv_w_out):
    p = dict(ada_w=ada_w, norm1_g=norm1_g, norm2_g=norm2_g, moe_w_group=moe_w_group, moe_b_group=moe_b_group,
             moe_w_expert=moe_w_expert, moe_b_expert=moe_b_expert, moe_w_gate=moe_w_gate, moe_w_up=moe_w_up,
             moe_w_down=moe_w_down, final_norm_g=final_norm_g, mlstm_w_in=mlstm_w_in, mlstm_conv_w=mlstm_conv_w,
             mlstm_w_gate=mlstm_w_gate, mlstm_b_gate=mlstm_b_gate, mlstm_norm_g=mlstm_norm_g, mlstm_w_out=mlstm_w_out,
             gla_w_in=gla_w_in, gla_conv_w=gla_conv_w, gla_w_a1=gla_w_a1, gla_w_a2=gla_w_a2, gla_b_a=gla_b_a,
             gla_norm_g=gla_norm_g, gla_w_out=gla_w_out, rwkv_mu=rwkv_mu, rwkv_w_rkv=rwkv_w_rkv, rwkv_w0=rwkv_w0,
             rwkv_w1=rwkv_w1, rwkv_w2=rwkv_w2, rwkv_a0=rwkv_a0, rwkv_a1=rwkv_a1, rwkv_a2=rwkv_a2, rwkv_g1=rwkv_g1,
             rwkv_g2=rwkv_g2, rwkv_k_k=rwkv_k_k, rwkv_k_a=rwkv_k_a, rwkv_r_k=rwkv_r_k, rwkv_ln_g=rwkv_ln_g,
             rwkv_ln_b=rwkv_ln_b, rwkv_w_out=rwkv_w_out)
    depth, d = ada_w.shape[0], ada_w.shape[1]
    n_dec = c.shape[0]
    cond8 = jnp.zeros((8, d), F32).at[0].set(c_ctx).at[1:1 + n_dec].set(c)
    mod = _ada(cond8, ada_w, ada_b)
    split = lambda rows: [tuple(rows[i][:, None, k * d:(k + 1) * d] for k in range(6)) for i in range(depth)]
    mods_ctx = split(mod[:, 0:1])
    mods_dec = split(mod[:, 1:1 + n_dec])

    bp = x_prompt.shape[0]
    zeros_like_ctx = lambda s: jnp.zeros((bp,) + s.shape[1:], F32)
    y_prompt, st_a, st_b, st_c = _run_trunk(
        x_prompt, mods_ctx, 1, tuple(zeros_like_ctx(s) for s in (state_mlstm_C, state_mlstm_n, state_mlstm_m)),
        (zeros_like_ctx(state_gla_S),), (zeros_like_ctx(state_rwkv_S),), p)
    y_sample = _run_trunk(x_sample, mods_dec, x_sample.shape[1] // GRID_W,
                          (state_mlstm_C, state_mlstm_n, state_mlstm_m), (state_gla_S,), (state_rwkv_S,), p)[0]
    return (y_prompt, y_sample, st_a[0], st_a[1], st_a[2], st_b[0], st_c[0])
```

```python
import functools

import jax
import jax.numpy as jnp
from jax import lax
from jax.experimental import pallas as pl
from jax.experimental.pallas import tpu as pltpu

F32 = jnp.float32
BF16 = jnp.bfloat16
HIGHEST = lax.Precision.HIGHEST

D_MODEL = 1024
GRID_W = 64
NORM_EPS = 1e-6

MLSTM_HEADS = 4
MLSTM_DK = 128
MLSTM_DV = 256
MLSTM_CHUNK = 64

GLA_HEADS = 4
GLA_DK = 128
GLA_DV = 256
GLA_TAU = 16.0
GLA_CHUNK = 32

RWKV_N = 64
RWKV_HEADS = 16
RWKV_PAIRS = RWKV_HEADS // 2
RWKV_GN_EPS = 64e-5
RWKV_CHUNK = 32

MOE_GROUPS = 4
MOE_PER_GROUP = 4
MOE_EXPERTS = 16
MOE_HIDDEN = 512
ROUTER_EXPERT_LANE0 = 8

LANES = 128
VMEM_LIMIT_BYTES = 56 * 1024 * 1024


def _call(kernel, *, grid, in_specs, out_specs, out_shape, scratch=(), sem, name):
    return pl.pallas_call(
        kernel, grid=grid, in_specs=in_specs, out_specs=out_specs, out_shape=out_shape,
        scratch_shapes=list(scratch), name=name,
        compiler_params=pltpu.CompilerParams(dimension_semantics=sem, vmem_limit_bytes=VMEM_LIMIT_BYTES))


def _sigmoid(x):
    return 1.0 / (1.0 + jnp.exp(-x))


def _silu(x):
    return x * _sigmoid(x)


def _logsig(x):
    return jnp.minimum(x, 0.0) - jnp.log1p(jnp.exp(-jnp.abs(x)))


def _dot(a, b):
    return jnp.dot(a.astype(BF16), b.astype(BF16), preferred_element_type=F32)


def _dot_nt(a, b):
    return lax.dot_general(a.astype(BF16), b.astype(BF16), (((1,), (1,)), ((), ())), preferred_element_type=F32)


def _dot_tn(a, b):
    return lax.dot_general(a.astype(BF16), b.astype(BF16), (((0,), (0,)), ((), ())), preferred_element_type=F32)


def _dot_f32(a, b):
    return jnp.dot(a, b, precision=HIGHEST, preferred_element_type=F32)


def _tri(n, kind, block=None):
    r = lax.broadcasted_iota(jnp.int32, (n, n), 0)
    c = lax.broadcasted_iota(jnp.int32, (n, n), 1)
    m = {"lower": c <= r, "upper": c >= r, "all": c >= 0}[kind]
    if block is not None:
        sh = block.bit_length() - 1
        m = jnp.logical_and(m, jnp.right_shift(r, sh) == jnp.right_shift(c, sh))
    return jnp.where(m, 1.0, 0.0).astype(F32)


def _mm(x, w, *, tm, tn, xlead=(), wlead=(), extras=(), epilogue=None, out_dtype=F32, name):
    m, k = x.shape[-2:]
    n = w.shape[-1]
    tm, tn = min(tm, m), min(tn, n)

    def kern(x_ref, w_ref, *rest):
        acc = _dot(x_ref[...], w_ref[...])
        if epilogue is not None:
            acc = epilogue(acc, *[r[...] for r in rest[:-1]])
        rest[-1][...] = acc.astype(out_dtype)

    def extra_spec(arr):
        if arr.ndim == 2:
            return pl.BlockSpec((tm, tn), lambda i, j: (i, j))
        tiles_per_row = (m // arr.shape[0]) // tm
        return pl.BlockSpec((1, 1, tn), lambda i, j: (i // tiles_per_row, 0, j))

    in_specs = [
        pl.BlockSpec((None,) * len(xlead) + (tm, k), lambda i, j: tuple(xlead) + (i, 0)),
        pl.BlockSpec((None,) * len(wlead) + (k, tn), lambda i, j: tuple(wlead) + (0, j)),
    ] + [extra_spec(a) for a in extras]
    return _call(kern, grid=(m // tm, n // tn), in_specs=in_specs,
                 out_specs=pl.BlockSpec((tm, tn), lambda i, j: (i, j)),
                 out_shape=jax.ShapeDtypeStruct((m, n), out_dtype),
                 sem=("parallel", "parallel"), name=name)(x, w, *extras)


def _residual_epilogue(acc, x, gate):
    return x + gate[0] * acc


def _ada_kernel(c_ref, w_ref, b_ref, o_ref):
    o_ref[0] = _dot(_silu(c_ref[...]), w_ref[0]) + b_ref[0]


def _ada(cond8, ada_w, ada_b):
    depth, d, n = ada_w.shape
    tn = 1536
    return _call(_ada_kernel, grid=(depth, n // tn),
                 in_specs=[pl.BlockSpec((8, d), lambda l, j: (0, 0)),
                           pl.BlockSpec((1, d, tn), lambda l, j: (l, 0, j)),
                           pl.BlockSpec((1, 1, tn), lambda l, j: (l, 0, j))],
                 out_specs=pl.BlockSpec((1, 8, tn), lambda l, j: (l, 0, j)),
                 out_shape=jax.ShapeDtypeStruct((depth, 8, n), F32),
                 sem=("parallel", "parallel"), name="ada")(cond8, ada_w, ada_b.reshape(depth, 1, n))


def _normmod_kernel(x_ref, g_ref, sc_ref, sh_ref, o_ref):
    x = x_ref[...]
    y = x * lax.rsqrt(jnp.mean(x * x, axis=-1, keepdims=True) + NORM_EPS)
    o_ref[...] = (y * g_ref[...]) * (1.0 + sc_ref[0]) + sh_ref[0]


def _normmod(x, g, scale, shift, *, tm=512):
    m, d = x.shape
    tm = min(tm, m)
    tiles_per_row = (m // scale.shape[0]) // tm
    mod = pl.BlockSpec((1, 1, d), lambda i: (i // tiles_per_row, 0, 0))
    return _call(_normmod_kernel, grid=(m // tm,),
                 in_specs=[pl.BlockSpec((tm, d), lambda i: (i, 0)), pl.BlockSpec((1, d), lambda i: (0, 0)), mod, mod],
                 out_specs=pl.BlockSpec((tm, d), lambda i: (i, 0)),
                 out_shape=jax.ShapeDtypeStruct((m, d), F32), sem=("parallel",), name="normmod")(
                     x, g.reshape(1, d), scale, shift)


def _rmsnorm_kernel(x_ref, g_ref, o_ref):
    x = x_ref[...]
    o_ref[...] = x * lax.rsqrt(jnp.mean(x * x, axis=-1, keepdims=True) + NORM_EPS) * g_ref[...]


def _rmsnorm(x, g, *, tm=512):
    m, d = x.shape
    tm = min(tm, m)
    return _call(_rmsnorm_kernel, grid=(m // tm,),
                 in_specs=[pl.BlockSpec((tm, d), lambda i: (i, 0)), pl.BlockSpec((1, d), lambda i: (0, 0))],
                 out_specs=pl.BlockSpec((tm, d), lambda i: (i, 0)),
                 out_shape=jax.ShapeDtypeStruct((m, d), F32), sem=("parallel",), name="final_norm")(
                     x, g.reshape(1, d))


def _conv_kernel(x_ref, w_ref, o_ref, *, rows, width):
    x = x_ref[0]
    t_len = x.shape[0]
    t = lax.broadcasted_iota(jnp.int32, x.shape, 0)
    col = jnp.bitwise_and(t, width - 1)
    row = jnp.right_shift(t, width.bit_length() - 1)
    acc = jnp.zeros_like(x)
    for dr in (-1, 0, 1):
        if rows == 1 and dr != 0:
            continue
        for dc in (-1, 0, 1):
            off = dr * width + dc
            xs = x if off == 0 else pltpu.roll(x, (-off) % t_len, axis=0)
            ok = jnp.where(jnp.logical_and(col + dc >= 0, col + dc < width), 1.0, 0.0)
            if dr != 0:
                ok = ok * jnp.where(jnp.logical_and(row + dr >= 0, row + dr < rows), 1.0, 0.0)
            tap = (dr + 1) * 3 + (dc + 1)
            acc = acc + (xs * ok) * w_ref[tap:tap + 1, :]
    o_ref[0] = _silu(acc)


def _conv_silu(big, conv_w, *, rows, width, channels, tc=256):
    b, t_len, _ = big.shape
    assert width & (width - 1) == 0 and rows * width == t_len
    return _call(functools.partial(_conv_kernel, rows=rows, width=width), grid=(b, channels // tc),
                 in_specs=[pl.BlockSpec((1, t_len, tc), lambda i, j: (i, 0, j)),
                           pl.BlockSpec((9, tc), lambda i, j: (0, j))],
                 out_specs=pl.BlockSpec((1, t_len, tc), lambda i, j: (i, 0, j)),
                 out_shape=jax.ShapeDtypeStruct((b, t_len, channels), F32),
                 sem=("parallel", "parallel"), name="conv_silu")(big, conv_w.reshape(9, channels))


def _mlstm_gate_kernel(h_ref, wi_ref, wf_ref, bi_ref, bf_ref, gb_ref, gw_ref, ga_ref):
    nh, L = MLSTM_HEADS, MLSTM_CHUNK
    h = h_ref[0]
    tg = h.shape[0]
    ig = _dot(h, wi_ref[...]) + bi_ref[...]
    lf = _logsig(_dot(h, wf_ref[...]) + bf_ref[...])
    lane = lax.broadcasted_iota(jnp.int32, ig.shape, 1)
    b = jnp.where(lane < nh, _dot_f32(_tri(tg, "lower", L), lf), _dot_f32(_tri(tg, "upper", L), lf))
    b_last = _dot_f32(_tri(tg, "all", L), lf)
    gb_ref[0] = b
    gw_ref[0] = b_last - b + ig
    ga_ref[0] = ig - b


def _mlstm_gates(h3, w_gate, b_gate, *, tg=256):
    b, t_len, d = h3.shape
    nh = MLSTM_HEADS
    tg = min(tg, t_len)
    wi = _pad_cols(jnp.concatenate([w_gate[0, :, :nh], w_gate[1, :, :nh]], axis=1), LANES)
    wf = _pad_cols(jnp.concatenate([w_gate[0, :, nh:], w_gate[1, :, nh:]], axis=1), LANES)
    bi = _pad_cols(jnp.concatenate([b_gate[0, :nh], b_gate[1, :nh]]).reshape(1, 2 * nh), LANES)
    bf = _pad_cols(jnp.concatenate([b_gate[0, nh:], b_gate[1, nh:]]).reshape(1, 2 * nh), LANES)
    full = lambda shape: pl.BlockSpec(shape, lambda i, c: (0,) * len(shape))
    tok = pl.BlockSpec((1, tg, LANES), lambda i, c: (i, c, 0))
    return _call(_mlstm_gate_kernel, grid=(b, t_len // tg),
                 in_specs=[pl.BlockSpec((1, tg, d), lambda i, c: (i, c, 0)),
                           full((d, LANES)), full((d, LANES)), full((1, LANES)), full((1, LANES))],
                 out_specs=[tok, tok, tok], out_shape=[jax.ShapeDtypeStruct((b, t_len, LANES), F32)] * 3,
                 sem=("parallel", "parallel"), name="mlstm_gates")(h3, wi, wf, bi, bf)


def _mlstm_scan_kernel(qkf_ref, qkb_ref, vf_ref, vb_ref, gbf_ref, gbb_ref, gwf_ref, gwb_ref, gaf_ref, gab_ref,
                       c0_ref, n0_ref, m0_ref, yf_ref, yb_ref, cn_ref, nn_ref, mn_ref, c_s, n_s, m_s):
    nh, L, dk, dv = MLSTM_HEADS, MLSTM_CHUNK, MLSTM_DK, MLSTM_DV
    c = pl.program_id(1)

    @pl.when(c == 0)
    def _():
        c_s[...] = c0_ref[0]
        n_s[...] = n0_ref[0]
        m_s[...] = m0_ref[0]

    ti = lax.broadcasted_iota(jnp.int32, (L, L), 0)
    si = lax.broadcasted_iota(jnp.int32, (L, L), 1)
    lane = lax.broadcasted_iota(jnp.int32, (L, LANES), 1)
    ones = jnp.ones((L, LANES), F32)
    refs = ((qkf_ref, vf_ref, gbf_ref, gwf_ref, gaf_ref, yf_ref), (qkb_ref, vb_ref, gbb_ref, gwb_ref, gab_ref, yb_ref))
    chains = [(d, h) for d in range(2) for h in range(nh)]
    st = {}
    for d, h in chains:
        qk_ref, v_ref, gb_ref, gw_ref, ga_ref, _ = refs[d]
        g = d * nh + h
        q = qk_ref[0, :, h * dk:(h + 1) * dk] * (dk ** -0.5)
        k = qk_ref[0, :, (nh + h) * dk:(nh + h + 1) * dk]
        v = v_ref[0, :, h * dv:(h + 1) * dv]
        a_row = lax.dot_general(ones, jnp.where(lane == g, ga_ref[0], 0.0), (((1,), (1,)), ((), ())),
                                precision=HIGHEST, preferred_element_type=F32)
        b_col = gb_ref[0, :, g:g + 1]
        wl_col = gw_ref[0, :, g:g + 1]
        st[d, h] = dict(q=q, k=k, v=v, a_row=a_row, b_col=b_col, wl_col=wl_col, qk=_dot_nt(q, k))
    for d, h in chains:
        z = st[d, h]
        m_st = m_s[d, h]
        mask = (si <= ti) if d == 0 else (si >= ti)
        log_d = jnp.where(mask, z["b_col"] + z["a_row"], -jnp.inf)
        m_t = jnp.maximum(z["b_col"] + m_st, jnp.max(log_d, axis=-1, keepdims=True))
        z.update(m_t=m_t, s=z["qk"] * jnp.exp(log_d - m_t), w_inter=jnp.exp(z["b_col"] + m_st - m_t))
    for d, h in chains:
        z = st[d, h]
        q, s, w_inter = z["q"], z["s"], z["w_inter"]
        num = _dot(s, z["v"]) + w_inter * _dot(q, c_s[d, h])
        den = jnp.sum(s, axis=-1, keepdims=True) + w_inter * jnp.sum(q * n_s[d, h], axis=-1, keepdims=True)
        refs[d][5][0, :, h * dv:(h + 1) * dv] = num / jnp.maximum(jnp.abs(den), jnp.exp(-z["m_t"]))
    for d, h in chains:
        z = st[d, h]
        last = L - 1 if d == 0 else 0
        m_st = m_s[d, h]
        b_last = z["b_col"][last:last + 1, :]
        m_new = jnp.maximum(b_last + m_st, jnp.max(z["wl_col"], axis=0, keepdims=True))
        ks = jnp.exp(z["wl_col"] - m_new) * z["k"]
        decay = jnp.exp(b_last + m_st - m_new)
        c_s[d, h] = decay * c_s[d, h] + _dot_tn(ks, z["v"])
        n_s[d, h] = decay * n_s[d, h] + jnp.sum(ks, axis=0, keepdims=True)
        m_s[d, h] = m_new

    @pl.when(c == pl.num_programs(1) - 1)
    def _():
        cn_ref[0] = c_s[...]
        nn_ref[0] = n_s[...]
        mn_ref[0] = m_s[...]


def _mlstm_scan(qk, big, gb, gw, ga, c0, n0, m0):
    b, t_len, _ = qk.shape
    nh, L, dk, dv = MLSTM_HEADS, MLSTM_CHUNK, MLSTM_DK, MLSTM_DV
    nc = t_len // L
    fwd = lambda i, c: (i, c, 0)
    bwd = lambda i, c: (i, nc - 1 - c, 0)
    state = lambda shape: pl.BlockSpec((1,) + shape, lambda i, c: (i,) + (0,) * len(shape))
    st_shapes = [(2, nh, dk, dv), (2, nh, 1, dk), (2, nh, 1, 1)]
    return _call(
        _mlstm_scan_kernel, grid=(b, nc),
        in_specs=[pl.BlockSpec((1, L, 2 * nh * dk), fwd), pl.BlockSpec((1, L, 2 * nh * dk), bwd),
                  pl.BlockSpec((1, L, nh * dv), lambda i, c: (i, c, 1)),
                  pl.BlockSpec((1, L, nh * dv), lambda i, c: (i, nc - 1 - c, 1)),
                  pl.BlockSpec((1, L, LANES), fwd), pl.BlockSpec((1, L, LANES), bwd),
                  pl.BlockSpec((1, L, LANES), fwd), pl.BlockSpec((1, L, LANES), bwd),
                  pl.BlockSpec((1, L, LANES), fwd), pl.BlockSpec((1, L, LANES), bwd)] + [state(s) for s in st_shapes],
        out_specs=[pl.BlockSpec((1, L, nh * dv), fwd), pl.BlockSpec((1, L, nh * dv), bwd)] + [state(s) for s in st_shapes],
        out_shape=[jax.ShapeDtypeStruct((b, t_len, nh * dv), F32)] * 2 + [jax.ShapeDtypeStruct((b,) + s, F32) for s in st_shapes],
        scratch=[pltpu.VMEM(s, F32) for s in st_shapes],
        sem=("parallel", "arbitrary"), name="mlstm_scan")(qk, qk, big, big, gb, gb, gw, gw, ga, ga, c0, n0, m0)


def _mlstm_post_kernel(yf_ref, yb_ref, o_ref, g_ref, out_ref):
    dv = MLSTM_DV
    for h in range(MLSTM_HEADS):
        sl = slice(h * dv, (h + 1) * dv)
        y = yf_ref[:, sl] + yb_ref[:, sl]
        y = y - jnp.mean(y, axis=-1, keepdims=True)
        y = y * lax.rsqrt(jnp.mean(y * y, axis=-1, keepdims=True) + NORM_EPS)
        out_ref[:, sl] = _sigmoid(o_ref[:, sl]) * (y * g_ref[:, sl])


def _mlstm_post(yf, yb, big, norm_g, *, tm=512):
    m, d = yf.shape
    tm = min(tm, m)
    row = lambda i: (i, 0)
    return _call(_mlstm_post_kernel, grid=(m // tm,),
                 in_specs=[pl.BlockSpec((tm, d), row), pl.BlockSpec((tm, d), row),
                           pl.BlockSpec((tm, d), lambda i: (i, 2)), pl.BlockSpec((1, d), lambda i: (0, 0))],
                 out_specs=pl.BlockSpec((tm, d), row), out_shape=jax.ShapeDtypeStruct((m, d), F32),
                 sem=("parallel",), name="mlstm_post")(yf, yb, big, norm_g.reshape(1, d))


def _mlstm_mixer(h, b, rows, init, w_in, conv_w, w_gate, b_gate, norm_g, w_out, x, gate1):
    m, d = h.shape
    t_len = m // b
    nh, dk = MLSTM_HEADS, MLSTM_DK
    big = _mm(h, w_in, tm=512, tn=1024, name="mlstm_in")
    big3 = big.reshape(b, t_len, 3 * d)
    qk = _conv_silu(big3, conv_w, rows=rows, width=t_len // rows, channels=2 * nh * dk)
    gb, gw, ga = _mlstm_gates(h.reshape(b, t_len, d), w_gate, b_gate)
    c0, n0, m0 = init
    yf, yb, cn, nn, mn = _mlstm_scan(qk, big3, gb, gw, ga, c0, n0.reshape(b, 2, nh, 1, dk), m0.reshape(b, 2, nh, 1, 1))
    y = _mlstm_post(yf.reshape(m, d), yb.reshape(m, d), big, norm_g)
    x = _mm(y, w_out, tm=512, tn=1024, extras=[x, gate1],
            epilogue=_residual_epilogue, name="mlstm_out")
    return x, (cn, nn.reshape(b, 2, nh, dk), mn.reshape(b, 2, nh))


def _gla_gate_kernel(h_ref, w1_ref, w2_ref, b_ref, gf_ref, gb_ref):
    half = GLA_HEADS * GLA_DK
    h = h_ref[0]
    z = _dot(_dot(h, w1_ref[...]), w2_ref[...]) + b_ref[...]
    lg = _logsig(z) * (1.0 / GLA_TAU)
    tg = h.shape[0]
    gf_ref[0] = _dot_f32(_tri(tg, "lower", GLA_CHUNK), lg[:, :half])
    gb_ref[0] = _dot_f32(_tri(tg, "upper", GLA_CHUNK), lg[:, half:])


def _gla_gates(h3, w_a1, w_a2, b_a, *, tg=256):
    b, t_len, d = h3.shape
    rank, half = w_a1.shape[-1], w_a2.shape[-1]
    w1 = jnp.zeros((d, LANES), F32).at[:, :rank].set(w_a1[0]).at[:, rank:2 * rank].set(w_a1[1])
    w2 = jnp.zeros((LANES, 2 * half), F32).at[:rank, :half].set(w_a2[0]).at[rank:2 * rank, half:].set(w_a2[1])
    bias = jnp.concatenate([b_a[0], b_a[1]]).reshape(1, 2 * half)
    tg = min(tg, t_len)
    full = lambda shape: pl.BlockSpec(shape, lambda i, c: (0,) * len(shape))
    tok = lambda n: pl.BlockSpec((1, tg, n), lambda i, c: (i, c, 0))
    return _call(_gla_gate_kernel, grid=(b, t_len // tg),
                 in_specs=[tok(d), full((d, LANES)), full((LANES, 2 * half)), full((1, 2 * half))],
                 out_specs=[tok(half), tok(half)],
                 out_shape=[jax.ShapeDtypeStruct((b, t_len, half), F32)] * 2,
                 sem=("parallel", "parallel"), name="gla_gates")(h3, w1, w2, bias)


def _gla_scan_kernel(xf_ref, xb_ref, gf_ref, gb_ref, s0_ref, yf_ref, yb_ref, sn_ref, s_s):
    nh, L, dk, dv = GLA_HEADS, GLA_CHUNK, GLA_DK, GLA_DV
    c = pl.program_id(1)

    @pl.when(c == 0)
    def _():
        s_s[...] = s0_ref[0]

    tcol = lax.broadcasted_iota(jnp.int32, (L, 1), 0)
    eye = jnp.where(lax.broadcasted_iota(jnp.int32, (dk, dk), 0) == lax.broadcasted_iota(jnp.int32, (dk, dk), 1), 1.0, 0.0)
    refs = ((xf_ref, gf_ref, yf_ref), (xb_ref, gb_ref, yb_ref))
    chains = [(d, h) for d in range(2) for h in range(nh)]
    st = {}
    for d, h in chains:
        x_ref, g_ref, _ = refs[d]
        q = x_ref[0, :, h * dk:(h + 1) * dk] * (dk ** -0.5)
        k = x_ref[0, :, (nh + h) * dk:(nh + h + 1) * dk]
        v = x_ref[0, :, 2 * nh * dk + h * dv:2 * nh * dk + (h + 1) * dv]
        g = g_ref[0, :, h * dk:(h + 1) * dk]
        st[d, h] = dict(q=q, k=k, v=v, g=g, o=_dot(q * jnp.exp(g), s_s[d, h]))
    for s in range(L):
        for d, h in chains:
            z = st[d, h]
            q, k, v, g = z["q"], z["k"], z["v"], z["g"]
            dec = jnp.exp(jnp.minimum(g - g[s:s + 1, :], 0.0))
            col = jnp.sum(q * k[s:s + 1, :] * dec, axis=-1, keepdims=True)
            col = jnp.where((tcol >= s) if d == 0 else (tcol <= s), col, 0.0)
            z["o"] = z["o"] + col * v[s:s + 1, :]
    for d, h in chains:
        z = st[d, h]
        k, v, g = z["k"], z["v"], z["g"]
        last = L - 1 if d == 0 else 0
        refs[d][2][0, :, h * dv:(h + 1) * dv] = z["o"]
        g_last = g[last:last + 1, :]
        decay_col = jnp.sum(eye * jnp.exp(g_last), axis=-1, keepdims=True)
        s_s[d, h] = decay_col * s_s[d, h] + _dot_tn(k * jnp.exp(g_last - g), v)

    @pl.when(c == pl.num_programs(1) - 1)
    def _():
        sn_ref[0] = s_s[...]


def _gla_scan(qkv, gf, gb, s0):
    b, t_len, width = qkv.shape
    nh, L, dk, dv = GLA_HEADS, GLA_CHUNK, GLA_DK, GLA_DV
    nc = t_len // L
    fwd = lambda i, c: (i, c, 0)
    bwd = lambda i, c: (i, nc - 1 - c, 0)
    st = (2, nh, dk, dv)
    state = pl.BlockSpec((1,) + st, lambda i, c: (i, 0, 0, 0, 0))
    return _call(
        _gla_scan_kernel, grid=(b, nc),
        in_specs=[pl.BlockSpec((1, L, width), fwd), pl.BlockSpec((1, L, width), bwd),
                  pl.BlockSpec((1, L, nh * dk), fwd), pl.BlockSpec((1, L, nh * dk), bwd), state],
        out_specs=[pl.BlockSpec((1, L, nh * dv), fwd), pl.BlockSpec((1, L, nh * dv), bwd), state],
        out_shape=[jax.ShapeDtypeStruct((b, t_len, nh * dv), F32)] * 2 + [jax.ShapeDtypeStruct((b,) + st, F32)],
        scratch=[pltpu.VMEM(st, F32)], sem=("parallel", "arbitrary"), name="gla_scan")(qkv, qkv, gf, gb, s0)


def _gla_post_kernel(yf_ref, yb_ref, r_ref, g_ref, out_ref):
    dv = GLA_DV
    for h in range(GLA_HEADS):
        sl = slice(h * dv, (h + 1) * dv)
        y = yf_ref[:, sl] + yb_ref[:, sl]
        y = y * lax.rsqrt(jnp.mean(y * y, axis=-1, keepdims=True) + NORM_EPS)
        out_ref[:, sl] = (y * g_ref[:, sl]) * _silu(r_ref[:, sl])


def _gla_post(yf, yb, big, norm_g, *, tm=512):
    m, d = yf.shape
    tm = min(tm, m)
    row = lambda i: (i, 0)
    return _call(_gla_post_kernel, grid=(m // tm,),
                 in_specs=[pl.BlockSpec((tm, d), row), pl.BlockSpec((tm, d), row),
                           pl.BlockSpec((tm, d), lambda i: (i, 2)), pl.BlockSpec((1, d), lambda i: (0, 0))],
                 out_specs=pl.BlockSpec((tm, d), row), out_shape=jax.ShapeDtypeStruct((m, d), F32),
                 sem=("parallel",), name="gla_post")(yf, yb, big, norm_g.reshape(1, d))


def _gla_mixer(h, b, rows, init, w_in, conv_w, w_a1, w_a2, b_a, norm_g, w_out, x, gate1):
    m, d = h.shape
    t_len = m // b
    big = _mm(h, w_in, tm=512, tn=1024, name="gla_in")
    big3 = big.reshape(b, t_len, 3 * d)
    qkv = _conv_silu(big3, conv_w, rows=rows, width=t_len // rows, channels=2 * d)
    gf, gb = _gla_gates(h.reshape(b, t_len, d), w_a1, w_a2, b_a)
    yf, yb, sn = _gla_scan(qkv, gf, gb, init[0])
    y = _gla_post(yf.reshape(m, d), yb.reshape(m, d), big, norm_g)
    x = _mm(y, w_out, tm=512, tn=1024, extras=[x, gate1],
            epilogue=_residual_epilogue, name="gla_out")
    return x, (sn,)


def _rwkv_mix_kernel(h_ref, hp_ref, hn_ref, mu_ref, o_ref):
    i = pl.program_id(1)
    h = h_ref[0]
    tm = h.shape[0]
    prev_row = jnp.where(i > 0, hp_ref[0, 7:8, :], 0.0)
    next_row = jnp.where(i < pl.num_programs(1) - 1, hn_ref[0, 0:1, :], 0.0)
    row = lax.broadcasted_iota(jnp.int32, (tm, 1), 0)
    x_prev = jnp.where(row == 0, prev_row, pltpu.roll(h, 1, axis=0))
    x_next = jnp.where(row == tm - 1, next_row, pltpu.roll(h, tm - 1, axis=0))
    xx = 0.5 * (x_prev + x_next) - h
    for j in range(6):
        o_ref[j, 0] = (h + xx * mu_ref[j:j + 1, :]).astype(BF16)


def _rwkv_mix(h3, mu, *, tm=256):
    b, t_len, d = h3.shape
    tm = min(tm, t_len)
    nt, n8 = t_len // tm, t_len // 8
    return _call(_rwkv_mix_kernel, grid=(b, nt),
                 in_specs=[pl.BlockSpec((1, tm, d), lambda i, j: (i, j, 0)),
                           pl.BlockSpec((1, 8, d), lambda i, j: (i, jnp.maximum(j * (tm // 8) - 1, 0), 0)),
                           pl.BlockSpec((1, 8, d), lambda i, j: (i, jnp.minimum((j + 1) * (tm // 8), n8 - 1), 0)),
                           pl.BlockSpec((6, d), lambda i, j: (0, 0))],
                 out_specs=pl.BlockSpec((6, 1, tm, d), lambda i, j: (0, i, j, 0)),
                 out_shape=jax.ShapeDtypeStruct((6, b, t_len, d), BF16),
                 sem=("parallel", "parallel"), name="rwkv_mix")(h3, h3, h3, mu)


def _segsum(x):
    lo = lax.broadcasted_iota(jnp.int32, x.shape, 1) < RWKV_N
    s0 = jnp.sum(jnp.where(lo, x, 0.0), axis=-1, keepdims=True)
    s1 = jnp.sum(jnp.where(lo, 0.0, x), axis=-1, keepdims=True)
    return jnp.where(lo, s0, s1)


def _rwkv_prep_kernel(r_ref, k_ref, v_ref, lora_ref, ap_ref, w0_ref, a0_ref, kk_w_ref, ka_w_ref, rk_w_ref,
                      kk_ref, kb_ref, k2_ref, bonus_ref, lw_ref):
    d = D_MODEL
    for j in range(d // LANES):
        sl = slice(j * LANES, (j + 1) * LANES)
        r, k, v = r_ref[:, sl], k_ref[:, sl], v_ref[:, sl]
        a = _sigmoid(a0_ref[:, sl] + ap_ref[:, sl])
        kk = k * kk_w_ref[:, sl]
        kk = kk / jnp.maximum(jnp.sqrt(_segsum(kk * kk)), 1e-12)
        k2 = k * (1.0 + (a - 1.0) * ka_w_ref[:, sl])
        kk_ref[:, sl] = kk
        kb_ref[:, sl] = kk * a
        k2_ref[:, sl] = k2
        bonus_ref[:, sl] = _segsum(r * k2 * rk_w_ref[:, sl]) * v
        for z in range(2):
            lw_ref[z, :, sl] = -jnp.exp(_logsig(w0_ref[z:z + 1, sl] + lora_ref[:, z * d + j * LANES:z * d + (j + 1) * LANES]) - 0.5)


def _rwkv_prep(r, k, v, lora, a_pre, w0, a0, k_k, k_a, r_k, *, tm=256):
    m, d = r.shape
    tm = min(tm, m)
    row = lambda i: (i, 0)
    tok = pl.BlockSpec((tm, d), row)
    par = pl.BlockSpec((1, d), lambda i: (0, 0))
    return _call(_rwkv_prep_kernel, grid=(m // tm,),
                 in_specs=[tok, tok, tok, pl.BlockSpec((tm, 2 * d), row), tok,
                           pl.BlockSpec((2, d), lambda i: (0, 0)), par, par, par, par],
                 out_specs=[tok, tok, tok, tok, pl.BlockSpec((2, tm, d), lambda i: (0, i, 0))],
                 out_shape=[jax.ShapeDtypeStruct((m, d), F32)] * 4 + [jax.ShapeDtypeStruct((2, m, d), F32)],
                 sem=("parallel",), name="rwkv_prep")(
                     r, k, v, lora, a_pre, w0, a0.reshape(1, d), k_k.reshape(1, d), k_a.reshape(1, d), r_k.reshape(1, d))


def _rwkv_scan_kernel(rf, kf, vf, kkf, kbf, lwf, rb, kb_, vb, kkb, kbb, lwb, s0_ref, yf_ref, yb_ref, sn_ref, s_s):
    L, pairs = RWKV_CHUNK, RWKV_PAIRS
    c = pl.program_id(1)

    @pl.when(c == 0)
    def _():
        s_s[...] = s0_ref[0]

    row = lax.broadcasted_iota(jnp.int32, (L, LANES), 0)
    lane = lax.broadcasted_iota(jnp.int32, (L, LANES), 1)
    lane_tok = jnp.bitwise_and(lane, L - 1)
    lane_head0 = lane - lane_tok
    lo = lane < RWKV_N
    sq_r = lax.broadcasted_iota(jnp.int32, (LANES, LANES), 0) < RWKV_N
    sq_c = lax.broadcasted_iota(jnp.int32, (LANES, LANES), 1) < RWKV_N
    same_head = sq_r == sq_c
    zeros64 = jnp.zeros((2 * L, LANES), F32)
    eye_cat = jnp.where(lane_tok == row, 1.0, 0.0)

    def head_rows(a):
        return jnp.concatenate([jnp.where(lo, a, 0.0), jnp.where(lo, 0.0, a)], axis=0)

    def group_rows(a0, a1):
        return jnp.concatenate([jnp.concatenate([head_rows(a0), zeros64], axis=1),
                                jnp.concatenate([zeros64, head_rows(a1)], axis=1)], axis=0)

    refs = ((rf, kf, vf, kkf, kbf, lwf, yf_ref), (rb, kb_, vb, kkb, kbb, lwb, yb_ref))
    groups = [(d, grp) for d in range(2) for grp in range(pairs // 2)]
    st = {}
    for d, grp in groups:
        r_ref, k_ref, v_ref, kk_ref, kb_ref, lw_ref, _ = refs[d]
        tri = _tri(L, "lower" if d == 0 else "upper")
        last = L - 1 if d == 0 else 0
        rt, kt, kd, bd, v, e_last, m_b, m_k = [], [], [], [], [], [], 0.0, 0.0
        for q in range(2):
            sl = slice((2 * grp + q) * LANES, (2 * grp + q + 1) * LANES)
            r, k, kk, kb, lw = r_ref[0, :, sl], k_ref[0, :, sl], kk_ref[0, :, sl], kb_ref[0, :, sl], lw_ref[0, 0, :, sl]
            cum = _dot_f32(tri, lw)
            c_last = cum[last:last + 1, :]
            e_neg, e_end = jnp.exp(-cum), jnp.exp(c_last - cum)
            rt.append(r * jnp.exp(cum))
            kt.append(kk * jnp.exp(cum - lw))
            kd.append(k * e_end)
            bd.append(kb * e_end)
            v.append(v_ref[0, :, sl])
            e_last.append(jnp.exp(c_last))
            lhs = jnp.concatenate([kt[q], rt[q]], axis=0)
            pad = lambda a: jnp.concatenate([head_rows(a), zeros64] if q == 0 else [zeros64, head_rows(a)], axis=0)
            m_b = m_b + _dot_nt(lhs, pad(kb * e_neg))
            m_k = m_k + _dot_nt(lhs, pad(k * e_neg))
        st[d, grp] = dict(rt=rt, kt=kt, kd=kd, bd=bd, v=v, e_last=e_last, m_b=m_b, m_k=m_k)
    for d, grp in groups:
        z = st[d, grp]
        strict = (lane_tok < row) if d == 0 else (lane_tok > row)
        incl = (lane_tok <= row) if d == 0 else (lane_tok >= row)
        m_b, m_k = z["m_b"], z["m_k"]
        z["n_kb"] = jnp.where(strict, m_b[:L], 0.0)
        z["m_rb"] = jnp.where(incl, m_b[L:], 0.0)
        m_kk_rk = jnp.concatenate([jnp.where(strict, m_k[:L], 0.0), jnp.where(incl, m_k[L:], 0.0)], axis=0)
        z["kv"] = _dot(m_kk_rk, group_rows(z["v"][0], z["v"][1]))
        z["x"] = eye_cat
    for step in range(L - 1):
        for d, grp in groups:
            z = st[d, grp]
            u = step if d == 0 else L - 1 - step
            coef = jnp.take_along_axis(z["n_kb"], lane_head0 + u, axis=1)
            z["x"] = z["x"] - coef * z["x"][u:u + 1, :]
    for d, grp in groups:
        z = st[d, grp]
        x, kt, kv = z["x"], z["kt"], z["kv"]
        x_hi = x.astype(BF16)
        x_lo = x - x_hi.astype(F32)
        rhs = jnp.concatenate([group_rows(kt[0], kt[1]), group_rows(kv[:L, :LANES], kv[:L, LANES:])], axis=1)
        z["w"] = _dot(x_hi, rhs) + _dot(x_lo, rhs)
    for d, grp in groups:
        z = st[d, grp]
        w = z["w"]
        z["sa"] = [_dot_nt(w[:, q * LANES:(q + 1) * LANES], s_s[d, 2 * grp + q]) + w[:, (2 + q) * LANES:(3 + q) * LANES]
                   for q in range(2)]
    for d, grp in groups:
        z = st[d, grp]
        rb_sa = _dot(z["m_rb"], group_rows(z["sa"][0], z["sa"][1]))
        for q in range(2):
            p = 2 * grp + q
            cols = slice(q * LANES, (q + 1) * LANES)
            refs[d][6][0, :, p * LANES:(p + 1) * LANES] = _dot_nt(z["rt"][q], s_s[d, p]) + z["kv"][L:, cols] - rb_sa[:, cols]
    for d, grp in groups:
        z = st[d, grp]
        for q in range(2):
            p = 2 * grp + q
            s_new = s_s[d, p] * z["e_last"][q] + _dot_tn(z["v"][q], z["kd"][q]) - _dot_tn(z["sa"][q], z["bd"][q])
            s_s[d, p] = jnp.where(same_head, s_new, 0.0)

    @pl.when(c == pl.num_programs(1) - 1)
    def _():
        sn_ref[0] = s_s[...]


def _rwkv_scan(r, k2, v, kk, kb, lw, s0):
    b, t_len, d = r.shape
    L, pairs = RWKV_CHUNK, RWKV_PAIRS
    nc = t_len // L
    fwd = pl.BlockSpec((1, L, d), lambda i, c: (i, c, 0))
    bwd = pl.BlockSpec((1, L, d), lambda i, c: (i, nc - 1 - c, 0))
    st = (2, pairs, LANES, LANES)
    state = pl.BlockSpec((1,) + st, lambda i, c: (i, 0, 0, 0, 0))
    return _call(
        _rwkv_scan_kernel, grid=(b, nc),
        in_specs=[fwd] * 5 + [pl.BlockSpec((1, 1, L, d), lambda i, c: (0, i, c, 0))]
        + [bwd] * 5 + [pl.BlockSpec((1, 1, L, d), lambda i, c: (1, i, nc - 1 - c, 0)), state],
        out_specs=[fwd, bwd, state],
        out_shape=[jax.ShapeDtypeStruct((b, t_len, d), F32)] * 2 + [jax.ShapeDtypeStruct((b,) + st, F32)],
        scratch=[pltpu.VMEM(st, F32)], sem=("parallel", "arbitrary"), name="rwkv_scan")(
            r, k2, v, kk, kb, lw, r, k2, v, kk, kb, lw, s0)


def _rwkv_post_kernel(yf_ref, yb_ref, bonus_ref, g_ref, lng_ref, lnb_ref, out_ref):
    for j in range(D_MODEL // LANES):
        sl = slice(j * LANES, (j + 1) * LANES)
        y = yf_ref[:, sl] + yb_ref[:, sl]
        y = y - _segsum(y) * (1.0 / RWKV_N)
        y = y * lax.rsqrt(_segsum(y * y) * (1.0 / RWKV_N) + RWKV_GN_EPS)
        out_ref[:, sl] = (y * lng_ref[:, sl] + lnb_ref[:, sl] + bonus_ref[:, sl]) * g_ref[:, sl]


def _rwkv_post(yf, yb, bonus, g, ln_g, ln_b, *, tm=256):
    m, d = yf.shape
    tm = min(tm, m)
    tok = pl.BlockSpec((tm, d), lambda i: (i, 0))
    par = pl.BlockSpec((1, d), lambda i: (0, 0))
    return _call(_rwkv_post_kernel, grid=(m // tm,), in_specs=[tok, tok, tok, tok, par, par], out_specs=tok,
                 out_shape=jax.ShapeDtypeStruct((m, d), F32), sem=("parallel",), name="rwkv_post")(
                     yf, yb, bonus, g, ln_g.reshape(1, d), ln_b.reshape(1, d))


def _pad_cols(w, n):
    return jnp.zeros(w.shape[:-1] + (n,), w.dtype).at[..., :w.shape[-1]].set(w)


def _pad_rows(w, n):
    return jnp.zeros((n,) + w.shape[1:], w.dtype).at[:w.shape[0]].set(w)


def _rwkv_mixer(h, b, init, j, mu, w_rkv, w0, w1, w2, a0, a1, a2, g1, g2, k_k, k_a, r_k, ln_g, ln_b, w_out, x, gate1):
    m, d = h.shape
    t_len = m // b
    xmix = _rwkv_mix(h.reshape(b, t_len, d), mu[j]).reshape(6, m, d)
    proj = lambda idx, w, wlead, name, **kw: _mm(xmix, w, xlead=(idx,), wlead=wlead, tm=512, tn=1024, name=name, **kw)
    r = proj(0, w_rkv, (j, 0), "rwkv_r")
    k = proj(2, w_rkv, (j, 1), "rwkv_k")
    v = proj(3, w_rkv, (j, 2), "rwkv_v")
    rank_w = w1.shape[-1]
    w1cat = jnp.concatenate([w1[j, 0], w1[j, 1]], axis=1)
    w2bd = jnp.zeros((2 * rank_w, 2 * d), F32).at[:rank_w, :d].set(w2[j, 0]).at[rank_w:, d:].set(w2[j, 1])
    tw = proj(1, w1cat, (), "rwkv_w1", epilogue=jnp.tanh)
    lora = _mm(tw, w2bd, tm=512, tn=1024, name="rwkv_w2")
    a_pre = _mm(proj(4, _pad_cols(a1[j], LANES), (), "rwkv_a1"), _pad_rows(a2[j], LANES), tm=512, tn=1024, name="rwkv_a2")
    g = _mm(proj(5, g1[j], (), "rwkv_g1", epilogue=_sigmoid), g2[j], tm=512, tn=1024, name="rwkv_g2")
    kk, kb, k2, bonus, lw = _rwkv_prep(r, k, v, lora, a_pre, w0[j], a0[j], k_k[j], k_a[j], r_k[j].reshape(d))
    s3 = lambda z: z.reshape(b, t_len, d)
    yf, yb, sn = _rwkv_scan(s3(r), s3(k2), s3(v), s3(kk), s3(kb), lw.reshape(2, b, t_len, d), init)
    y = _rwkv_post(yf.reshape(m, d), yb.reshape(m, d), bonus, g, ln_g[j], ln_b[j])
    x = _mm(y, w_out, wlead=(j,), tm=512, tn=1024, extras=[x, gate1],
            epilogue=_residual_epilogue, name="rwkv_out")
    return x, sn


def _rwkv_state_to_pairs(s):
    b = s.shape[0]
    n = RWKV_N
    out = jnp.zeros((b, 2, RWKV_PAIRS, 2 * n, 2 * n), F32)
    return out.at[..., :n, :n].set(s[:, :, 0::2]).at[..., n:, n:].set(s[:, :, 1::2])


def _rwkv_state_from_pairs(sp):
    n = RWKV_N
    b = sp.shape[0]
    return jnp.stack([sp[..., :n, :n], sp[..., n:, n:]], axis=3).reshape(b, 2, RWKV_HEADS, n, n)


def _router_kernel(x_ref, w_ref, b_ref, o_ref):
    e0, per = ROUTER_EXPERT_LANE0, MOE_PER_GROUP
    logits = _dot_f32(x_ref[...], w_ref[...]) + b_ref[...]
    lane = lax.broadcasted_iota(jnp.int32, logits.shape, 1).astype(F32)
    big = float(LANES)
    rmax = lambda z: jnp.max(z, axis=-1, keepdims=True)
    rsum = lambda z: jnp.sum(z, axis=-1, keepdims=True)
    first = lambda m: jnp.min(jnp.where(m, lane, big), axis=-1, keepdims=True)
    is_g = lane < MOE_GROUPS
    gmax = rmax(jnp.where(is_g, logits, -jnp.inf))
    gsel = first(jnp.logical_and(is_g, logits >= gmax))
    gw = 1.0 / rsum(jnp.where(is_g, jnp.exp(logits - gmax), 0.0))
    in_grp = jnp.logical_and(lane >= e0 + gsel * per, lane < e0 + (gsel + 1) * per)
    emax = rmax(jnp.where(in_grp, logits, -jnp.inf))
    p = jnp.where(in_grp, jnp.exp(logits - emax), 0.0)
    p = p / rsum(p)
    p1 = rmax(jnp.where(in_grp, p, -1.0))
    i1 = first(jnp.logical_and(in_grp, p >= p1))
    rest = jnp.logical_and(in_grp, lane != i1)
    p2 = rmax(jnp.where(rest, p, -1.0))
    i2 = first(jnp.logical_and(rest, p >= p2))
    tot = p1 + p2
    o_ref[...] = jnp.where(lane == i1, p1 / tot * gw, jnp.where(lane == i2, p2 / tot * gw, 0.0))


def _router(x, w_group, b_group, w_expert, b_expert, *, tm=512):
    m, d = x.shape
    tm = min(tm, m)
    e0 = ROUTER_EXPERT_LANE0
    w = jnp.zeros((d, LANES), F32).at[:, :MOE_GROUPS].set(w_group).at[:, e0:e0 + MOE_EXPERTS].set(w_expert)
    bias = jnp.zeros((1, LANES), F32).at[0, :MOE_GROUPS].set(b_group).at[0, e0:e0 + MOE_EXPERTS].set(b_expert)
    return _call(_router_kernel, grid=(m // tm,),
                 in_specs=[pl.BlockSpec((tm, d), lambda i: (i, 0)), pl.BlockSpec((d, LANES), lambda i: (0, 0)),
                           pl.BlockSpec((1, LANES), lambda i: (0, 0))],
                 out_specs=pl.BlockSpec((tm, LANES), lambda i: (i, 0)),
                 out_shape=jax.ShapeDtypeStruct((m, LANES), F32), sem=("parallel",), name="moe_router")(x, w, bias)


def _moe_ffn_kernel(h_ref, comb_ref, wg_ref, wu_ref, wd_ref, x_ref, gate_ref, o_ref, acc):
    e = pl.program_id(1)

    @pl.when(e == 0)
    def _():
        acc[...] = jnp.zeros_like(acc)

    h = h_ref[...].astype(BF16)
    comb = comb_ref[...]
    lane = lax.broadcasted_iota(jnp.int32, comb.shape, 1)
    ce = jnp.sum(jnp.where(lane == e + ROUTER_EXPERT_LANE0, comb, 0.0), axis=-1, keepdims=True)
    hid = _silu(_dot(h, wg_ref[0, 0])) * _dot(h, wu_ref[0, 0])
    acc[...] += _dot(hid * ce, wd_ref[0, 0])

    @pl.when(e == pl.num_programs(1) - 1)
    def _():
        o_ref[...] = x_ref[...] + gate_ref[0] * acc[...]


def _moe_ffn(h, comb, w_gate, w_up, w_down, layer, x, gate2, *, tm=1024):
    m, d = h.shape
    tm = min(tm, m)
    f = w_gate.shape[-1]
    tiles_per_row = (m // gate2.shape[0]) // tm
    tok = pl.BlockSpec((tm, d), lambda i, e: (i, 0))
    return _call(_moe_ffn_kernel, grid=(m // tm, MOE_EXPERTS),
                 in_specs=[tok, pl.BlockSpec((tm, LANES), lambda i, e: (i, 0)),
                           pl.BlockSpec((1, 1, d, f), lambda i, e: (layer, e, 0, 0)),
                           pl.BlockSpec((1, 1, d, f), lambda i, e: (layer, e, 0, 0)),
                           pl.BlockSpec((1, 1, f, d), lambda i, e: (layer, e, 0, 0)),
                           tok, pl.BlockSpec((1, 1, d), lambda i, e: (i // tiles_per_row, 0, 0))],
                 out_specs=tok, out_shape=jax.ShapeDtypeStruct((m, d), F32),
                 scratch=[pltpu.VMEM((tm, d), F32)], sem=("parallel", "arbitrary"), name="moe_ffn")(
                     h, comb, w_gate, w_up, w_down, x, gate2)


def _run_trunk(x3, mods, rows, init_a, init_b, init_c, p):
    b, t_len, d = x3.shape
    m = b * t_len
    x = x3.reshape(m, d)
    new_a, new_b, new_c = [], [], []
    for i in range(p["ada_w"].shape[0]):
        j = i // 3
        shift1, scale1, gate1, shift2, scale2, gate2 = mods[i]
        h = _normmod(x, p["norm1_g"][i], scale1, shift1)
        if i % 3 == 0:
            x, st = _mlstm_mixer(h, b, rows, tuple(s[:, j] for s in init_a), p["mlstm_w_in"][j], p["mlstm_conv_w"][j],
                                 p["mlstm_w_gate"][j], p["mlstm_b_gate"][j], p["mlstm_norm_g"][j], p["mlstm_w_out"][j],
                                 x, gate1)
            new_a.append(st)
        elif i % 3 == 1:
            x, st = _gla_mixer(h, b, rows, tuple(s[:, j] for s in init_b), p["gla_w_in"][j], p["gla_conv_w"][j],
                               p["gla_w_a1"][j], p["gla_w_a2"][j], p["gla_b_a"][j], p["gla_norm_g"][j], p["gla_w_out"][j],
                               x, gate1)
            new_b.append(st)
        else:
            s0 = (jnp.zeros((b, 2, RWKV_PAIRS, LANES, LANES), F32) if init_c is None
                  else _rwkv_state_to_pairs(init_c[0][:, j]))
            x, st = _rwkv_mixer(h, b, s0, j,p["rwkv_mu"], p["rwkv_w_rkv"], p["rwkv_w0"],
                                p["rwkv_w1"], p["rwkv_w2"], p["rwkv_a0"], p["rwkv_a1"], p["rwkv_a2"], p["rwkv_g1"],
                                p["rwkv_g2"], p["rwkv_k_k"], p["rwkv_k_a"], p["rwkv_r_k"], p["rwkv_ln_g"], p["rwkv_ln_b"],
                                p["rwkv_w_out"], x, gate1)
            new_c.append((_rwkv_state_from_pairs(st),))
        hff = _normmod(x, p["norm2_g"][i], scale2, shift2)
        comb = _router(hff, p["moe_w_group"][i], p["moe_b_group"][i], p["moe_w_expert"][i], p["moe_b_expert"][i])
        x = _moe_ffn(hff, comb, p["moe_w_gate"], p["moe_w_up"], p["moe_w_down"], i, x, gate2)
    stack = lambda per_layer: tuple(jnp.stack(parts, axis=1) for parts in zip(*per_layer))
    y = _rmsnorm(x, p["final_norm_g"]).reshape(b, t_len, d)
    return y, stack(new_a), stack(new_b), stack(new_c)


def kernel(x_prompt, x_sample, state_mlstm_C, state_mlstm_n, state_mlstm_m, state_gla_S, state_rwkv_S, c, c_ctx, ada_w, ada_b, norm1_g, norm2_g, moe_w_group, moe_b_group, moe_w_expert, moe_b_expert, moe_w_gate, moe_w_up, moe_w_down, final_norm_g, mlstm_w_in, mlstm_conv_w, mlstm_w_gate, mlstm_b_gate, mlstm_norm_g, mlstm_w_out, gla_w_in, gla_conv_w, gla_w_a1, gla_w_a2, gla_b_a, gla_norm_g, gla_w_out, rwkv_mu, rwkv_w_rkv, rwkv_w0, rwkv_w1, rwkv_w2, rwkv_a0, rwkv_a1, rwkv_a2, rwkv_g1, rwkv_g2, rwkv_k_k, rwkv_k_a, rwkv_r_k, rwkv_ln_g, rwkv_ln_b, rwkv_w_out):
    p = dict(ada_w=ada_w, norm1_g=norm1_g, norm2_g=norm2_g, moe_w_group=moe_w_group, moe_b_group=moe_b_group,
             moe_w_expert=moe_w_expert, moe_b_expert=moe_b_expert, moe_w_gate=moe_w_gate, moe_w_up=moe_w_up,
             moe_w_down=moe_w_down, final_norm_g=final_norm_g, mlstm_w_in=mlstm_w_in, mlstm_conv_w=mlstm_conv_w,
             mlstm_w_gate=mlstm_w_gate, mlstm_b_gate=mlstm_b_gate, mlstm_norm_g=mlstm_norm_g, mlstm_w_out=mlstm_w_out,
             gla_w_in=gla_w_in, gla_conv_w=gla_conv_w, gla_w_a1=gla_w_a1, gla_w_a2=gla_w_a2, gla_b_a=gla_b_a,
             gla_norm_g=gla_norm_g, gla_w_out=gla_w_out, rwkv_mu=rwkv_mu, rwkv_w_rkv=rwkv_w_rkv, rwkv_w0=rwkv_w0,
             rwkv_w1=rwkv_w1, rwkv_w2=rwkv_w2, rwkv_a0=rwkv_a0, rwkv_a1=rwkv_a1, rwkv_a2=rwkv_a2, rwkv_g1=rwkv_g1,
             rwkv_g2=rwkv_g2, rwkv_k_k=rwkv_k_k, rwkv_k_a=rwkv_k_a, rwkv_r_k=rwkv_r_k, rwkv_ln_g=rwkv_ln_g,
             rwkv_ln_b=rwkv_ln_b, rwkv_w_out=rwkv_w_out)
    for name in ("moe_w_gate", "moe_w_up", "moe_w_down"):
        p[name] = p[name].astype(BF16)
    depth, d = ada_w.shape[0], ada_w.shape[1]
    n_dec = c.shape[0]
    cond8 = jnp.zeros((8, d), F32).at[0].set(c_ctx).at[1:1 + n_dec].set(c)
    mod = _ada(cond8, ada_w, ada_b)
    split = lambda rows: [tuple(rows[i][:, None, k * d:(k + 1) * d] for k in range(6)) for i in range(depth)]
    mods_ctx = split(mod[:, 0:1])
    mods_dec = split(mod[:, 1:1 + n_dec])

    bp = x_prompt.shape[0]
    zeros_like_ctx = lambda s: jnp.zeros((bp,) + s.shape[1:], F32)
    y_prompt, st_a, st_b, st_c = _run_trunk(
        x_prompt, mods_ctx, 1, tuple(zeros_like_ctx(s) for s in (state_mlstm_C, state_mlstm_n, state_mlstm_m)),
        (zeros_like_ctx(state_gla_S),), None, p)
    y_sample = _run_trunk(x_sample, mods_dec, x_sample.shape[1] // GRID_W,
                          (state_mlstm_C, state_mlstm_n, state_mlstm_m), (state_gla_S,), (state_rwkv_S,), p)[0]
    return (y_prompt, y_sample, st_a[0], st_a[1], st_a[2], st_b[0], st_c[0])
```

```python
import functools

import jax
import jax.numpy as jnp
from jax import lax
from jax.experimental import pallas as pl
from jax.experimental.pallas import tpu as pltpu

F32 = jnp.float32
BF16 = jnp.bfloat16
HIGHEST = lax.Precision.HIGHEST

D_MODEL = 1024
GRID_W = 64
NORM_EPS = 1e-6

MLSTM_HEADS = 4
MLSTM_DK = 128
MLSTM_DV = 256
MLSTM_CHUNK = 64

GLA_HEADS = 4
GLA_DK = 128
GLA_DV = 256
GLA_TAU = 16.0
GLA_CHUNK = 32

RWKV_N = 64
RWKV_HEADS = 16
RWKV_PAIRS = RWKV_HEADS // 2
RWKV_GN_EPS = 64e-5
RWKV_CHUNK = 32

MOE_GROUPS = 4
MOE_PER_GROUP = 4
MOE_EXPERTS = 16
MOE_HIDDEN = 512
ROUTER_EXPERT_LANE0 = 8

CONV_BLOCK_BYTES = 2 * 1024 * 1024
SCAN_BATCH_ROWS = 2

LANES = 128
VMEM_LIMIT_BYTES = 56 * 1024 * 1024


def _call(kernel, *, grid, in_specs, out_specs, out_shape, scratch=(), sem, name):
    return pl.pallas_call(
        kernel, grid=grid, in_specs=in_specs, out_specs=out_specs, out_shape=out_shape,
        scratch_shapes=list(scratch), name=name,
        compiler_params=pltpu.CompilerParams(dimension_semantics=sem, vmem_limit_bytes=VMEM_LIMIT_BYTES))


def _sigmoid(x):
    return 1.0 / (1.0 + jnp.exp(-x))


def _silu(x):
    return x * _sigmoid(x)


def _logsig(x):
    return jnp.minimum(x, 0.0) - jnp.log1p(jnp.exp(-jnp.abs(x)))


def _dot(a, b):
    return jnp.dot(a.astype(BF16), b.astype(BF16), preferred_element_type=F32)


def _dot_nt(a, b):
    return lax.dot_general(a.astype(BF16), b.astype(BF16), (((1,), (1,)), ((), ())), preferred_element_type=F32)


def _dot_tn(a, b):
    return lax.dot_general(a.astype(BF16), b.astype(BF16), (((0,), (0,)), ((), ())), preferred_element_type=F32)


def _dot_f32(a, b):
    return jnp.dot(a, b, precision=HIGHEST, preferred_element_type=F32)


def _tri(n, kind, block=None):
    r = lax.broadcasted_iota(jnp.int32, (n, n), 0)
    c = lax.broadcasted_iota(jnp.int32, (n, n), 1)
    m = {"lower": c <= r, "upper": c >= r, "all": c >= 0}[kind]
    if block is not None:
        sh = block.bit_length() - 1
        m = jnp.logical_and(m, jnp.right_shift(r, sh) == jnp.right_shift(c, sh))
    return jnp.where(m, 1.0, 0.0).astype(F32)


def _mm(x, w, *, tm, tn, xlead=(), wlead=(), extras=(), epilogue=None, out_dtype=F32, name):
    m, k = x.shape[-2:]
    n = w.shape[-1]
    tm, tn = min(tm, m), min(tn, n)

    def kern(x_ref, w_ref, *rest):
        acc = _dot(x_ref[...], w_ref[...])
        if epilogue is not None:
            acc = epilogue(acc, *[r[...] for r in rest[:-1]])
        rest[-1][...] = acc.astype(out_dtype)

    def extra_spec(arr):
        if arr.ndim == 2:
            return pl.BlockSpec((tm, tn), lambda i, j: (i, j))
        tiles_per_row = (m // arr.shape[0]) // tm
        return pl.BlockSpec((1, 1, tn), lambda i, j: (i // tiles_per_row, 0, j))

    in_specs = [
        pl.BlockSpec((None,) * len(xlead) + (tm, k), lambda i, j: tuple(xlead) + (i, 0)),
        pl.BlockSpec((None,) * len(wlead) + (k, tn), lambda i, j: tuple(wlead) + (0, j)),
    ] + [extra_spec(a) for a in extras]
    return _call(kern, grid=(m // tm, n // tn), in_specs=in_specs,
                 out_specs=pl.BlockSpec((tm, tn), lambda i, j: (i, j)),
                 out_shape=jax.ShapeDtypeStruct((m, n), out_dtype),
                 sem=("parallel", "parallel"), name=name)(x, w, *extras)


def _residual_epilogue(acc, x, gate):
    return x + gate[0] * acc


def _ada_kernel(c_ref, w_ref, b_ref, o_ref):
    o_ref[0] = _dot(_silu(c_ref[...]), w_ref[0]) + b_ref[0]


def _ada(cond8, ada_w, ada_b):
    depth, d, n = ada_w.shape
    tn = 1536
    return _call(_ada_kernel, grid=(depth, n // tn),
                 in_specs=[pl.BlockSpec((8, d), lambda l, j: (0, 0)),
                           pl.BlockSpec((1, d, tn), lambda l, j: (l, 0, j)),
                           pl.BlockSpec((1, 1, tn), lambda l, j: (l, 0, j))],
                 out_specs=pl.BlockSpec((1, 8, tn), lambda l, j: (l, 0, j)),
                 out_shape=jax.ShapeDtypeStruct((depth, 8, n), F32),
                 sem=("parallel", "parallel"), name="ada")(cond8, ada_w, ada_b.reshape(depth, 1, n))


def _normmod_kernel(x_ref, g_ref, sc_ref, sh_ref, o_ref):
    x = x_ref[...]
    y = x * lax.rsqrt(jnp.mean(x * x, axis=-1, keepdims=True) + NORM_EPS)
    o_ref[...] = ((y * g_ref[...]) * (1.0 + sc_ref[0]) + sh_ref[0]).astype(o_ref.dtype)


def _normmod(x, g, scale, shift, *, out_dtype=F32, tm=512):
    m, d = x.shape
    tm = min(tm, m)
    tiles_per_row = (m // scale.shape[0]) // tm
    mod = pl.BlockSpec((1, 1, d), lambda i: (i // tiles_per_row, 0, 0))
    return _call(_normmod_kernel, grid=(m // tm,),
                 in_specs=[pl.BlockSpec((tm, d), lambda i: (i, 0)), pl.BlockSpec((1, d), lambda i: (0, 0)), mod, mod],
                 out_specs=pl.BlockSpec((tm, d), lambda i: (i, 0)),
                 out_shape=jax.ShapeDtypeStruct((m, d), out_dtype), sem=("parallel",), name="normmod")(
                     x, g.reshape(1, d), scale, shift)


def _rmsnorm_kernel(x_ref, g_ref, o_ref):
    x = x_ref[...]
    o_ref[...] = x * lax.rsqrt(jnp.mean(x * x, axis=-1, keepdims=True) + NORM_EPS) * g_ref[...]


def _rmsnorm(x, g, *, tm=512):
    m, d = x.shape
    tm = min(tm, m)
    return _call(_rmsnorm_kernel, grid=(m // tm,),
                 in_specs=[pl.BlockSpec((tm, d), lambda i: (i, 0)), pl.BlockSpec((1, d), lambda i: (0, 0))],
                 out_specs=pl.BlockSpec((tm, d), lambda i: (i, 0)),
                 out_shape=jax.ShapeDtypeStruct((m, d), F32), sem=("parallel",), name="final_norm")(
                     x, g.reshape(1, d))


def _conv_kernel(x_ref, w_ref, o_ref, *, rows, width):
    x = x_ref[0]
    t_len = x.shape[0]
    t = lax.broadcasted_iota(jnp.int32, x.shape, 0)
    col = jnp.bitwise_and(t, width - 1)
    row = jnp.right_shift(t, width.bit_length() - 1)
    acc = jnp.zeros_like(x)
    for dr in (-1, 0, 1):
        if rows == 1 and dr != 0:
            continue
        for dc in (-1, 0, 1):
            off = dr * width + dc
            xs = x if off == 0 else pltpu.roll(x, (-off) % t_len, axis=0)
            ok = jnp.where(jnp.logical_and(col + dc >= 0, col + dc < width), 1.0, 0.0)
            if dr != 0:
                ok = ok * jnp.where(jnp.logical_and(row + dr >= 0, row + dr < rows), 1.0, 0.0)
            tap = (dr + 1) * 3 + (dc + 1)
            acc = acc + (xs * ok) * w_ref[tap:tap + 1, :]
    o_ref[0] = _silu(acc)


def _conv_silu(big, conv_w, *, rows, width, channels):
    b, t_len, _ = big.shape
    assert width & (width - 1) == 0 and rows * width == t_len
    tc = max(2 * LANES, min(channels, CONV_BLOCK_BYTES // (4 * t_len)))
    return _call(functools.partial(_conv_kernel, rows=rows, width=width), grid=(b, channels // tc),
                 in_specs=[pl.BlockSpec((1, t_len, tc), lambda i, j: (i, 0, j)),
                           pl.BlockSpec((9, tc), lambda i, j: (0, j))],
                 out_specs=pl.BlockSpec((1, t_len, tc), lambda i, j: (i, 0, j)),
                 out_shape=jax.ShapeDtypeStruct((b, t_len, channels), F32),
                 sem=("parallel", "parallel"), name="conv_silu")(big, conv_w.reshape(9, channels))


def _mlstm_gate_kernel(h_ref, wi_ref, wf_ref, bi_ref, bf_ref, gb_ref, gw_ref, ga_ref):
    nh, L = MLSTM_HEADS, MLSTM_CHUNK
    h = h_ref[0]
    tg = h.shape[0]
    ig = _dot(h, wi_ref[...]) + bi_ref[...]
    lf = _logsig(_dot(h, wf_ref[...]) + bf_ref[...])
    lane = lax.broadcasted_iota(jnp.int32, ig.shape, 1)
    b = jnp.where(lane < nh, _dot_f32(_tri(tg, "lower", L), lf), _dot_f32(_tri(tg, "upper", L), lf))
    b_last = _dot_f32(_tri(tg, "all", L), lf)
    gb_ref[0] = b
    gw_ref[0] = b_last - b + ig
    ga_ref[0] = ig - b


def _mlstm_gates(h3, w_gate, b_gate, *, tg=256):
    b, t_len, d = h3.shape
    nh = MLSTM_HEADS
    tg = min(tg, t_len)
    wi = _pad_cols(jnp.concatenate([w_gate[0, :, :nh], w_gate[1, :, :nh]], axis=1), LANES)
    wf = _pad_cols(jnp.concatenate([w_gate[0, :, nh:], w_gate[1, :, nh:]], axis=1), LANES)
    bi = _pad_cols(jnp.concatenate([b_gate[0, :nh], b_gate[1, :nh]]).reshape(1, 2 * nh), LANES)
    bf = _pad_cols(jnp.concatenate([b_gate[0, nh:], b_gate[1, nh:]]).reshape(1, 2 * nh), LANES)
    full = lambda shape: pl.BlockSpec(shape, lambda i, c: (0,) * len(shape))
    tok = pl.BlockSpec((1, tg, LANES), lambda i, c: (i, c, 0))
    return _call(_mlstm_gate_kernel, grid=(b, t_len // tg),
                 in_specs=[pl.BlockSpec((1, tg, d), lambda i, c: (i, c, 0)),
                           full((d, LANES)), full((d, LANES)), full((1, LANES)), full((1, LANES))],
                 out_specs=[tok, tok, tok], out_shape=[jax.ShapeDtypeStruct((b, t_len, LANES), F32)] * 3,
                 sem=("parallel", "parallel"), name="mlstm_gates")(h3, wi, wf, bi, bf)


def _mlstm_scan_kernel(qkf_ref, qkb_ref, vf_ref, vb_ref, gbf_ref, gbb_ref, gwf_ref, gwb_ref, gaf_ref, gab_ref,
                       c0_ref, n0_ref, m0_ref, yf_ref, yb_ref, cn_ref, nn_ref, mn_ref, c_s, n_s, m_s):
    nh, L, dk, dv = MLSTM_HEADS, MLSTM_CHUNK, MLSTM_DK, MLSTM_DV
    c = pl.program_id(1)

    @pl.when(c == 0)
    def _():
        c_s[...] = c0_ref[...]
        n_s[...] = n0_ref[...]
        m_s[...] = m0_ref[...]

    ti = lax.broadcasted_iota(jnp.int32, (L, L), 0)
    si = lax.broadcasted_iota(jnp.int32, (L, L), 1)
    lane = lax.broadcasted_iota(jnp.int32, (L, LANES), 1)
    ones = jnp.ones((L, LANES), F32)
    refs = ((qkf_ref, vf_ref, gbf_ref, gwf_ref, gaf_ref, yf_ref), (qkb_ref, vb_ref, gbb_ref, gwb_ref, gab_ref, yb_ref))
    chains = [(i, d, h) for i in range(c_s.shape[0]) for d in range(2) for h in range(nh)]
    st = {}
    for i, d, h in chains:
        qk_ref, v_ref, gb_ref, gw_ref, ga_ref, _ = refs[d]
        g = d * nh + h
        q = qk_ref[i, :, h * dk:(h + 1) * dk] * (dk ** -0.5)
        k = qk_ref[i, :, (nh + h) * dk:(nh + h + 1) * dk]
        v = v_ref[i, :, h * dv:(h + 1) * dv]
        a_row = lax.dot_general(ones, jnp.where(lane == g, ga_ref[i], 0.0), (((1,), (1,)), ((), ())),
                                precision=HIGHEST, preferred_element_type=F32)
        b_col = gb_ref[i, :, g:g + 1]
        wl_col = gw_ref[i, :, g:g + 1]
        st[i, d, h] = dict(q=q, k=k, v=v, a_row=a_row, b_col=b_col, wl_col=wl_col, qk=_dot_nt(q, k))
    for i, d, h in chains:
        z = st[i, d, h]
        m_st = m_s[i, d, h]
        mask = (si <= ti) if d == 0 else (si >= ti)
        log_d = jnp.where(mask, z["b_col"] + z["a_row"], -jnp.inf)
        m_t = jnp.maximum(z["b_col"] + m_st, jnp.max(log_d, axis=-1, keepdims=True))
        z.update(m_t=m_t, s=z["qk"] * jnp.exp(log_d - m_t), w_inter=jnp.exp(z["b_col"] + m_st - m_t))
    for i, d, h in chains:
        z = st[i, d, h]
        q, s, w_inter = z["q"], z["s"], z["w_inter"]
        num = _dot(s, z["v"]) + w_inter * _dot(q, c_s[i, d, h])
        den = jnp.sum(s, axis=-1, keepdims=True) + w_inter * jnp.sum(q * n_s[i, d, h], axis=-1, keepdims=True)
        refs[d][5][i, :, h * dv:(h + 1) * dv] = num / jnp.maximum(jnp.abs(den), jnp.exp(-z["m_t"]))
    for i, d, h in chains:
        z = st[i, d, h]
        last = L - 1 if d == 0 else 0
        m_st = m_s[i, d, h]
        b_last = z["b_col"][last:last + 1, :]
        m_new = jnp.maximum(b_last + m_st, jnp.max(z["wl_col"], axis=0, keepdims=True))
        ks = jnp.exp(z["wl_col"] - m_new) * z["k"]
        decay = jnp.exp(b_last + m_st - m_new)
        c_s[i, d, h] = decay * c_s[i, d, h] + _dot_tn(ks, z["v"])
        n_s[i, d, h] = decay * n_s[i, d, h] + jnp.sum(ks, axis=0, keepdims=True)
        m_s[i, d, h] = m_new

    @pl.when(c == pl.num_programs(1) - 1)
    def _():
        cn_ref[...] = c_s[...]
        nn_ref[...] = n_s[...]
        mn_ref[...] = m_s[...]


def _mlstm_scan(qk, big, gb, gw, ga, c0, n0, m0, *, nb=SCAN_BATCH_ROWS):
    b, t_len, _ = qk.shape
    nh, L, dk, dv = MLSTM_HEADS, MLSTM_CHUNK, MLSTM_DK, MLSTM_DV
    nc = t_len // L
    fwd = lambda i, c: (i, c, 0)
    bwd = lambda i, c: (i, nc - 1 - c, 0)
    state = lambda shape: pl.BlockSpec((nb,) + shape, lambda i, c: (i,) + (0,) * len(shape))
    st_shapes = [(2, nh, dk, dv), (2, nh, 1, dk), (2, nh, 1, 1)]
    return _call(
        _mlstm_scan_kernel, grid=(b // nb, nc),
        in_specs=[pl.BlockSpec((nb, L, 2 * nh * dk), fwd), pl.BlockSpec((nb, L, 2 * nh * dk), bwd),
                  pl.BlockSpec((nb, L, nh * dv), lambda i, c: (i, c, 1)),
                  pl.BlockSpec((nb, L, nh * dv), lambda i, c: (i, nc - 1 - c, 1)),
                  pl.BlockSpec((nb, L, LANES), fwd), pl.BlockSpec((nb, L, LANES), bwd),
                  pl.BlockSpec((nb, L, LANES), fwd), pl.BlockSpec((nb, L, LANES), bwd),
                  pl.BlockSpec((nb, L, LANES), fwd), pl.BlockSpec((nb, L, LANES), bwd)] + [state(s) for s in st_shapes],
        out_specs=[pl.BlockSpec((nb, L, nh * dv), fwd), pl.BlockSpec((nb, L, nh * dv), bwd)] + [state(s) for s in st_shapes],
        out_shape=[jax.ShapeDtypeStruct((b, t_len, nh * dv), F32)] * 2 + [jax.ShapeDtypeStruct((b,) + s, F32) for s in st_shapes],
        scratch=[pltpu.VMEM((nb,) + s, F32) for s in st_shapes],
        sem=("parallel", "arbitrary"), name="mlstm_scan")(qk, qk, big, big, gb, gb, gw, gw, ga, ga, c0, n0, m0)


def _mlstm_post_kernel(yf_ref, yb_ref, o_ref, g_ref, out_ref):
    dv = MLSTM_DV
    for h in range(MLSTM_HEADS):
        sl = slice(h * dv, (h + 1) * dv)
        y = yf_ref[:, sl] + yb_ref[:, sl]
        y = y - jnp.mean(y, axis=-1, keepdims=True)
        y = y * lax.rsqrt(jnp.mean(y * y, axis=-1, keepdims=True) + NORM_EPS)
        out_ref[:, sl] = (_sigmoid(o_ref[:, sl]) * (y * g_ref[:, sl])).astype(BF16)


def _mlstm_post(yf, yb, big, norm_g, *, tm=512):
    m, d = yf.shape
    tm = min(tm, m)
    row = lambda i: (i, 0)
    return _call(_mlstm_post_kernel, grid=(m // tm,),
                 in_specs=[pl.BlockSpec((tm, d), row), pl.BlockSpec((tm, d), row),
                           pl.BlockSpec((tm, d), lambda i: (i, 2)), pl.BlockSpec((1, d), lambda i: (0, 0))],
                 out_specs=pl.BlockSpec((tm, d), row), out_shape=jax.ShapeDtypeStruct((m, d), BF16),
                 sem=("parallel",), name="mlstm_post")(yf, yb, big, norm_g.reshape(1, d))


def _mlstm_mixer(h, b, rows, init, w_in, conv_w, w_gate, b_gate, norm_g, w_out, x, gate1):
    m, d = h.shape
    t_len = m // b
    nh, dk = MLSTM_HEADS, MLSTM_DK
    big = _mm(h, w_in, tm=512, tn=1024, name="mlstm_in")
    big3 = big.reshape(b, t_len, 3 * d)
    qk = _conv_silu(big3, conv_w, rows=rows, width=t_len // rows, channels=2 * nh * dk)
    gb, gw, ga = _mlstm_gates(h.reshape(b, t_len, d), w_gate, b_gate)
    c0, n0, m0 = init
    yf, yb, cn, nn, mn = _mlstm_scan(qk, big3, gb, gw, ga, c0, n0.reshape(b, 2, nh, 1, dk), m0.reshape(b, 2, nh, 1, 1))
    y = _mlstm_post(yf.reshape(m, d), yb.reshape(m, d), big, norm_g)
    x = _mm(y, w_out, tm=512, tn=1024, extras=[x, gate1],
            epilogue=_residual_epilogue, name="mlstm_out")
    return x, (cn, nn.reshape(b, 2, nh, dk), mn.reshape(b, 2, nh))


def _gla_gate_kernel(h_ref, w1_ref, w2_ref, b_ref, gf_ref, gb_ref):
    half = GLA_HEADS * GLA_DK
    h = h_ref[0]
    z = _dot(_dot(h, w1_ref[...]), w2_ref[...]) + b_ref[...]
    lg = _logsig(z) * (1.0 / GLA_TAU)
    tg = h.shape[0]
    gf_ref[0] = _dot_f32(_tri(tg, "lower", GLA_CHUNK), lg[:, :half])
    gb_ref[0] = _dot_f32(_tri(tg, "upper", GLA_CHUNK), lg[:, half:])


def _gla_gates(h3, w_a1, w_a2, b_a, *, tg=256):
    b, t_len, d = h3.shape
    rank, half = w_a1.shape[-1], w_a2.shape[-1]
    w1 = jnp.zeros((d, LANES), F32).at[:, :rank].set(w_a1[0]).at[:, rank:2 * rank].set(w_a1[1])
    w2 = jnp.zeros((LANES, 2 * half), F32).at[:rank, :half].set(w_a2[0]).at[rank:2 * rank, half:].set(w_a2[1])
    bias = jnp.concatenate([b_a[0], b_a[1]]).reshape(1, 2 * half)
    tg = min(tg, t_len)
    full = lambda shape: pl.BlockSpec(shape, lambda i, c: (0,) * len(shape))
    tok = lambda n: pl.BlockSpec((1, tg, n), lambda i, c: (i, c, 0))
    return _call(_gla_gate_kernel, grid=(b, t_len // tg),
                 in_specs=[tok(d), full((d, LANES)), full((LANES, 2 * half)), full((1, 2 * half))],
                 out_specs=[tok(half), tok(half)],
                 out_shape=[jax.ShapeDtypeStruct((b, t_len, half), F32)] * 2,
                 sem=("parallel", "parallel"), name="gla_gates")(h3, w1, w2, bias)


def _gla_scan_kernel(xf_ref, xb_ref, gf_ref, gb_ref, s0_ref, yf_ref, yb_ref, sn_ref, s_s):
    nh, L, dk, dv = GLA_HEADS, GLA_CHUNK, GLA_DK, GLA_DV
    c = pl.program_id(1)

    @pl.when(c == 0)
    def _():
        s_s[...] = s0_ref[0]

    tcol = lax.broadcasted_iota(jnp.int32, (L, 1), 0)
    eye = jnp.where(lax.broadcasted_iota(jnp.int32, (dk, dk), 0) == lax.broadcasted_iota(jnp.int32, (dk, dk), 1), 1.0, 0.0)
    refs = ((xf_ref, gf_ref, yf_ref), (xb_ref, gb_ref, yb_ref))
    chains = [(d, h) for d in range(2) for h in range(nh)]
    st = {}
    for d, h in chains:
        x_ref, g_ref, _ = refs[d]
        q = x_ref[0, :, h * dk:(h + 1) * dk] * (dk ** -0.5)
        k = x_ref[0, :, (nh + h) * dk:(nh + h + 1) * dk]
        v = x_ref[0, :, 2 * nh * dk + h * dv:2 * nh * dk + (h + 1) * dv]
        g = g_ref[0, :, h * dk:(h + 1) * dk]
        st[d, h] = dict(q=q, k=k, v=v, g=g, o=_dot(q * jnp.exp(g), s_s[d, h]))
    sub = 8
    for s in range(L):
        for d, h in chains:
            z = st[d, h]
            q, k, v, g, o = z["q"], z["k"], z["v"], z["g"], z["o"]
            lo, hi = ((s // sub) * sub, L) if d == 0 else (0, (s // sub + 1) * sub)
            dec = jnp.exp(jnp.minimum(g[lo:hi] - g[s:s + 1, :], 0.0))
            col = jnp.sum(q[lo:hi] * k[s:s + 1, :] * dec, axis=-1, keepdims=True)
            col = jnp.where((tcol[lo:hi] >= s) if d == 0 else (tcol[lo:hi] <= s), col, 0.0)
            parts = [o[:lo]] * (lo > 0) + [o[lo:hi] + col * v[s:s + 1, :]] + [o[hi:]] * (hi < L)
            z["o"] = parts[0] if len(parts) == 1 else jnp.concatenate(parts, axis=0)
    for d, h in chains:
        z = st[d, h]
        k, v, g = z["k"], z["v"], z["g"]
        last = L - 1 if d == 0 else 0
        refs[d][2][0, :, h * dv:(h + 1) * dv] = z["o"]
        g_last = g[last:last + 1, :]
        decay_col = jnp.sum(eye * jnp.exp(g_last), axis=-1, keepdims=True)
        s_s[d, h] = decay_col * s_s[d, h] + _dot_tn(k * jnp.exp(g_last - g), v)

    @pl.when(c == pl.num_programs(1) - 1)
    def _():
        sn_ref[0] = s_s[...]


def _gla_scan(qkv, gf, gb, s0):
    b, t_len, width = qkv.shape
    nh, L, dk, dv = GLA_HEADS, GLA_CHUNK, GLA_DK, GLA_DV
    nc = t_len // L
    fwd = lambda i, c: (i, c, 0)
    bwd = lambda i, c: (i, nc - 1 - c, 0)
    st = (2, nh, dk, dv)
    state = pl.BlockSpec((1,) + st, lambda i, c: (i, 0, 0, 0, 0))
    return _call(
        _gla_scan_kernel, grid=(b, nc),
        in_specs=[pl.BlockSpec((1, L, width), fwd), pl.BlockSpec((1, L, width), bwd),
                  pl.BlockSpec((1, L, nh * dk), fwd), pl.BlockSpec((1, L, nh * dk), bwd), state],
        out_specs=[pl.BlockSpec((1, L, nh * dv), fwd), pl.BlockSpec((1, L, nh * dv), bwd), state],
        out_shape=[jax.ShapeDtypeStruct((b, t_len, nh * dv), F32)] * 2 + [jax.ShapeDtypeStruct((b,) + st, F32)],
        scratch=[pltpu.VMEM(st, F32)], sem=("parallel", "arbitrary"), name="gla_scan")(qkv, qkv, gf, gb, s0)


def _gla_post_kernel(yf_ref, yb_ref, r_ref, g_ref, out_ref):
    dv = GLA_DV
    for h in range(GLA_HEADS):
        sl = slice(h * dv, (h + 1) * dv)
        y = yf_ref[:, sl] + yb_ref[:, sl]
        y = y * lax.rsqrt(jnp.mean(y * y, axis=-1, keepdims=True) + NORM_EPS)
        out_ref[:, sl] = ((y * g_ref[:, sl]) * _silu(r_ref[:, sl])).astype(BF16)


def _gla_post(yf, yb, big, norm_g, *, tm=512):
    m, d = yf.shape
    tm = min(tm, m)
    row = lambda i: (i, 0)
    return _call(_gla_post_kernel, grid=(m // tm,),
                 in_specs=[pl.BlockSpec((tm, d), row), pl.BlockSpec((tm, d), row),
                           pl.BlockSpec((tm, d), lambda i: (i, 2)), pl.BlockSpec((1, d), lambda i: (0, 0))],
                 out_specs=pl.BlockSpec((tm, d), row), out_shape=jax.ShapeDtypeStruct((m, d), BF16),
                 sem=("parallel",), name="gla_post")(yf, yb, big, norm_g.reshape(1, d))


def _gla_mixer(h, b, rows, init, w_in, conv_w, w_a1, w_a2, b_a, norm_g, w_out, x, gate1):
    m, d = h.shape
    t_len = m // b
    big = _mm(h, w_in, tm=512, tn=1024, name="gla_in")
    big3 = big.reshape(b, t_len, 3 * d)
    qkv = _conv_silu(big3, conv_w, rows=rows, width=t_len // rows, channels=2 * d)
    gf, gb = _gla_gates(h.reshape(b, t_len, d), w_a1, w_a2, b_a)
    yf, yb, sn = _gla_scan(qkv, gf, gb, init[0])
    y = _gla_post(yf.reshape(m, d), yb.reshape(m, d), big, norm_g)
    x = _mm(y, w_out, tm=512, tn=1024, extras=[x, gate1],
            epilogue=_residual_epilogue, name="gla_out")
    return x, (sn,)


def _rwkv_mix_kernel(h_ref, hp_ref, hn_ref, mu_ref, o_ref):
    i = pl.program_id(1)
    h = h_ref[0]
    tm = h.shape[0]
    prev_row = jnp.where(i > 0, hp_ref[0, 7:8, :], 0.0)
    next_row = jnp.where(i < pl.num_programs(1) - 1, hn_ref[0, 0:1, :], 0.0)
    row = lax.broadcasted_iota(jnp.int32, (tm, 1), 0)
    x_prev = jnp.where(row == 0, prev_row, pltpu.roll(h, 1, axis=0))
    x_next = jnp.where(row == tm - 1, next_row, pltpu.roll(h, tm - 1, axis=0))
    xx = 0.5 * (x_prev + x_next) - h
    for j in range(6):
        o_ref[j, 0] = (h + xx * mu_ref[j:j + 1, :]).astype(BF16)


def _rwkv_mix(h3, mu, *, tm=256):
    b, t_len, d = h3.shape
    tm = min(tm, t_len)
    nt, n8 = t_len // tm, t_len // 8
    return _call(_rwkv_mix_kernel, grid=(b, nt),
                 in_specs=[pl.BlockSpec((1, tm, d), lambda i, j: (i, j, 0)),
                           pl.BlockSpec((1, 8, d), lambda i, j: (i, jnp.maximum(j * (tm // 8) - 1, 0), 0)),
                           pl.BlockSpec((1, 8, d), lambda i, j: (i, jnp.minimum((j + 1) * (tm // 8), n8 - 1), 0)),
                           pl.BlockSpec((6, d), lambda i, j: (0, 0))],
                 out_specs=pl.BlockSpec((6, 1, tm, d), lambda i, j: (0, i, j, 0)),
                 out_shape=jax.ShapeDtypeStruct((6, b, t_len, d), BF16),
                 sem=("parallel", "parallel"), name="rwkv_mix")(h3, h3, h3, mu)


def _segsum(x):
    lo = lax.broadcasted_iota(jnp.int32, x.shape, 1) < RWKV_N
    s0 = jnp.sum(jnp.where(lo, x, 0.0), axis=-1, keepdims=True)
    s1 = jnp.sum(jnp.where(lo, 0.0, x), axis=-1, keepdims=True)
    return jnp.where(lo, s0, s1)


def _rwkv_prep_kernel(r_ref, k_ref, v_ref, lora_ref, ap_ref, w0_ref, a0_ref, kk_w_ref, ka_w_ref, rk_w_ref,
                      kk_ref, kb_ref, k2_ref, bonus_ref, lw_ref):
    d = D_MODEL
    for j in range(d // LANES):
        sl = slice(j * LANES, (j + 1) * LANES)
        r, k, v = r_ref[:, sl], k_ref[:, sl], v_ref[:, sl]
        a = _sigmoid(a0_ref[:, sl] + ap_ref[:, sl])
        kk = k * kk_w_ref[:, sl]
        kk = kk / jnp.maximum(jnp.sqrt(_segsum(kk * kk)), 1e-12)
        k2 = k * (1.0 + (a - 1.0) * ka_w_ref[:, sl])
        kk_ref[:, sl] = kk
        kb_ref[:, sl] = kk * a
        k2_ref[:, sl] = k2
        bonus_ref[:, sl] = _segsum(r * k2 * rk_w_ref[:, sl]) * v
        for z in range(2):
            lw_ref[z, :, sl] = -jnp.exp(_logsig(w0_ref[z:z + 1, sl] + lora_ref[:, z * d + j * LANES:z * d + (j + 1) * LANES]) - 0.5)


def _rwkv_prep(r, k, v, lora, a_pre, w0, a0, k_k, k_a, r_k, *, tm=256):
    m, d = r.shape
    tm = min(tm, m)
    row = lambda i: (i, 0)
    tok = pl.BlockSpec((tm, d), row)
    par = pl.BlockSpec((1, d), lambda i: (0, 0))
    return _call(_rwkv_prep_kernel, grid=(m // tm,),
                 in_specs=[tok, tok, tok, pl.BlockSpec((tm, 2 * d), row), tok,
                           pl.BlockSpec((2, d), lambda i: (0, 0)), par, par, par, par],
                 out_specs=[tok, tok, tok, tok, pl.BlockSpec((2, tm, d), lambda i: (0, i, 0))],
                 out_shape=[jax.ShapeDtypeStruct((m, d), F32)] * 4 + [jax.ShapeDtypeStruct((2, m, d), F32)],
                 sem=("parallel",), name="rwkv_prep")(
                     r, k, v, lora, a_pre, w0, a0.reshape(1, d), k_k.reshape(1, d), k_a.reshape(1, d), r_k.reshape(1, d))


def _rwkv_scan_kernel(rf, kf, vf, kkf, kbf, lwf, rb, kb_, vb, kkb, kbb, lwb, s0_ref, yf_ref, yb_ref, sn_ref, s_s):
    L, pairs = RWKV_CHUNK, RWKV_PAIRS
    c = pl.program_id(1)

    @pl.when(c == 0)
    def _():
        s_s[...] = s0_ref[...]

    row = lax.broadcasted_iota(jnp.int32, (L, LANES), 0)
    lane = lax.broadcasted_iota(jnp.int32, (L, LANES), 1)
    lane_tok = jnp.bitwise_and(lane, L - 1)
    lane_head0 = lane - lane_tok
    lo = lane < RWKV_N
    sq_r = lax.broadcasted_iota(jnp.int32, (LANES, LANES), 0) < RWKV_N
    sq_c = lax.broadcasted_iota(jnp.int32, (LANES, LANES), 1) < RWKV_N
    same_head = sq_r == sq_c
    zeros64 = jnp.zeros((2 * L, LANES), F32)
    eye_cat = jnp.where(lane_tok == row, 1.0, 0.0)

    def head_rows(a):
        return jnp.concatenate([jnp.where(lo, a, 0.0), jnp.where(lo, 0.0, a)], axis=0)

    def group_rows(a0, a1):
        return jnp.concatenate([jnp.concatenate([head_rows(a0), zeros64], axis=1),
                                jnp.concatenate([zeros64, head_rows(a1)], axis=1)], axis=0)

    refs = ((rf, kf, vf, kkf, kbf, lwf, yf_ref), (rb, kb_, vb, kkb, kbb, lwb, yb_ref))
    groups = [(i, d, grp) for i in range(s_s.shape[0]) for d in range(2) for grp in range(pairs // 2)]
    st = {}
    for i, d, grp in groups:
        r_ref, k_ref, v_ref, kk_ref, kb_ref, lw_ref, _ = refs[d]
        tri = _tri(L, "lower" if d == 0 else "upper")
        last = L - 1 if d == 0 else 0
        rt, kt, kd, bd, v, e_last, m_b, m_k = [], [], [], [], [], [], 0.0, 0.0
        for q in range(2):
            sl = slice((2 * grp + q) * LANES, (2 * grp + q + 1) * LANES)
            r, k, kk, kb, lw = r_ref[i, :, sl], k_ref[i, :, sl], kk_ref[i, :, sl], kb_ref[i, :, sl], lw_ref[0, i, :, sl]
            cum = _dot_f32(tri, lw)
            c_last = cum[last:last + 1, :]
            e_neg, e_end = jnp.exp(-cum), jnp.exp(c_last - cum)
            rt.append(r * jnp.exp(cum))
            kt.append(kk * jnp.exp(cum - lw))
            kd.append(k * e_end)
            bd.append(kb * e_end)
            v.append(v_ref[i, :, sl])
            e_last.append(jnp.exp(c_last))
            lhs = jnp.concatenate([kt[q], rt[q]], axis=0)
            pad = lambda a: jnp.concatenate([head_rows(a), zeros64] if q == 0 else [zeros64, head_rows(a)], axis=0)
            m_b = m_b + _dot_nt(lhs, pad(kb * e_neg))
            m_k = m_k + _dot_nt(lhs, pad(k * e_neg))
        st[i, d, grp] = dict(rt=rt, kt=kt, kd=kd, bd=bd, v=v, e_last=e_last, m_b=m_b, m_k=m_k)
    for i, d, grp in groups:
        z = st[i, d, grp]
        strict = (lane_tok < row) if d == 0 else (lane_tok > row)
        incl = (lane_tok <= row) if d == 0 else (lane_tok >= row)
        m_b, m_k = z["m_b"], z["m_k"]
        z["n_kb"] = jnp.where(strict, m_b[:L], 0.0)
        z["m_rb"] = jnp.where(incl, m_b[L:], 0.0)
        m_kk_rk = jnp.concatenate([jnp.where(strict, m_k[:L], 0.0), jnp.where(incl, m_k[L:], 0.0)], axis=0)
        z["kv"] = _dot(m_kk_rk, group_rows(z["v"][0], z["v"][1]))
        z["x"] = eye_cat
    for step in range(L - 1):
        for i, d, grp in groups:
            z = st[i, d, grp]
            u = step if d == 0 else L - 1 - step
            coef = jnp.take_along_axis(z["n_kb"], lane_head0 + u, axis=1)
            z["x"] = z["x"] - coef * z["x"][u:u + 1, :]
    for i, d, grp in groups:
        z = st[i, d, grp]
        x, kt, kv = z["x"], z["kt"], z["kv"]
        x_hi = x.astype(BF16)
        x_lo = x - x_hi.astype(F32)
        rhs = jnp.concatenate([group_rows(kt[0], kt[1]), group_rows(kv[:L, :LANES], kv[:L, LANES:])], axis=1)
        z["w"] = _dot(x_hi, rhs) + _dot(x_lo, rhs)
    for i, d, grp in groups:
        z = st[i, d, grp]
        w = z["w"]
        z["sa"] = [_dot_nt(w[:, q * LANES:(q + 1) * LANES], s_s[i, d, 2 * grp + q]) + w[:, (2 + q) * LANES:(3 + q) * LANES]
                   for q in range(2)]
    for i, d, grp in groups:
        z = st[i, d, grp]
        rb_sa = _dot(z["m_rb"], group_rows(z["sa"][0], z["sa"][1]))
        for q in range(2):
            p = 2 * grp + q
            cols = slice(q * LANES, (q + 1) * LANES)
            refs[d][6][i, :, p * LANES:(p + 1) * LANES] = _dot_nt(z["rt"][q], s_s[i, d, p]) + z["kv"][L:, cols] - rb_sa[:, cols]
    for i, d, grp in groups:
        z = st[i, d, grp]
        for q in range(2):
            p = 2 * grp + q
            s_new = s_s[i, d, p] * z["e_last"][q] + _dot_tn(z["v"][q], z["kd"][q]) - _dot_tn(z["sa"][q], z["bd"][q])
            s_s[i, d, p] = jnp.where(same_head, s_new, 0.0)

    @pl.when(c == pl.num_programs(1) - 1)
    def _():
        sn_ref[...] = s_s[...]


def _rwkv_scan(r, k2, v, kk, kb, lw, s0, *, nb=SCAN_BATCH_ROWS):
    b, t_len, d = r.shape
    L, pairs = RWKV_CHUNK, RWKV_PAIRS
    nc = t_len // L
    fwd = pl.BlockSpec((nb, L, d), lambda i, c: (i, c, 0))
    bwd = pl.BlockSpec((nb, L, d), lambda i, c: (i, nc - 1 - c, 0))
    st = (2, pairs, LANES, LANES)
    state = pl.BlockSpec((nb,) + st, lambda i, c: (i, 0, 0, 0, 0))
    return _call(
        _rwkv_scan_kernel, grid=(b // nb, nc),
        in_specs=[fwd] * 5 + [pl.BlockSpec((1, nb, L, d), lambda i, c: (0, i, c, 0))]
        + [bwd] * 5 + [pl.BlockSpec((1, nb, L, d), lambda i, c: (1, i, nc - 1 - c, 0)), state],
        out_specs=[fwd, bwd, state],
        out_shape=[jax.ShapeDtypeStruct((b, t_len, d), F32)] * 2 + [jax.ShapeDtypeStruct((b,) + st, F32)],
        scratch=[pltpu.VMEM((nb,) + st, F32)], sem=("parallel", "arbitrary"), name="rwkv_scan")(
            r, k2, v, kk, kb, lw, r, k2, v, kk, kb, lw, s0)


def _rwkv_post_kernel(yf_ref, yb_ref, bonus_ref, g_ref, lng_ref, lnb_ref, out_ref):
    for j in range(D_MODEL // LANES):
        sl = slice(j * LANES, (j + 1) * LANES)
        y = yf_ref[:, sl] + yb_ref[:, sl]
        y = y - _segsum(y) * (1.0 / RWKV_N)
        y = y * lax.rsqrt(_segsum(y * y) * (1.0 / RWKV_N) + RWKV_GN_EPS)
        out_ref[:, sl] = ((y * lng_ref[:, sl] + lnb_ref[:, sl] + bonus_ref[:, sl]) * g_ref[:, sl]).astype(BF16)


def _rwkv_post(yf, yb, bonus, g, ln_g, ln_b, *, tm=256):
    m, d = yf.shape
    tm = min(tm, m)
    tok = pl.BlockSpec((tm, d), lambda i: (i, 0))
    par = pl.BlockSpec((1, d), lambda i: (0, 0))
    return _call(_rwkv_post_kernel, grid=(m // tm,), in_specs=[tok, tok, tok, tok, par, par], out_specs=tok,
                 out_shape=jax.ShapeDtypeStruct((m, d), BF16), sem=("parallel",), name="rwkv_post")(
                     yf, yb, bonus, g, ln_g.reshape(1, d), ln_b.reshape(1, d))


def _pad_cols(w, n):
    return jnp.zeros(w.shape[:-1] + (n,), w.dtype).at[..., :w.shape[-1]].set(w)


def _pad_rows(w, n):
    return jnp.zeros((n,) + w.shape[1:], w.dtype).at[:w.shape[0]].set(w)


def _rwkv_mixer(h, b, init, j, mu, w_rkv, w0, w1, w2, a0, a1, a2, g1, g2, k_k, k_a, r_k, ln_g, ln_b, w_out, x, gate1):
    m, d = h.shape
    t_len = m // b
    xmix = _rwkv_mix(h.reshape(b, t_len, d), mu[j]).reshape(6, m, d)
    proj = lambda idx, w, wlead, name, **kw: _mm(xmix, w, xlead=(idx,), wlead=wlead, tm=512, tn=1024, name=name, **kw)
    r = proj(0, w_rkv, (j, 0), "rwkv_r")
    k = proj(2, w_rkv, (j, 1), "rwkv_k")
    v = proj(3, w_rkv, (j, 2), "rwkv_v")
    rank_w = w1.shape[-1]
    w1cat = jnp.concatenate([w1[j, 0], w1[j, 1]], axis=1)
    w2bd = jnp.zeros((2 * rank_w, 2 * d), F32).at[:rank_w, :d].set(w2[j, 0]).at[rank_w:, d:].set(w2[j, 1])
    tw = proj(1, w1cat, (), "rwkv_w1", epilogue=jnp.tanh)
    lora = _mm(tw, w2bd, tm=512, tn=1024, name="rwkv_w2")
    a_pre = _mm(proj(4, _pad_cols(a1[j], LANES), (), "rwkv_a1"), _pad_rows(a2[j], LANES), tm=512, tn=1024, name="rwkv_a2")
    g = _mm(proj(5, g1[j], (), "rwkv_g1", epilogue=_sigmoid), g2[j], tm=512, tn=1024, name="rwkv_g2")
    kk, kb, k2, bonus, lw = _rwkv_prep(r, k, v, lora, a_pre, w0[j], a0[j], k_k[j], k_a[j], r_k[j].reshape(d))
    s3 = lambda z: z.reshape(b, t_len, d)
    yf, yb, sn = _rwkv_scan(s3(r), s3(k2), s3(v), s3(kk), s3(kb), lw.reshape(2, b, t_len, d), init)
    y = _rwkv_post(yf.reshape(m, d), yb.reshape(m, d), bonus, g, ln_g[j], ln_b[j])
    x = _mm(y, w_out, wlead=(j,), tm=512, tn=1024, extras=[x, gate1],
            epilogue=_residual_epilogue, name="rwkv_out")
    return x, sn


def _rwkv_state_to_pairs(s):
    b = s.shape[0]
    n = RWKV_N
    out = jnp.zeros((b, 2, RWKV_PAIRS, 2 * n, 2 * n), F32)
    return out.at[..., :n, :n].set(s[:, :, 0::2]).at[..., n:, n:].set(s[:, :, 1::2])


def _rwkv_state_from_pairs(sp):
    n = RWKV_N
    b = sp.shape[0]
    return jnp.stack([sp[..., :n, :n], sp[..., n:, n:]], axis=3).reshape(b, 2, RWKV_HEADS, n, n)


def _router_kernel(x_ref, w_ref, b_ref, o_ref):
    e0, per = ROUTER_EXPERT_LANE0, MOE_PER_GROUP
    logits = _dot_f32(x_ref[...], w_ref[...]) + b_ref[...]
    lane = lax.broadcasted_iota(jnp.int32, logits.shape, 1).astype(F32)
    big = float(LANES)
    rmax = lambda z: jnp.max(z, axis=-1, keepdims=True)
    rsum = lambda z: jnp.sum(z, axis=-1, keepdims=True)
    first = lambda m: jnp.min(jnp.where(m, lane, big), axis=-1, keepdims=True)
    is_g = lane < MOE_GROUPS
    gmax = rmax(jnp.where(is_g, logits, -jnp.inf))
    gsel = first(jnp.logical_and(is_g, logits >= gmax))
    gw = 1.0 / rsum(jnp.where(is_g, jnp.exp(logits - gmax), 0.0))
    in_grp = jnp.logical_and(lane >= e0 + gsel * per, lane < e0 + (gsel + 1) * per)
    emax = rmax(jnp.where(in_grp, logits, -jnp.inf))
    p = jnp.where(in_grp, jnp.exp(logits - emax), 0.0)
    p = p / rsum(p)
    p1 = rmax(jnp.where(in_grp, p, -1.0))
    i1 = first(jnp.logical_and(in_grp, p >= p1))
    rest = jnp.logical_and(in_grp, lane != i1)
    p2 = rmax(jnp.where(rest, p, -1.0))
    i2 = first(jnp.logical_and(rest, p >= p2))
    tot = p1 + p2
    o_ref[...] = jnp.where(lane == i1, p1 / tot * gw, jnp.where(lane == i2, p2 / tot * gw, 0.0))


def _router(x, w_group, b_group, w_expert, b_expert, *, tm=512):
    m, d = x.shape
    tm = min(tm, m)
    e0 = ROUTER_EXPERT_LANE0
    w = jnp.zeros((d, LANES), F32).at[:, :MOE_GROUPS].set(w_group).at[:, e0:e0 + MOE_EXPERTS].set(w_expert)
    bias = jnp.zeros((1, LANES), F32).at[0, :MOE_GROUPS].set(b_group).at[0, e0:e0 + MOE_EXPERTS].set(b_expert)
    return _call(_router_kernel, grid=(m // tm,),
                 in_specs=[pl.BlockSpec((tm, d), lambda i: (i, 0)), pl.BlockSpec((d, LANES), lambda i: (0, 0)),
                           pl.BlockSpec((1, LANES), lambda i: (0, 0))],
                 out_specs=pl.BlockSpec((tm, LANES), lambda i: (i, 0)),
                 out_shape=jax.ShapeDtypeStruct((m, LANES), F32), sem=("parallel",), name="moe_router")(x, w, bias)


def _moe_ffn_kernel(h_ref, comb_ref, wg_ref, wu_ref, wd_ref, x_ref, gate_ref, o_ref, acc):
    e = pl.program_id(1)

    @pl.when(e == 0)
    def _():
        acc[...] = jnp.zeros_like(acc)

    h = h_ref[...].astype(BF16)
    comb = comb_ref[...]
    lane = lax.broadcasted_iota(jnp.int32, comb.shape, 1)
    ce = jnp.sum(jnp.where(lane == e + ROUTER_EXPERT_LANE0, comb, 0.0), axis=-1, keepdims=True)
    hid = _silu(_dot(h, wg_ref[0, 0])) * _dot(h, wu_ref[0, 0])
    acc[...] += _dot(hid * ce, wd_ref[0, 0])

    @pl.when(e == pl.num_programs(1) - 1)
    def _():
        o_ref[...] = x_ref[...] + gate_ref[0] * acc[...]


def _moe_ffn(h, comb, w_gate, w_up, w_down, layer, x, gate2, *, tm=1024):
    m, d = h.shape
    tm = min(tm, m)
    f = w_gate.shape[-1]
    tiles_per_row = (m // gate2.shape[0]) // tm
    tok = pl.BlockSpec((tm, d), lambda i, e: (i, 0))
    return _call(_moe_ffn_kernel, grid=(m // tm, MOE_EXPERTS),
                 in_specs=[tok, pl.BlockSpec((tm, LANES), lambda i, e: (i, 0)),
                           pl.BlockSpec((1, 1, d, f), lambda i, e: (layer, e, 0, 0)),
                           pl.BlockSpec((1, 1, d, f), lambda i, e: (layer, e, 0, 0)),
                           pl.BlockSpec((1, 1, f, d), lambda i, e: (layer, e, 0, 0)),
                           tok, pl.BlockSpec((1, 1, d), lambda i, e: (i // tiles_per_row, 0, 0))],
                 out_specs=tok, out_shape=jax.ShapeDtypeStruct((m, d), F32),
                 scratch=[pltpu.VMEM((tm, d), F32)], sem=("parallel", "arbitrary"), name="moe_ffn")(
                     h, comb, w_gate, w_up, w_down, x, gate2)


def _run_trunk(x3, mods, rows, init_a, init_b, init_c, p):
    b, t_len, d = x3.shape
    m = b * t_len
    x = x3.reshape(m, d)
    new_a, new_b, new_c = [], [], []
    for i in range(p["ada_w"].shape[0]):
        j = i // 3
        shift1, scale1, gate1, shift2, scale2, gate2 = mods[i]
        h = _normmod(x, p["norm1_g"][i], scale1, shift1, out_dtype=F32 if i % 3 == 2 else BF16)
        if i % 3 == 0:
            x, st = _mlstm_mixer(h, b, rows, tuple(s[:, j] for s in init_a), p["mlstm_w_in"][j], p["mlstm_conv_w"][j],
                                 p["mlstm_w_gate"][j], p["mlstm_b_gate"][j], p["mlstm_norm_g"][j], p["mlstm_w_out"][j],
                                 x, gate1)
            new_a.append(st)
        elif i % 3 == 1:
            x, st = _gla_mixer(h, b, rows, tuple(s[:, j] for s in init_b), p["gla_w_in"][j], p["gla_conv_w"][j],
                               p["gla_w_a1"][j], p["gla_w_a2"][j], p["gla_b_a"][j], p["gla_norm_g"][j], p["gla_w_out"][j],
                               x, gate1)
            new_b.append(st)
        else:
            s0 = (jnp.zeros((b, 2, RWKV_PAIRS, LANES, LANES), F32) if init_c is None
                  else _rwkv_state_to_pairs(init_c[0][:, j]))
            x, st = _rwkv_mixer(h, b, s0, j,p["rwkv_mu"], p["rwkv_w_rkv"], p["rwkv_w0"],
                                p["rwkv_w1"], p["rwkv_w2"], p["rwkv_a0"], p["rwkv_a1"], p["rwkv_a2"], p["rwkv_g1"],
                                p["rwkv_g2"], p["rwkv_k_k"], p["rwkv_k_a"], p["rwkv_r_k"], p["rwkv_ln_g"], p["rwkv_ln_b"],
                                p["rwkv_w_out"], x, gate1)
            new_c.append((_rwkv_state_from_pairs(st),))
        hff = _normmod(x, p["norm2_g"][i], scale2, shift2)
        comb = _router(hff, p["moe_w_group"][i], p["moe_b_group"][i], p["moe_w_expert"][i], p["moe_b_expert"][i])
        x = _moe_ffn(hff, comb, p["moe_w_gate"], p["moe_w_up"], p["moe_w_down"], i, x, gate2)
    stack = lambda per_layer: tuple(jnp.stack(parts, axis=1) for parts in zip(*per_layer))
    y = _rmsnorm(x, p["final_norm_g"]).reshape(b, t_len, d)
    return y, stack(new_a), stack(new_b), stack(new_c)


def kernel(x_prompt, x_sample, state_mlstm_C, state_mlstm_n, state_mlstm_m, state_gla_S, state_rwkv_S, c, c_ctx, ada_w, ada_b, norm1_g, norm2_g, moe_w_group, moe_b_group, moe_w_expert, moe_b_expert, moe_w_gate, moe_w_up, moe_w_down, final_norm_g, mlstm_w_in, mlstm_conv_w, mlstm_w_gate, mlstm_b_gate, mlstm_norm_g, mlstm_w_out, gla_w_in, gla_conv_w, gla_w_a1, gla_w_a2, gla_b_a, gla_norm_g, gla_w_out, rwkv_mu, rwkv_w_rkv, rwkv_w0, rwkv_w1, rwkv_w2, rwkv_a0, rwkv_a1, rwkv_a2, rwkv_g1, rwkv_g2, rwkv_k_k, rwkv_k_a, rwkv_r_k, rwkv_ln_g, rwkv_ln_b, rwkv_w_out):
    p = dict(ada_w=ada_w, norm1_g=norm1_g, norm2_g=norm2_g, moe_w_group=moe_w_group, moe_b_group=moe_b_group,
             moe_w_expert=moe_w_expert, moe_b_expert=moe_b_expert, moe_w_gate=moe_w_gate, moe_w_up=moe_w_up,
             moe_w_down=moe_w_down, final_norm_g=final_norm_g, mlstm_w_in=mlstm_w_in, mlstm_conv_w=mlstm_conv_w,
             mlstm_w_gate=mlstm_w_gate, mlstm_b_gate=mlstm_b_gate, mlstm_norm_g=mlstm_norm_g, mlstm_w_out=mlstm_w_out,
             gla_w_in=gla_w_in, gla_conv_w=gla_conv_w, gla_w_a1=gla_w_a1, gla_w_a2=gla_w_a2, gla_b_a=gla_b_a,
             gla_norm_g=gla_norm_g, gla_w_out=gla_w_out, rwkv_mu=rwkv_mu, rwkv_w_rkv=rwkv_w_rkv, rwkv_w0=rwkv_w0,
             rwkv_w1=rwkv_w1, rwkv_w2=rwkv_w2, rwkv_a0=rwkv_a0, rwkv_a1=rwkv_a1, rwkv_a2=rwkv_a2, rwkv_g1=rwkv_g1,
             rwkv_g2=rwkv_g2, rwkv_k_k=rwkv_k_k, rwkv_k_a=rwkv_k_a, rwkv_r_k=rwkv_r_k, rwkv_ln_g=rwkv_ln_g,
             rwkv_ln_b=rwkv_ln_b, rwkv_w_out=rwkv_w_out)
    for name in ("moe_w_gate", "moe_w_up", "moe_w_down", "mlstm_w_in", "mlstm_w_out", "gla_w_in", "gla_w_out",
                 "rwkv_w_rkv", "rwkv_w_out"):
        p[name] = p[name].astype(BF16)
    depth, d = ada_w.shape[0], ada_w.shape[1]
    n_dec = c.shape[0]
    cond8 = jnp.zeros((8, d), F32).at[0].set(c_ctx).at[1:1 + n_dec].set(c)
    mod = _ada(cond8, ada_w, ada_b)
    split = lambda rows: [tuple(rows[i][:, None, k * d:(k + 1) * d] for k in range(6)) for i in range(depth)]
    mods_ctx = split(mod[:, 0:1])
    mods_dec = split(mod[:, 1:1 + n_dec])

    bp = x_prompt.shape[0]
    zeros_like_ctx = lambda s: jnp.zeros((bp,) + s.shape[1:], F32)
    y_prompt, st_a, st_b, st_c = _run_trunk(
        x_prompt, mods_ctx, 1, tuple(zeros_like_ctx(s) for s in (state_mlstm_C, state_mlstm_n, state_mlstm_m)),
        (zeros_like_ctx(state_gla_S),), None, p)
    y_sample = _run_trunk(x_sample, mods_dec, x_sample.shape[1] // GRID_W,
                          (state_mlstm_C, state_mlstm_n, state_mlstm_m), (state_gla_S,), (state_rwkv_S,), p)[0]
    return (y_prompt, y_sample, st_a[0], st_a[1], st_a[2], st_b[0], st_c[0])
```

```python
import functools

import jax
import jax.numpy as jnp
from jax import lax
from jax.experimental import pallas as pl
from jax.experimental.pallas import tpu as pltpu

F32 = jnp.float32
BF16 = jnp.bfloat16

D_MODEL = 1024
GRID_W = 64
NORM_EPS = 1e-6

MLSTM_HEADS = 4
MLSTM_DK = 128
MLSTM_DV = 256
MLSTM_CHUNK = 128

GLA_HEADS = 4
GLA_DK = 128
GLA_DV = 256
GLA_TAU = 16.0
GLA_CHUNK = 32

RWKV_N = 64
RWKV_HEADS = 16
RWKV_PAIRS = RWKV_HEADS // 2
RWKV_GN_EPS = 64e-5
RWKV_CHUNK = 32

MOE_GROUPS = 4
MOE_PER_GROUP = 4
MOE_EXPERTS = 16
ROUTER_EXPERT_LANE0 = 8
ROUTER_GROUP_LANE = 127
MOE_BLOCK_TOKENS = 2048
MOE_ROW_TILE = 256

CONV_BLOCK_BYTES = 2 * 1024 * 1024
SCAN_BATCH_ROWS = 2

LANES = 128
VMEM_LIMIT_BYTES = 56 * 1024 * 1024


def _call(kernel, *, grid, in_specs, out_specs, out_shape, scratch=(), sem, name):
    return pl.pallas_call(
        kernel, grid=grid, in_specs=in_specs, out_specs=out_specs, out_shape=out_shape,
        scratch_shapes=list(scratch), name=name,
        compiler_params=pltpu.CompilerParams(dimension_semantics=sem, vmem_limit_bytes=VMEM_LIMIT_BYTES))


def _sigmoid(x):
    return 1.0 / (1.0 + jnp.exp(-x))


def _silu(x):
    return x * _sigmoid(x)


def _logsig(x):
    return jnp.minimum(x, 0.0) - jnp.log1p(jnp.exp(-jnp.abs(x)))


def _dot(a, b):
    return jnp.dot(a.astype(BF16), b.astype(BF16), preferred_element_type=F32)


def _dot_nt(a, b):
    return lax.dot_general(a.astype(BF16), b.astype(BF16), (((1,), (1,)), ((), ())), preferred_element_type=F32)


def _dot_tn(a, b):
    return lax.dot_general(a.astype(BF16), b.astype(BF16), (((0,), (0,)), ((), ())), preferred_element_type=F32)


def _split_bf16(x, parts):
    out = []
    for _ in range(parts):
        piece = x.astype(BF16)
        out.append(piece)
        x = x - piece.astype(F32)
    return out


def _dot_exact_lhs(a, b):
    a = a.astype(BF16)
    return sum(jnp.dot(a, piece, preferred_element_type=F32) for piece in _split_bf16(b, 3))


def _dot_3pass(a, b):
    (a_hi, a_lo), (b_hi, b_lo) = _split_bf16(a, 2), _split_bf16(b, 2)
    dot = lambda p, q: jnp.dot(p, q, preferred_element_type=F32)
    return dot(a_hi, b_hi) + (dot(a_hi, b_lo) + dot(a_lo, b_hi))


def _tri(n, kind, block=None):
    r = lax.broadcasted_iota(jnp.int32, (n, n), 0)
    c = lax.broadcasted_iota(jnp.int32, (n, n), 1)
    m = {"lower": c <= r, "upper": c >= r, "all": c >= 0}[kind]
    if block is not None:
        sh = block.bit_length() - 1
        m = jnp.logical_and(m, jnp.right_shift(r, sh) == jnp.right_shift(c, sh))
    return jnp.where(m, 1.0, 0.0).astype(F32)


def _mm(x, w, *, tm, tn, xlead=(), wlead=(), extras=(), epilogue=None, out_dtype=F32, name):
    m, k = x.shape[-2:]
    n = w.shape[-1]
    tm, tn = min(tm, m), min(tn, n)

    def kern(x_ref, w_ref, *rest):
        acc = _dot(x_ref[...], w_ref[...])
        if epilogue is not None:
            acc = epilogue(acc, *[r[...] for r in rest[:-1]])
        rest[-1][...] = acc.astype(out_dtype)

    def extra_spec(arr):
        if arr.ndim == 2:
            return pl.BlockSpec((tm, tn), lambda i, j: (i, j))
        tiles_per_row = (m // arr.shape[0]) // tm
        return pl.BlockSpec((1, 1, tn), lambda i, j: (i // tiles_per_row, 0, j))

    in_specs = [
        pl.BlockSpec((None,) * len(xlead) + (tm, k), lambda i, j: tuple(xlead) + (i, 0)),
        pl.BlockSpec((None,) * len(wlead) + (k, tn), lambda i, j: tuple(wlead) + (0, j)),
    ] + [extra_spec(a) for a in extras]
    return _call(kern, grid=(m // tm, n // tn), in_specs=in_specs,
                 out_specs=pl.BlockSpec((tm, tn), lambda i, j: (i, j)),
                 out_shape=jax.ShapeDtypeStruct((m, n), out_dtype),
                 sem=("parallel", "parallel"), name=name)(x, w, *extras)


def _residual_epilogue(acc, x, gate):
    return x + gate[0] * acc


def _ada_kernel(c_ref, w_ref, b_ref, o_ref):
    o_ref[0] = _dot(_silu(c_ref[...]), w_ref[0]) + b_ref[0]


def _ada(cond8, ada_w, ada_b):
    depth, d, n = ada_w.shape
    tn = 1536
    return _call(_ada_kernel, grid=(depth, n // tn),
                 in_specs=[pl.BlockSpec((8, d), lambda l, j: (0, 0)),
                           pl.BlockSpec((1, d, tn), lambda l, j: (l, 0, j)),
                           pl.BlockSpec((1, 1, tn), lambda l, j: (l, 0, j))],
                 out_specs=pl.BlockSpec((1, 8, tn), lambda l, j: (l, 0, j)),
                 out_shape=jax.ShapeDtypeStruct((depth, 8, n), F32),
                 sem=("parallel", "parallel"), name="ada")(cond8, ada_w, ada_b.reshape(depth, 1, n))


def _normmod_kernel(x_ref, g_ref, sc_ref, sh_ref, o_ref):
    x = x_ref[...]
    y = x * lax.rsqrt(jnp.mean(x * x, axis=-1, keepdims=True) + NORM_EPS)
    o_ref[...] = ((y * g_ref[...]) * (1.0 + sc_ref[0]) + sh_ref[0]).astype(o_ref.dtype)


def _resid_normmod_kernel(x_ref, y_ref, gate_ref, g_ref, sc_ref, sh_ref, xo_ref, o_ref):
    x = x_ref[...] + gate_ref[0] * y_ref[...]
    xo_ref[...] = x
    y = x * lax.rsqrt(jnp.mean(x * x, axis=-1, keepdims=True) + NORM_EPS)
    o_ref[...] = ((y * g_ref[...]) * (1.0 + sc_ref[0]) + sh_ref[0]).astype(o_ref.dtype)


def _normmod(x, g, scale, shift, *, resid=None, out_dtype=F32, tm=512):
    m, d = x.shape
    tm = min(tm, m)
    tiles_per_row = (m // scale.shape[0]) // tm
    mod = pl.BlockSpec((1, 1, d), lambda i: (i // tiles_per_row, 0, 0))
    tok = pl.BlockSpec((tm, d), lambda i: (i, 0))
    par = pl.BlockSpec((1, d), lambda i: (0, 0))
    if resid is None:
        return _call(_normmod_kernel, grid=(m // tm,), in_specs=[tok, par, mod, mod], out_specs=tok,
                     out_shape=jax.ShapeDtypeStruct((m, d), out_dtype), sem=("parallel",), name="normmod")(
                         x, g.reshape(1, d), scale, shift)
    y, gate = resid
    return _call(_resid_normmod_kernel, grid=(m // tm,), in_specs=[tok, tok, mod, par, mod, mod], out_specs=[tok, tok],
                 out_shape=[jax.ShapeDtypeStruct((m, d), F32), jax.ShapeDtypeStruct((m, d), out_dtype)],
                 sem=("parallel",), name="resid_normmod")(x, y, gate, g.reshape(1, d), scale, shift)


def _resid_rmsnorm_kernel(x_ref, y_ref, gate_ref, g_ref, o_ref):
    x = x_ref[...] + gate_ref[0] * y_ref[...]
    o_ref[...] = x * lax.rsqrt(jnp.mean(x * x, axis=-1, keepdims=True) + NORM_EPS) * g_ref[...]


def _resid_rmsnorm(x, y, gate, g, *, tm=512):
    m, d = x.shape
    tm = min(tm, m)
    tiles_per_row = (m // gate.shape[0]) // tm
    tok = pl.BlockSpec((tm, d), lambda i: (i, 0))
    return _call(_resid_rmsnorm_kernel, grid=(m // tm,),
                 in_specs=[tok, tok, pl.BlockSpec((1, 1, d), lambda i: (i // tiles_per_row, 0, 0)),
                           pl.BlockSpec((1, d), lambda i: (0, 0))],
                 out_specs=tok, out_shape=jax.ShapeDtypeStruct((m, d), F32), sem=("parallel",), name="final_norm")(
                     x, y, gate, g.reshape(1, d))


def _conv_kernel(x_ref, w_ref, o_ref, *, rows, width):
    x = x_ref[0]
    t_len = x.shape[0]
    t = lax.broadcasted_iota(jnp.int32, x.shape, 0)
    col = jnp.bitwise_and(t, width - 1)
    row = jnp.right_shift(t, width.bit_length() - 1)
    acc = jnp.zeros_like(x)
    for dr in (-1, 0, 1):
        if rows == 1 and dr != 0:
            continue
        for dc in (-1, 0, 1):
            off = dr * width + dc
            xs = x if off == 0 else pltpu.roll(x, (-off) % t_len, axis=0)
            ok = jnp.where(jnp.logical_and(col + dc >= 0, col + dc < width), 1.0, 0.0)
            if dr != 0:
                ok = ok * jnp.where(jnp.logical_and(row + dr >= 0, row + dr < rows), 1.0, 0.0)
            tap = (dr + 1) * 3 + (dc + 1)
            acc = acc + (xs * ok) * w_ref[tap:tap + 1, :]
    o_ref[0] = _silu(acc)


def _conv_silu(big, conv_w, *, rows, width, channels):
    b, t_len, _ = big.shape
    assert width & (width - 1) == 0 and rows * width == t_len
    tc = max(2 * LANES, min(channels, CONV_BLOCK_BYTES // (4 * t_len)))
    return _call(functools.partial(_conv_kernel, rows=rows, width=width), grid=(b, channels // tc),
                 in_specs=[pl.BlockSpec((1, t_len, tc), lambda i, j: (i, 0, j)),
                           pl.BlockSpec((9, tc), lambda i, j: (0, j))],
                 out_specs=pl.BlockSpec((1, t_len, tc), lambda i, j: (i, 0, j)),
                 out_shape=jax.ShapeDtypeStruct((b, t_len, channels), F32),
                 sem=("parallel", "parallel"), name="conv_silu")(big, conv_w.reshape(9, channels))


def _mlstm_gate_kernel(h_ref, wi_ref, wf_ref, bi_ref, bf_ref, gb_ref, gw_ref, ga_ref):
    nh, L = MLSTM_HEADS, MLSTM_CHUNK
    h = h_ref[0]
    tg = h.shape[0]
    ig = _dot(h, wi_ref[...]) + bi_ref[...]
    lf = _logsig(_dot(h, wf_ref[...]) + bf_ref[...])
    lane = lax.broadcasted_iota(jnp.int32, ig.shape, 1)
    b = jnp.where(lane < nh, _dot_exact_lhs(_tri(tg, "lower", L), lf), _dot_exact_lhs(_tri(tg, "upper", L), lf))
    b_last = _dot_exact_lhs(_tri(tg, "all", L), lf)
    gb_ref[0] = b
    gw_ref[0] = b_last - b + ig
    ga_ref[0] = ig - b


def _mlstm_gates(h3, w_gate, b_gate, *, tg=256):
    b, t_len, d = h3.shape
    nh = MLSTM_HEADS
    tg = min(tg, t_len)
    wi = _pad_cols(jnp.concatenate([w_gate[0, :, :nh], w_gate[1, :, :nh]], axis=1), LANES)
    wf = _pad_cols(jnp.concatenate([w_gate[0, :, nh:], w_gate[1, :, nh:]], axis=1), LANES)
    bi = _pad_cols(jnp.concatenate([b_gate[0, :nh], b_gate[1, :nh]]).reshape(1, 2 * nh), LANES)
    bf = _pad_cols(jnp.concatenate([b_gate[0, nh:], b_gate[1, nh:]]).reshape(1, 2 * nh), LANES)
    full = lambda shape: pl.BlockSpec(shape, lambda i, c: (0,) * len(shape))
    tok = pl.BlockSpec((1, tg, LANES), lambda i, c: (i, c, 0))
    return _call(_mlstm_gate_kernel, grid=(b, t_len // tg),
                 in_specs=[pl.BlockSpec((1, tg, d), lambda i, c: (i, c, 0)),
                           full((d, LANES)), full((d, LANES)), full((1, LANES)), full((1, LANES))],
                 out_specs=[tok, tok, tok], out_shape=[jax.ShapeDtypeStruct((b, t_len, LANES), F32)] * 3,
                 sem=("parallel", "parallel"), name="mlstm_gates")(h3, wi, wf, bi, bf)


def _mlstm_scan_kernel(qkf_ref, qkb_ref, vf_ref, vb_ref, gbf_ref, gbb_ref, gwf_ref, gwb_ref, gaf_ref, gab_ref,
                       c0_ref, n0_ref, m0_ref, yf_ref, yb_ref, cn_ref, nn_ref, mn_ref, c_s, n_s, m_s):
    nh, L, dk, dv = MLSTM_HEADS, MLSTM_CHUNK, MLSTM_DK, MLSTM_DV
    c = pl.program_id(1)

    @pl.when(c == 0)
    def _():
        c_s[...] = c0_ref[...]
        n_s[...] = n0_ref[...]
        m_s[...] = m0_ref[...]

    ti = lax.broadcasted_iota(jnp.int32, (L, L), 0)
    si = lax.broadcasted_iota(jnp.int32, (L, L), 1)
    lane = lax.broadcasted_iota(jnp.int32, (L, LANES), 1)
    ones = jnp.ones((L, LANES), BF16)
    refs = ((qkf_ref, vf_ref, gbf_ref, gwf_ref, gaf_ref, yf_ref), (qkb_ref, vb_ref, gbb_ref, gwb_ref, gab_ref, yb_ref))
    chains = [(i, d, h) for i in range(c_s.shape[0]) for d in range(2) for h in range(nh)]
    st = {}
    for i, d, h in chains:
        qk_ref, v_ref, gb_ref, gw_ref, ga_ref, _ = refs[d]
        g = d * nh + h
        q = qk_ref[i, :, h * dk:(h + 1) * dk] * (dk ** -0.5)
        k = qk_ref[i, :, (nh + h) * dk:(nh + h + 1) * dk]
        v = v_ref[i, :, h * dv:(h + 1) * dv]
        a_row = sum(lax.dot_general(ones, piece, (((1,), (1,)), ((), ())), preferred_element_type=F32)
                    for piece in _split_bf16(jnp.where(lane == g, ga_ref[i], 0.0), 3))
        b_col = gb_ref[i, :, g:g + 1]
        wl_col = gw_ref[i, :, g:g + 1]
        st[i, d, h] = dict(q=q, k=k, v=v, a_row=a_row, b_col=b_col, wl_col=wl_col, qk=_dot_nt(q, k))
    for i, d, h in chains:
        z = st[i, d, h]
        m_st = m_s[i, d, h]
        mask = (si <= ti) if d == 0 else (si >= ti)
        log_d = jnp.where(mask, z["b_col"] + z["a_row"], -jnp.inf)
        m_t = jnp.maximum(z["b_col"] + m_st, jnp.max(log_d, axis=-1, keepdims=True))
        z.update(m_t=m_t, s=z["qk"] * jnp.exp(log_d - m_t), w_inter=jnp.exp(z["b_col"] + m_st - m_t))
    for i, d, h in chains:
        z = st[i, d, h]
        q, s, w_inter = z["q"], z["s"], z["w_inter"]
        num = _dot(s, z["v"]) + w_inter * _dot(q, c_s[i, d, h])
        den = jnp.sum(s, axis=-1, keepdims=True) + w_inter * jnp.sum(q * n_s[i, d, h], axis=-1, keepdims=True)
        refs[d][5][i, :, h * dv:(h + 1) * dv] = num / jnp.maximum(jnp.abs(den), jnp.exp(-z["m_t"]))
    for i, d, h in chains:
        z = st[i, d, h]
        last = L - 1 if d == 0 else 0
        m_st = m_s[i, d, h]
        b_last = z["b_col"][last:last + 1, :]
        m_new = jnp.maximum(b_last + m_st, jnp.max(z["wl_col"], axis=0, keepdims=True))
        ks = jnp.exp(z["wl_col"] - m_new) * z["k"]
        decay = jnp.exp(b_last + m_st - m_new)
        c_s[i, d, h] = decay * c_s[i, d, h] + _dot_tn(ks, z["v"])
        n_s[i, d, h] = decay * n_s[i, d, h] + jnp.sum(ks, axis=0, keepdims=True)
        m_s[i, d, h] = m_new

    @pl.when(c == pl.num_programs(1) - 1)
    def _():
        cn_ref[...] = c_s[...]
        nn_ref[...] = n_s[...]
        mn_ref[...] = m_s[...]


def _mlstm_scan(qk, big, gb, gw, ga, c0, n0, m0, *, nb=SCAN_BATCH_ROWS):
    b, t_len, _ = qk.shape
    nh, L, dk, dv = MLSTM_HEADS, MLSTM_CHUNK, MLSTM_DK, MLSTM_DV
    nc = t_len // L
    fwd = lambda i, c: (i, c, 0)
    bwd = lambda i, c: (i, nc - 1 - c, 0)
    state = lambda shape: pl.BlockSpec((nb,) + shape, lambda i, c: (i,) + (0,) * len(shape))
    st_shapes = [(2, nh, dk, dv), (2, nh, 1, dk), (2, nh, 1, 1)]
    return _call(
        _mlstm_scan_kernel, grid=(b // nb, nc),
        in_specs=[pl.BlockSpec((nb, L, 2 * nh * dk), fwd), pl.BlockSpec((nb, L, 2 * nh * dk), bwd),
                  pl.BlockSpec((nb, L, nh * dv), lambda i, c: (i, c, 1)),
                  pl.BlockSpec((nb, L, nh * dv), lambda i, c: (i, nc - 1 - c, 1)),
                  pl.BlockSpec((nb, L, LANES), fwd), pl.BlockSpec((nb, L, LANES), bwd),
                  pl.BlockSpec((nb, L, LANES), fwd), pl.BlockSpec((nb, L, LANES), bwd),
                  pl.BlockSpec((nb, L, LANES), fwd), pl.BlockSpec((nb, L, LANES), bwd)] + [state(s) for s in st_shapes],
        out_specs=[pl.BlockSpec((nb, L, nh * dv), fwd), pl.BlockSpec((nb, L, nh * dv), bwd)] + [state(s) for s in st_shapes],
        out_shape=[jax.ShapeDtypeStruct((b, t_len, nh * dv), F32)] * 2 + [jax.ShapeDtypeStruct((b,) + s, F32) for s in st_shapes],
        scratch=[pltpu.VMEM((nb,) + s, F32) for s in st_shapes],
        sem=("parallel", "arbitrary"), name="mlstm_scan")(qk, qk, big, big, gb, gb, gw, gw, ga, ga, c0, n0, m0)


def _mlstm_post_kernel(yf_ref, yb_ref, o_ref, g_ref, out_ref):
    dv = MLSTM_DV
    for h in range(MLSTM_HEADS):
        sl = slice(h * dv, (h + 1) * dv)
        y = yf_ref[:, sl] + yb_ref[:, sl]
        y = y - jnp.mean(y, axis=-1, keepdims=True)
        y = y * lax.rsqrt(jnp.mean(y * y, axis=-1, keepdims=True) + NORM_EPS)
        out_ref[:, sl] = (_sigmoid(o_ref[:, sl]) * (y * g_ref[:, sl])).astype(BF16)


def _mlstm_post(yf, yb, big, norm_g, *, tm=512):
    m, d = yf.shape
    tm = min(tm, m)
    row = lambda i: (i, 0)
    return _call(_mlstm_post_kernel, grid=(m // tm,),
                 in_specs=[pl.BlockSpec((tm, d), row), pl.BlockSpec((tm, d), row),
                           pl.BlockSpec((tm, d), lambda i: (i, 2)), pl.BlockSpec((1, d), lambda i: (0, 0))],
                 out_specs=pl.BlockSpec((tm, d), row), out_shape=jax.ShapeDtypeStruct((m, d), BF16),
                 sem=("parallel",), name="mlstm_post")(yf, yb, big, norm_g.reshape(1, d))


def _mlstm_mixer(h, b, rows, init, w_in, conv_w, w_gate, b_gate, norm_g, w_out, x, gate1):
    m, d = h.shape
    t_len = m // b
    nh, dk = MLSTM_HEADS, MLSTM_DK
    big = _mm(h, w_in, tm=1024, tn=1024, name="mlstm_in")
    big3 = big.reshape(b, t_len, 3 * d)
    qk = _conv_silu(big3, conv_w, rows=rows, width=t_len // rows, channels=2 * nh * dk)
    gb, gw, ga = _mlstm_gates(h.reshape(b, t_len, d), w_gate, b_gate)
    c0, n0, m0 = init
    yf, yb, cn, nn, mn = _mlstm_scan(qk, big3, gb, gw, ga, c0, n0.reshape(b, 2, nh, 1, dk), m0.reshape(b, 2, nh, 1, 1))
    y = _mlstm_post(yf.reshape(m, d), yb.reshape(m, d), big, norm_g)
    x = _mm(y, w_out, tm=512, tn=1024, extras=[x, gate1],
            epilogue=_residual_epilogue, name="mlstm_out")
    return x, (cn, nn.reshape(b, 2, nh, dk), mn.reshape(b, 2, nh))


def _gla_gate_kernel(h_ref, w1_ref, w2_ref, b_ref, gf_ref, gb_ref):
    half = GLA_HEADS * GLA_DK
    h = h_ref[0]
    z = _dot(_dot(h, w1_ref[...]), w2_ref[...]) + b_ref[...]
    lg = _logsig(z) * (1.0 / GLA_TAU)
    tg = h.shape[0]
    gf_ref[0] = _dot_exact_lhs(_tri(tg, "lower", GLA_CHUNK), lg[:, :half])
    gb_ref[0] = _dot_exact_lhs(_tri(tg, "upper", GLA_CHUNK), lg[:, half:])


def _gla_gates(h3, w_a1, w_a2, b_a, *, tg=256):
    b, t_len, d = h3.shape
    rank, half = w_a1.shape[-1], w_a2.shape[-1]
    w1 = jnp.zeros((d, LANES), F32).at[:, :rank].set(w_a1[0]).at[:, rank:2 * rank].set(w_a1[1])
    w2 = jnp.zeros((LANES, 2 * half), F32).at[:rank, :half].set(w_a2[0]).at[rank:2 * rank, half:].set(w_a2[1])
    bias = jnp.concatenate([b_a[0], b_a[1]]).reshape(1, 2 * half)
    tg = min(tg, t_len)
    full = lambda shape: pl.BlockSpec(shape, lambda i, c: (0,) * len(shape))
    tok = lambda n: pl.BlockSpec((1, tg, n), lambda i, c: (i, c, 0))
    return _call(_gla_gate_kernel, grid=(b, t_len // tg),
                 in_specs=[tok(d), full((d, LANES)), full((LANES, 2 * half)), full((1, 2 * half))],
                 out_specs=[tok(half), tok(half)],
                 out_shape=[jax.ShapeDtypeStruct((b, t_len, half), F32)] * 2,
                 sem=("parallel", "parallel"), name="gla_gates")(h3, w1, w2, bias)


def _gla_scan_kernel(xf_ref, xb_ref, gf_ref, gb_ref, s0_ref, yf_ref, yb_ref, sn_ref, s_s):
    nh, L, dk, dv = GLA_HEADS, GLA_CHUNK, GLA_DK, GLA_DV
    c = pl.program_id(1)

    @pl.when(c == 0)
    def _():
        s_s[...] = s0_ref[0]

    tcol = lax.broadcasted_iota(jnp.int32, (L, 1), 0)
    eye = jnp.where(lax.broadcasted_iota(jnp.int32, (dk, dk), 0) == lax.broadcasted_iota(jnp.int32, (dk, dk), 1), 1.0, 0.0)
    refs = ((xf_ref, gf_ref, yf_ref), (xb_ref, gb_ref, yb_ref))
    chains = [(d, h) for d in range(2) for h in range(nh)]
    st = {}
    for d, h in chains:
        x_ref, g_ref, _ = refs[d]
        q = x_ref[0, :, h * dk:(h + 1) * dk] * (dk ** -0.5)
        k = x_ref[0, :, (nh + h) * dk:(nh + h + 1) * dk]
        v = x_ref[0, :, 2 * nh * dk + h * dv:2 * nh * dk + (h + 1) * dv]
        g = g_ref[0, :, h * dk:(h + 1) * dk]
        st[d, h] = dict(q=q, k=k, v=v, g=g, o=_dot(q * jnp.exp(g), s_s[d, h]))
    sub = 8
    for s in range(L):
        for d, h in chains:
            z = st[d, h]
            q, k, v, g, o = z["q"], z["k"], z["v"], z["g"], z["o"]
            lo, hi = ((s // sub) * sub, L) if d == 0 else (0, (s // sub + 1) * sub)
            dec = jnp.exp(jnp.minimum(g[lo:hi] - g[s:s + 1, :], 0.0))
            col = jnp.sum(q[lo:hi] * k[s:s + 1, :] * dec, axis=-1, keepdims=True)
            col = jnp.where((tcol[lo:hi] >= s) if d == 0 else (tcol[lo:hi] <= s), col, 0.0)
            parts = [o[:lo]] * (lo > 0) + [o[lo:hi] + col * v[s:s + 1, :]] + [o[hi:]] * (hi < L)
            z["o"] = parts[0] if len(parts) == 1 else jnp.concatenate(parts, axis=0)
    for d, h in chains:
        z = st[d, h]
        k, v, g = z["k"], z["v"], z["g"]
        last = L - 1 if d == 0 else 0
        refs[d][2][0, :, h * dv:(h + 1) * dv] = z["o"]
        g_last = g[last:last + 1, :]
        decay_col = jnp.sum(eye * jnp.exp(g_last), axis=-1, keepdims=True)
        s_s[d, h] = decay_col * s_s[d, h] + _dot_tn(k * jnp.exp(g_last - g), v)

    @pl.when(c == pl.num_programs(1) - 1)
    def _():
        sn_ref[0] = s_s[...]


def _gla_scan(qkv, gf, gb, s0):
    b, t_len, width = qkv.shape
    nh, L, dk, dv = GLA_HEADS, GLA_CHUNK, GLA_DK, GLA_DV
    nc = t_len // L
    fwd = lambda i, c: (i, c, 0)
    bwd = lambda i, c: (i, nc - 1 - c, 0)
    st = (2, nh, dk, dv)
    state = pl.BlockSpec((1,) + st, lambda i, c: (i, 0, 0, 0, 0))
    return _call(
        _gla_scan_kernel, grid=(b, nc),
        in_specs=[pl.BlockSpec((1, L, width), fwd), pl.BlockSpec((1, L, width), bwd),
                  pl.BlockSpec((1, L, nh * dk), fwd), pl.BlockSpec((1, L, nh * dk), bwd), state],
        out_specs=[pl.BlockSpec((1, L, nh * dv), fwd), pl.BlockSpec((1, L, nh * dv), bwd), state],
        out_shape=[jax.ShapeDtypeStruct((b, t_len, nh * dv), F32)] * 2 + [jax.ShapeDtypeStruct((b,) + st, F32)],
        scratch=[pltpu.VMEM(st, F32)], sem=("parallel", "arbitrary"), name="gla_scan")(qkv, qkv, gf, gb, s0)


def _gla_post_kernel(yf_ref, yb_ref, r_ref, g_ref, out_ref):
    dv = GLA_DV
    for h in range(GLA_HEADS):
        sl = slice(h * dv, (h + 1) * dv)
        y = yf_ref[:, sl] + yb_ref[:, sl]
        y = y * lax.rsqrt(jnp.mean(y * y, axis=-1, keepdims=True) + NORM_EPS)
        out_ref[:, sl] = ((y * g_ref[:, sl]) * _silu(r_ref[:, sl])).astype(BF16)


def _gla_post(yf, yb, big, norm_g, *, tm=512):
    m, d = yf.shape
    tm = min(tm, m)
    row = lambda i: (i, 0)
    return _call(_gla_post_kernel, grid=(m // tm,),
                 in_specs=[pl.BlockSpec((tm, d), row), pl.BlockSpec((tm, d), row),
                           pl.BlockSpec((tm, d), lambda i: (i, 2)), pl.BlockSpec((1, d), lambda i: (0, 0))],
                 out_specs=pl.BlockSpec((tm, d), row), out_shape=jax.ShapeDtypeStruct((m, d), BF16),
                 sem=("parallel",), name="gla_post")(yf, yb, big, norm_g.reshape(1, d))


def _gla_mixer(h, b, rows, init, w_in, conv_w, w_a1, w_a2, b_a, norm_g, w_out, x, gate1):
    m, d = h.shape
    t_len = m // b
    big = _mm(h, w_in, tm=1024, tn=1024, name="gla_in")
    big3 = big.reshape(b, t_len, 3 * d)
    qkv = _conv_silu(big3, conv_w, rows=rows, width=t_len // rows, channels=2 * d)
    gf, gb = _gla_gates(h.reshape(b, t_len, d), w_a1, w_a2, b_a)
    yf, yb, sn = _gla_scan(qkv, gf, gb, init[0])
    y = _gla_post(yf.reshape(m, d), yb.reshape(m, d), big, norm_g)
    x = _mm(y, w_out, tm=512, tn=1024, extras=[x, gate1],
            epilogue=_residual_epilogue, name="gla_out")
    return x, (sn,)


def _rwkv_mix_kernel(h_ref, hp_ref, hn_ref, mu_ref, o_ref):
    i = pl.program_id(1)
    h = h_ref[0]
    tm = h.shape[0]
    prev_row = jnp.where(i > 0, hp_ref[0, 7:8, :], 0.0)
    next_row = jnp.where(i < pl.num_programs(1) - 1, hn_ref[0, 0:1, :], 0.0)
    row = lax.broadcasted_iota(jnp.int32, (tm, 1), 0)
    x_prev = jnp.where(row == 0, prev_row, pltpu.roll(h, 1, axis=0))
    x_next = jnp.where(row == tm - 1, next_row, pltpu.roll(h, tm - 1, axis=0))
    xx = 0.5 * (x_prev + x_next) - h
    for j in range(6):
        o_ref[j, 0] = (h + xx * mu_ref[j:j + 1, :]).astype(BF16)


def _rwkv_mix(h3, mu, *, tm=256):
    b, t_len, d = h3.shape
    tm = min(tm, t_len)
    nt, n8 = t_len // tm, t_len // 8
    return _call(_rwkv_mix_kernel, grid=(b, nt),
                 in_specs=[pl.BlockSpec((1, tm, d), lambda i, j: (i, j, 0)),
                           pl.BlockSpec((1, 8, d), lambda i, j: (i, jnp.maximum(j * (tm // 8) - 1, 0), 0)),
                           pl.BlockSpec((1, 8, d), lambda i, j: (i, jnp.minimum((j + 1) * (tm // 8), n8 - 1), 0)),
                           pl.BlockSpec((6, d), lambda i, j: (0, 0))],
                 out_specs=pl.BlockSpec((6, 1, tm, d), lambda i, j: (0, i, j, 0)),
                 out_shape=jax.ShapeDtypeStruct((6, b, t_len, d), BF16),
                 sem=("parallel", "parallel"), name="rwkv_mix")(h3, h3, h3, mu)


def _segsum(x):
    lo = lax.broadcasted_iota(jnp.int32, x.shape, 1) < RWKV_N
    s0 = jnp.sum(jnp.where(lo, x, 0.0), axis=-1, keepdims=True)
    s1 = jnp.sum(jnp.where(lo, 0.0, x), axis=-1, keepdims=True)
    return jnp.where(lo, s0, s1)


def _rwkv_prep_kernel(r_ref, k_ref, v_ref, lora_ref, ap_ref, w0_ref, a0_ref, kk_w_ref, ka_w_ref, rk_w_ref,
                      kk_ref, kb_ref, k2_ref, bonus_ref, lw_ref):
    d = D_MODEL
    for j in range(d // LANES):
        sl = slice(j * LANES, (j + 1) * LANES)
        r, k, v = r_ref[:, sl], k_ref[:, sl], v_ref[:, sl]
        a = _sigmoid(a0_ref[:, sl] + ap_ref[:, sl])
        kk = k * kk_w_ref[:, sl]
        kk = kk / jnp.maximum(jnp.sqrt(_segsum(kk * kk)), 1e-12)
        k2 = k * (1.0 + (a - 1.0) * ka_w_ref[:, sl])
        kk_ref[:, sl] = kk
        kb_ref[:, sl] = kk * a
        k2_ref[:, sl] = k2
        bonus_ref[:, sl] = _segsum(r * k2 * rk_w_ref[:, sl]) * v
        for z in range(2):
            lw_ref[z, :, sl] = -jnp.exp(_logsig(w0_ref[z:z + 1, sl] + lora_ref[:, z * d + j * LANES:z * d + (j + 1) * LANES]) - 0.5)


def _rwkv_prep(r, k, v, lora, a_pre, w0, a0, k_k, k_a, r_k, *, tm=256):
    m, d = r.shape
    tm = min(tm, m)
    row = lambda i: (i, 0)
    tok = pl.BlockSpec((tm, d), row)
    par = pl.BlockSpec((1, d), lambda i: (0, 0))
    return _call(_rwkv_prep_kernel, grid=(m // tm,),
                 in_specs=[tok, tok, tok, pl.BlockSpec((tm, 2 * d), row), tok,
                           pl.BlockSpec((2, d), lambda i: (0, 0)), par, par, par, par],
                 out_specs=[tok, tok, tok, tok, pl.BlockSpec((2, tm, d), lambda i: (0, i, 0))],
                 out_shape=[jax.ShapeDtypeStruct((m, d), F32)] * 4 + [jax.ShapeDtypeStruct((2, m, d), F32)],
                 sem=("parallel",), name="rwkv_prep")(
                     r, k, v, lora, a_pre, w0, a0.reshape(1, d), k_k.reshape(1, d), k_a.reshape(1, d), r_k.reshape(1, d))


def _rwkv_scan_kernel(rf, kf, vf, kkf, kbf, lwf, rb, kb_, vb, kkb, kbb, lwb, s0_ref, yf_ref, yb_ref, sn_ref, s_s):
    L, pairs = RWKV_CHUNK, RWKV_PAIRS
    c = pl.program_id(1)

    @pl.when(c == 0)
    def _():
        s_s[...] = s0_ref[...]

    row = lax.broadcasted_iota(jnp.int32, (L, LANES), 0)
    lane = lax.broadcasted_iota(jnp.int32, (L, LANES), 1)
    lane_tok = jnp.bitwise_and(lane, L - 1)
    lane_head0 = lane - lane_tok
    lo = lane < RWKV_N
    sq_r = lax.broadcasted_iota(jnp.int32, (LANES, LANES), 0) < RWKV_N
    sq_c = lax.broadcasted_iota(jnp.int32, (LANES, LANES), 1) < RWKV_N
    same_head = sq_r == sq_c
    zeros64 = jnp.zeros((2 * L, LANES), F32)
    eye_cat = jnp.where(lane_tok == row, 1.0, 0.0)

    def head_rows(a):
        return jnp.concatenate([jnp.where(lo, a, 0.0), jnp.where(lo, 0.0, a)], axis=0)

    def group_rows(a0, a1):
        return jnp.concatenate([jnp.concatenate([head_rows(a0), zeros64], axis=1),
                                jnp.concatenate([zeros64, head_rows(a1)], axis=1)], axis=0)

    refs = ((rf, kf, vf, kkf, kbf, lwf, yf_ref), (rb, kb_, vb, kkb, kbb, lwb, yb_ref))
    groups = [(i, d, grp) for i in range(s_s.shape[0]) for d in range(2) for grp in range(pairs // 2)]
    st = {}
    for i, d, grp in groups:
        r_ref, k_ref, v_ref, kk_ref, kb_ref, lw_ref, _ = refs[d]
        tri = _tri(L, "lower" if d == 0 else "upper")
        last = L - 1 if d == 0 else 0
        rt, kt, kd, bd, v, e_last, m_b, m_k = [], [], [], [], [], [], 0.0, 0.0
        for q in range(2):
            sl = slice((2 * grp + q) * LANES, (2 * grp + q + 1) * LANES)
            r, k, kk, kb, lw = r_ref[i, :, sl], k_ref[i, :, sl], kk_ref[i, :, sl], kb_ref[i, :, sl], lw_ref[0, i, :, sl]
            cum = _dot_exact_lhs(tri, lw)
            c_last = cum[last:last + 1, :]
            e_neg, e_end = jnp.exp(-cum), jnp.exp(c_last - cum)
            rt.append(r * jnp.exp(cum))
            kt.append(kk * jnp.exp(cum - lw))
            kd.append(k * e_end)
            bd.append(kb * e_end)
            v.append(v_ref[i, :, sl])
            e_last.append(jnp.exp(c_last))
            lhs = jnp.concatenate([kt[q], rt[q]], axis=0)
            pad = lambda a: jnp.concatenate([head_rows(a), zeros64] if q == 0 else [zeros64, head_rows(a)], axis=0)
            m_b = m_b + _dot_nt(lhs, pad(kb * e_neg))
            m_k = m_k + _dot_nt(lhs, pad(k * e_neg))
        st[i, d, grp] = dict(rt=rt, kt=kt, kd=kd, bd=bd, v=v, e_last=e_last, m_b=m_b, m_k=m_k)
    for i, d, grp in groups:
        z = st[i, d, grp]
        strict = (lane_tok < row) if d == 0 else (lane_tok > row)
        incl = (lane_tok <= row) if d == 0 else (lane_tok >= row)
        m_b, m_k = z["m_b"], z["m_k"]
        z["n_kb"] = jnp.where(strict, m_b[:L], 0.0)
        z["m_rb"] = jnp.where(incl, m_b[L:], 0.0)
        m_kk_rk = jnp.concatenate([jnp.where(strict, m_k[:L], 0.0), jnp.where(incl, m_k[L:], 0.0)], axis=0)
        z["kv"] = _dot(m_kk_rk, group_rows(z["v"][0], z["v"][1]))
        z["x"] = eye_cat
    for step in range(L - 1):
        for i, d, grp in groups:
            z = st[i, d, grp]
            u = step if d == 0 else L - 1 - step
            coef = jnp.take_along_axis(z["n_kb"], lane_head0 + u, axis=1)
            z["x"] = z["x"] - coef * z["x"][u:u + 1, :]
    for i, d, grp in groups:
        z = st[i, d, grp]
        x, kt, kv = z["x"], z["kt"], z["kv"]
        x_hi = x.astype(BF16)
        x_lo = x - x_hi.astype(F32)
        rhs = jnp.concatenate([group_rows(kt[0], kt[1]), group_rows(kv[:L, :LANES], kv[:L, LANES:])], axis=1)
        z["w"] = _dot(x_hi, rhs) + _dot(x_lo, rhs)
    for i, d, grp in groups:
        z = st[i, d, grp]
        w = z["w"]
        z["sa"] = [_dot_nt(w[:, q * LANES:(q + 1) * LANES], s_s[i, d, 2 * grp + q]) + w[:, (2 + q) * LANES:(3 + q) * LANES]
                   for q in range(2)]
    for i, d, grp in groups:
        z = st[i, d, grp]
        rb_sa = _dot(z["m_rb"], group_rows(z["sa"][0], z["sa"][1]))
        for q in range(2):
            p = 2 * grp + q
            cols = slice(q * LANES, (q + 1) * LANES)
            refs[d][6][i, :, p * LANES:(p + 1) * LANES] = _dot_nt(z["rt"][q], s_s[i, d, p]) + z["kv"][L:, cols] - rb_sa[:, cols]
    for i, d, grp in groups:
        z = st[i, d, grp]
        for q in range(2):
            p = 2 * grp + q
            s_new = s_s[i, d, p] * z["e_last"][q] + _dot_tn(z["v"][q], z["kd"][q]) - _dot_tn(z["sa"][q], z["bd"][q])
            s_s[i, d, p] = jnp.where(same_head, s_new, 0.0)

    @pl.when(c == pl.num_programs(1) - 1)
    def _():
        sn_ref[...] = s_s[...]


def _rwkv_scan(r, k2, v, kk, kb, lw, s0, *, nb=SCAN_BATCH_ROWS):
    b, t_len, d = r.shape
    L, pairs = RWKV_CHUNK, RWKV_PAIRS
    nc = t_len // L
    fwd = pl.BlockSpec((nb, L, d), lambda i, c: (i, c, 0))
    bwd = pl.BlockSpec((nb, L, d), lambda i, c: (i, nc - 1 - c, 0))
    st = (2, pairs, LANES, LANES)
    state = pl.BlockSpec((nb,) + st, lambda i, c: (i, 0, 0, 0, 0))
    return _call(
        _rwkv_scan_kernel, grid=(b // nb, nc),
        in_specs=[fwd] * 5 + [pl.BlockSpec((1, nb, L, d), lambda i, c: (0, i, c, 0))]
        + [bwd] * 5 + [pl.BlockSpec((1, nb, L, d), lambda i, c: (1, i, nc - 1 - c, 0)), state],
        out_specs=[fwd, bwd, state],
        out_shape=[jax.ShapeDtypeStruct((b, t_len, d), F32)] * 2 + [jax.ShapeDtypeStruct((b,) + st, F32)],
        scratch=[pltpu.VMEM((nb,) + st, F32)], sem=("parallel", "arbitrary"), name="rwkv_scan")(
            r, k2, v, kk, kb, lw, r, k2, v, kk, kb, lw, s0)


def _rwkv_post_kernel(yf_ref, yb_ref, bonus_ref, g_ref, lng_ref, lnb_ref, out_ref):
    for j in range(D_MODEL // LANES):
        sl = slice(j * LANES, (j + 1) * LANES)
        y = yf_ref[:, sl] + yb_ref[:, sl]
        y = y - _segsum(y) * (1.0 / RWKV_N)
        y = y * lax.rsqrt(_segsum(y * y) * (1.0 / RWKV_N) + RWKV_GN_EPS)
        out_ref[:, sl] = ((y * lng_ref[:, sl] + lnb_ref[:, sl] + bonus_ref[:, sl]) * g_ref[:, sl]).astype(BF16)


def _rwkv_post(yf, yb, bonus, g, ln_g, ln_b, *, tm=256):
    m, d = yf.shape
    tm = min(tm, m)
    tok = pl.BlockSpec((tm, d), lambda i: (i, 0))
    par = pl.BlockSpec((1, d), lambda i: (0, 0))
    return _call(_rwkv_post_kernel, grid=(m // tm,), in_specs=[tok, tok, tok, tok, par, par], out_specs=tok,
                 out_shape=jax.ShapeDtypeStruct((m, d), BF16), sem=("parallel",), name="rwkv_post")(
                     yf, yb, bonus, g, ln_g.reshape(1, d), ln_b.reshape(1, d))


def _pad_cols(w, n):
    return jnp.zeros(w.shape[:-1] + (n,), w.dtype).at[..., :w.shape[-1]].set(w)


def _pad_rows(w, n):
    return jnp.zeros((n,) + w.shape[1:], w.dtype).at[:w.shape[0]].set(w)


def _rwkv_mixer(h, b, init, j, mu, w_rkv, w0, w1, w2, a0, a1, a2, g1, g2, k_k, k_a, r_k, ln_g, ln_b, w_out, x, gate1):
    m, d = h.shape
    t_len = m // b
    xmix = _rwkv_mix(h.reshape(b, t_len, d), mu[j]).reshape(6, m, d)
    proj = lambda idx, w, wlead, name, **kw: _mm(xmix, w, xlead=(idx,), wlead=wlead, tm=512, tn=1024, name=name, **kw)
    r = proj(0, w_rkv, (j, 0), "rwkv_r")
    k = proj(2, w_rkv, (j, 1), "rwkv_k")
    v = proj(3, w_rkv, (j, 2), "rwkv_v")
    rank_w = w1.shape[-1]
    w1cat = jnp.concatenate([w1[j, 0], w1[j, 1]], axis=1)
    w2bd = jnp.zeros((2 * rank_w, 2 * d), F32).at[:rank_w, :d].set(w2[j, 0]).at[rank_w:, d:].set(w2[j, 1])
    tw = proj(1, w1cat, (), "rwkv_w1", epilogue=jnp.tanh)
    lora = _mm(tw, w2bd, tm=512, tn=1024, name="rwkv_w2")
    a_pre = _mm(proj(4, _pad_cols(a1[j], LANES), (), "rwkv_a1"), _pad_rows(a2[j], LANES), tm=512, tn=1024, name="rwkv_a2")
    g = _mm(proj(5, g1[j], (), "rwkv_g1", epilogue=_sigmoid), g2[j], tm=512, tn=1024, name="rwkv_g2")
    kk, kb, k2, bonus, lw = _rwkv_prep(r, k, v, lora, a_pre, w0[j], a0[j], k_k[j], k_a[j], r_k[j].reshape(d))
    s3 = lambda z: z.reshape(b, t_len, d)
    yf, yb, sn = _rwkv_scan(s3(r), s3(k2), s3(v), s3(kk), s3(kb), lw.reshape(2, b, t_len, d), init)
    y = _rwkv_post(yf.reshape(m, d), yb.reshape(m, d), bonus, g, ln_g[j], ln_b[j])
    x = _mm(y, w_out, wlead=(j,), tm=512, tn=1024, extras=[x, gate1],
            epilogue=_residual_epilogue, name="rwkv_out")
    return x, sn


def _rwkv_state_to_pairs(s):
    b = s.shape[0]
    n = RWKV_N
    out = jnp.zeros((b, 2, RWKV_PAIRS, 2 * n, 2 * n), F32)
    return out.at[..., :n, :n].set(s[:, :, 0::2]).at[..., n:, n:].set(s[:, :, 1::2])


def _rwkv_state_from_pairs(sp):
    n = RWKV_N
    b = sp.shape[0]
    return jnp.stack([sp[..., :n, :n], sp[..., n:, n:]], axis=3).reshape(b, 2, RWKV_HEADS, n, n)


def _router_kernel(x_ref, w_ref, b_ref, o_ref):
    e0, per = ROUTER_EXPERT_LANE0, MOE_PER_GROUP
    logits = _dot_3pass(x_ref[...], w_ref[...]) + b_ref[...]
    lane = lax.broadcasted_iota(jnp.int32, logits.shape, 1).astype(F32)
    big = float(LANES)
    rmax = lambda z: jnp.max(z, axis=-1, keepdims=True)
    rsum = lambda z: jnp.sum(z, axis=-1, keepdims=True)
    first = lambda m: jnp.min(jnp.where(m, lane, big), axis=-1, keepdims=True)
    is_g = lane < MOE_GROUPS
    gmax = rmax(jnp.where(is_g, logits, -jnp.inf))
    gsel = first(jnp.logical_and(is_g, logits >= gmax))
    gw = 1.0 / rsum(jnp.where(is_g, jnp.exp(logits - gmax), 0.0))
    in_grp = jnp.logical_and(lane >= e0 + gsel * per, lane < e0 + (gsel + 1) * per)
    emax = rmax(jnp.where(in_grp, logits, -jnp.inf))
    p = jnp.where(in_grp, jnp.exp(logits - emax), 0.0)
    p = p / rsum(p)
    p1 = rmax(jnp.where(in_grp, p, -1.0))
    i1 = first(jnp.logical_and(in_grp, p >= p1))
    rest = jnp.logical_and(in_grp, lane != i1)
    p2 = rmax(jnp.where(rest, p, -1.0))
    i2 = first(jnp.logical_and(rest, p >= p2))
    tot = p1 + p2
    comb = jnp.where(lane == i1, p1 / tot * gw, jnp.where(lane == i2, p2 / tot * gw, 0.0))
    o_ref[...] = jnp.where(lane == ROUTER_GROUP_LANE, gsel, comb)


def _router(x, w_group, b_group, w_expert, b_expert, *, tm=512):
    m, d = x.shape
    tm = min(tm, m)
    e0 = ROUTER_EXPERT_LANE0
    w = jnp.zeros((d, LANES), F32).at[:, :MOE_GROUPS].set(w_group).at[:, e0:e0 + MOE_EXPERTS].set(w_expert)
    bias = jnp.zeros((1, LANES), F32).at[0, :MOE_GROUPS].set(b_group).at[0, e0:e0 + MOE_EXPERTS].set(b_expert)
    return _call(_router_kernel, grid=(m // tm,),
                 in_specs=[pl.BlockSpec((tm, d), lambda i: (i, 0)), pl.BlockSpec((d, LANES), lambda i: (0, 0)),
                           pl.BlockSpec((1, LANES), lambda i: (0, 0))],
                 out_specs=pl.BlockSpec((tm, LANES), lambda i: (i, 0)),
                 out_shape=jax.ShapeDtypeStruct((m, LANES), F32), sem=("parallel",), name="moe_router")(x, w, bias)


def _moe_dispatch(comb):
    n = comb.shape[0]
    tb, tr = min(MOE_BLOCK_TOKENS, n), MOE_ROW_TILE
    nblk, r_blk = n // tb, tb + MOE_GROUPS * tr
    grp = comb[:, ROUTER_GROUP_LANE].astype(jnp.int32).reshape(nblk, tb)
    onehot = (grp[..., None] == jnp.arange(MOE_GROUPS, dtype=jnp.int32)).astype(jnp.int32)
    csum = jnp.cumsum(onehot, axis=1)
    padded = (csum[:, -1, :] + tr - 1) // tr * tr
    start = jnp.cumsum(padded, axis=1) - padded
    pos = jnp.sum(onehot * (start[:, None, :] + csum - onehot), axis=-1)
    tok = jnp.full((nblk, r_blk), -1, jnp.int32).at[jnp.arange(nblk)[:, None], pos].set(
        jnp.broadcast_to(jnp.arange(tb, dtype=jnp.int32), (nblk, tb)))
    meta = jnp.concatenate([start, padded // tr], axis=1).astype(jnp.int32)
    return tok.reshape(-1), meta.reshape(-1)


def _moe_sparse_kernel(tok_ref, meta_ref, h_ref, comb_ref, wg_ref, wu_ref, wd_ref, y_ref, xs, cs, acc):
    blk, e = pl.program_id(0), pl.program_id(1)
    tr, per = MOE_ROW_TILE, MOE_PER_GROUP
    r_blk = xs.shape[0]
    g, i = e // per, e % per
    base, mbase = blk * r_blk, blk * 2 * MOE_GROUPS

    @pl.when(e == 0)
    def _():
        n_rows = meta_ref[mbase + MOE_GROUPS - 1] + meta_ref[mbase + 2 * MOE_GROUPS - 1] * tr

        def body(j, carry):
            r0 = pl.multiple_of(j * 8, 8)
            for u in range(8):
                t = tok_ref[base + r0 + u]
                tt = jnp.maximum(t, 0)
                xs[pl.ds(r0 + u, 1), :] = h_ref[pl.ds(tt, 1), :]
                cs[pl.ds(r0 + u, 1), :] = jnp.where(t >= 0, comb_ref[pl.ds(tt, 1), :], 0.0)
            return carry

        lax.fori_loop(0, n_rows // 8, body, 0)

    lo, n_tiles = meta_ref[mbase + g], meta_ref[mbase + MOE_GROUPS + g]
    lane = lax.broadcasted_iota(jnp.int32, (tr, LANES), 1)

    def tile_body(t, carry):
        rows = pl.ds(pl.multiple_of(lo + t * tr, tr), tr)
        x = xs[rows, :]
        ce = jnp.sum(jnp.where(lane == e + ROUTER_EXPERT_LANE0, cs[rows, :], 0.0), axis=-1, keepdims=True)
        hid = _silu(_dot(x, wg_ref[0, 0])) * _dot(x, wu_ref[0, 0])
        contrib = _dot(hid * ce, wd_ref[0, 0])

        @pl.when(i == 0)
        def _():
            acc[rows, :] = contrib

        @pl.when(i != 0)
        def _():
            acc[rows, :] += contrib

        return carry

    lax.fori_loop(0, n_tiles, tile_body, 0)

    @pl.when(i == per - 1)
    def _():
        def body(j, carry):
            r0 = pl.multiple_of(lo + j * 8, 8)
            for u in range(8):
                t = tok_ref[base + r0 + u]

                @pl.when(t >= 0)
                def _():
                    y_ref[pl.ds(t, 1), :] = acc[pl.ds(r0 + u, 1), :]

            return carry

        lax.fori_loop(0, n_tiles * (tr // 8), body, 0)


def _moe_sparse(h, comb, w_gate, w_up, w_down, layer):
    n, d = h.shape
    f = w_gate.shape[-1]
    tb, tr = min(MOE_BLOCK_TOKENS, n), MOE_ROW_TILE
    nblk, r_blk = n // tb, tb + MOE_GROUPS * tr
    tok, meta = _moe_dispatch(comb)
    wspec = lambda shape: pl.BlockSpec((1, 1) + shape, lambda b, e, *_: (layer, e, 0, 0))
    once = pl.Buffered(1)
    grid_spec = pltpu.PrefetchScalarGridSpec(
        num_scalar_prefetch=2, grid=(nblk, MOE_EXPERTS),
        in_specs=[pl.BlockSpec((tb, d), lambda b, e, *_: (b, 0), pipeline_mode=once),
                  pl.BlockSpec((tb, LANES), lambda b, e, *_: (b, 0), pipeline_mode=once),
                  wspec((d, f)), wspec((d, f)), wspec((f, d))],
        out_specs=pl.BlockSpec((tb, d), lambda b, e, *_: (b, 0), pipeline_mode=once),
        scratch_shapes=[pltpu.VMEM((r_blk, d), F32), pltpu.VMEM((r_blk, LANES), F32), pltpu.VMEM((r_blk, d), F32)])
    return pl.pallas_call(
        _moe_sparse_kernel, grid_spec=grid_spec, out_shape=jax.ShapeDtypeStruct((n, d), F32), name="moe_sparse",
        compiler_params=pltpu.CompilerParams(dimension_semantics=("arbitrary", "arbitrary"),
                                             vmem_limit_bytes=VMEM_LIMIT_BYTES))(tok, meta, h, comb, w_gate, w_up, w_down)


def _run_trunk(x3, mods, rows, init_a, init_b, init_c, p):
    b, t_len, d = x3.shape
    m = b * t_len
    x = x3.reshape(m, d)
    new_a, new_b, new_c = [], [], []
    resid = None
    for i in range(p["ada_w"].shape[0]):
        j = i // 3
        shift1, scale1, gate1, shift2, scale2, gate2 = mods[i]
        h = _normmod(x, p["norm1_g"][i], scale1, shift1, resid=resid, out_dtype=F32 if i % 3 == 2 else BF16)
        if resid is not None:
            x, h = h
        if i % 3 == 0:
            x, st = _mlstm_mixer(h, b, rows, tuple(s[:, j] for s in init_a), p["mlstm_w_in"][j], p["mlstm_conv_w"][j],
                                 p["mlstm_w_gate"][j], p["mlstm_b_gate"][j], p["mlstm_norm_g"][j], p["mlstm_w_out"][j],
                                 x, gate1)
            new_a.append(st)
        elif i % 3 == 1:
            x, st = _gla_mixer(h, b, rows, tuple(s[:, j] for s in init_b), p["gla_w_in"][j], p["gla_conv_w"][j],
                               p["gla_w_a1"][j], p["gla_w_a2"][j], p["gla_b_a"][j], p["gla_norm_g"][j], p["gla_w_out"][j],
                               x, gate1)
            new_b.append(st)
        else:
            s0 = (jnp.zeros((b, 2, RWKV_PAIRS, LANES, LANES), F32) if init_c is None
                  else _rwkv_state_to_pairs(init_c[0][:, j]))
            x, st = _rwkv_mixer(h, b, s0, j,p["rwkv_mu"], p["rwkv_w_rkv"], p["rwkv_w0"],
                                p["rwkv_w1"], p["rwkv_w2"], p["rwkv_a0"], p["rwkv_a1"], p["rwkv_a2"], p["rwkv_g1"],
                                p["rwkv_g2"], p["rwkv_k_k"], p["rwkv_k_a"], p["rwkv_r_k"], p["rwkv_ln_g"], p["rwkv_ln_b"],
                                p["rwkv_w_out"], x, gate1)
            new_c.append((_rwkv_state_from_pairs(st),))
        hff = _normmod(x, p["norm2_g"][i], scale2, shift2)
        comb = _router(hff, p["moe_w_group"][i], p["moe_b_group"][i], p["moe_w_expert"][i], p["moe_b_expert"][i])
        resid = (_moe_sparse(hff, comb, p["moe_w_gate"], p["moe_w_up"], p["moe_w_down"], i), gate2)
    stack = lambda per_layer: tuple(jnp.stack(parts, axis=1) for parts in zip(*per_layer))
    y = _resid_rmsnorm(x, resid[0], resid[1], p["final_norm_g"]).reshape(b, t_len, d)
    return y, stack(new_a), stack(new_b), stack(new_c)


def kernel(x_prompt, x_sample, state_mlstm_C, state_mlstm_n, state_mlstm_m, state_gla_S, state_rwkv_S, c, c_ctx, ada_w, ada_b, norm1_g, norm2_g, moe_w_group, moe_b_group, moe_w_expert, moe_b_expert, moe_w_gate, moe_w_up, moe_w_down, final_norm_g, mlstm_w_in, mlstm_conv_w, mlstm_w_gate, mlstm_b_gate, mlstm_norm_g, mlstm_w_out, gla_w_in, gla_conv_w, gla_w_a1, gla_w_a2, gla_b_a, gla_norm_g, gla_w_out, rwkv_mu, rwkv_w_rkv, rwkv_w0, rwkv_w1, rwkv_w2, rwkv_a0, rwkv_a1, rwkv_a2, rwkv_g1, rwkv_g2, rwkv_k_k, rwkv_k_a, rwkv_r_k, rwkv_ln_g, rwkv_ln_b, rwkv_w_out):
    p = dict(ada_w=ada_w, norm1_g=norm1_g, norm2_g=norm2_g, moe_w_group=moe_w_group, moe_b_group=moe_b_group,
             moe_w_expert=moe_w_expert, moe_b_expert=moe_b_expert, moe_w_gate=moe_w_gate, moe_w_up=moe_w_up,
             moe_w_down=moe_w_down, final_norm_g=final_norm_g, mlstm_w_in=mlstm_w_in, mlstm_conv_w=mlstm_conv_w,
             mlstm_w_gate=mlstm_w_gate, mlstm_b_gate=mlstm_b_gate, mlstm_norm_g=mlstm_norm_g, mlstm_w_out=mlstm_w_out,
             gla_w_in=gla_w_in, gla_conv_w=gla_conv_w, gla_w_a1=gla_w_a1, gla_w_a2=gla_w_a2, gla_b_a=gla_b_a,
             gla_norm_g=gla_norm_g, gla_w_out=gla_w_out, rwkv_mu=rwkv_mu, rwkv_w_rkv=rwkv_w_rkv, rwkv_w0=rwkv_w0,
             rwkv_w1=rwkv_w1, rwkv_w2=rwkv_w2, rwkv_a0=rwkv_a0, rwkv_a1=rwkv_a1, rwkv_a2=rwkv_a2, rwkv_g1=rwkv_g1,
             rwkv_g2=rwkv_g2, rwkv_k_k=rwkv_k_k, rwkv_k_a=rwkv_k_a, rwkv_r_k=rwkv_r_k, rwkv_ln_g=rwkv_ln_g,
             rwkv_ln_b=rwkv_ln_b, rwkv_w_out=rwkv_w_out)
    for name in ("moe_w_gate", "moe_w_up", "moe_w_down", "mlstm_w_in", "mlstm_w_out", "gla_w_in", "gla_w_out",
                 "rwkv_w_rkv", "rwkv_w_out"):
        p[name] = p[name].astype(BF16)
    depth, d = ada_w.shape[0], ada_w.shape[1]
    n_dec = c.shape[0]
    cond8 = jnp.zeros((8, d), F32).at[0].set(c_ctx).at[1:1 + n_dec].set(c)
    mod = _ada(cond8, ada_w, ada_b)
    split = lambda rows: [tuple(rows[i][:, None, k * d:(k + 1) * d] for k in range(6)) for i in range(depth)]
    mods_ctx = split(mod[:, 0:1])
    mods_dec = split(mod[:, 1:1 + n_dec])

    bp = x_prompt.shape[0]
    zeros_like_ctx = lambda s: jnp.zeros((bp,) + s.shape[1:], F32)
    y_prompt, st_a, st_b, st_c = _run_trunk(
        x_prompt, mods_ctx, 1, tuple(zeros_like_ctx(s) for s in (state_mlstm_C, state_mlstm_n, state_mlstm_m)),
        (zeros_like_ctx(state_gla_S),), None, p)
    y_sample = _run_trunk(x_sample, mods_dec, x_sample.shape[1] // GRID_W,
                          (state_mlstm_C, state_mlstm_n, state_mlstm_m), (state_gla_S,), (state_rwkv_S,), p)[0]
    return (y_prompt, y_sample, st_a[0], st_a[1], st_a[2], st_b[0], st_c[0])
```

```python
import functools

import jax
import jax.numpy as jnp
from jax import lax
from jax.experimental import pallas as pl
from jax.experimental.pallas import tpu as pltpu

F32 = jnp.float32
BF16 = jnp.bfloat16

D_MODEL = 1024
GRID_W = 64
NORM_EPS = 1e-6

MLSTM_HEADS = 4
MLSTM_DK = 128
MLSTM_DV = 256
MLSTM_CHUNK = 128

GLA_HEADS = 4
GLA_DK = 128
GLA_DV = 256
GLA_TAU = 16.0
GLA_CHUNK = 32

RWKV_N = 64
RWKV_HEADS = 16
RWKV_PAIRS = RWKV_HEADS // 2
RWKV_GN_EPS = 64e-5
RWKV_CHUNK = 32

MOE_GROUPS = 4
MOE_PER_GROUP = 4
MOE_EXPERTS = 16
ROUTER_EXPERT_LANE0 = 8
ROUTER_GROUP_LANE = 127
ROUTER_RANK_LANE = 126
MOE_BLOCK_TOKENS = 2048
MOE_ROW_TILE = 256

CONV_BLOCK_BYTES = 2 * 1024 * 1024
SCAN_BATCH_ROWS = 2

LANES = 128
VMEM_LIMIT_BYTES = 56 * 1024 * 1024


def _call(kernel, *, grid, in_specs, out_specs, out_shape, scratch=(), sem, name):
    return pl.pallas_call(
        kernel, grid=grid, in_specs=in_specs, out_specs=out_specs, out_shape=out_shape,
        scratch_shapes=list(scratch), name=name,
        compiler_params=pltpu.CompilerParams(dimension_semantics=sem, vmem_limit_bytes=VMEM_LIMIT_BYTES))


def _sigmoid(x):
    return 1.0 / (1.0 + jnp.exp(-x))


def _silu(x):
    return x * _sigmoid(x)


def _logsig(x):
    return jnp.minimum(x, 0.0) - jnp.log1p(jnp.exp(-jnp.abs(x)))


def _dot(a, b):
    return jnp.dot(a.astype(BF16), b.astype(BF16), preferred_element_type=F32)


def _dot_nt(a, b):
    return lax.dot_general(a.astype(BF16), b.astype(BF16), (((1,), (1,)), ((), ())), preferred_element_type=F32)


def _dot_tn(a, b):
    return lax.dot_general(a.astype(BF16), b.astype(BF16), (((0,), (0,)), ((), ())), preferred_element_type=F32)


def _split_bf16(x, parts):
    out = []
    for _ in range(parts):
        piece = x.astype(BF16)
        out.append(piece)
        x = x - piece.astype(F32)
    return out


def _dot_exact_lhs(a, b):
    a = a.astype(BF16)
    return sum(jnp.dot(a, piece, preferred_element_type=F32) for piece in _split_bf16(b, 3))


def _dot_3pass(a, b):
    (a_hi, a_lo), (b_hi, b_lo) = _split_bf16(a, 2), _split_bf16(b, 2)
    dot = lambda p, q: jnp.dot(p, q, preferred_element_type=F32)
    return dot(a_hi, b_hi) + (dot(a_hi, b_lo) + dot(a_lo, b_hi))


def _tri(n, kind, block=None):
    r = lax.broadcasted_iota(jnp.int32, (n, n), 0)
    c = lax.broadcasted_iota(jnp.int32, (n, n), 1)
    m = {"lower": c <= r, "upper": c >= r, "all": c >= 0}[kind]
    if block is not None:
        sh = block.bit_length() - 1
        m = jnp.logical_and(m, jnp.right_shift(r, sh) == jnp.right_shift(c, sh))
    return jnp.where(m, 1.0, 0.0).astype(F32)


def _mm(x, w, *, tm, tn, xlead=(), wlead=(), extras=(), epilogue=None, out_dtype=F32, name):
    m, k = x.shape[-2:]
    n = w.shape[-1]
    tm, tn = min(tm, m), min(tn, n)

    def kern(x_ref, w_ref, *rest):
        acc = _dot(x_ref[...], w_ref[...])
        if epilogue is not None:
            acc = epilogue(acc, *[r[...] for r in rest[:-1]])
        rest[-1][...] = acc.astype(out_dtype)

    def extra_spec(arr):
        if arr.ndim == 2:
            return pl.BlockSpec((tm, tn), lambda i, j: (i, j))
        tiles_per_row = (m // arr.shape[0]) // tm
        return pl.BlockSpec((1, 1, tn), lambda i, j: (i // tiles_per_row, 0, j))

    in_specs = [
        pl.BlockSpec((None,) * len(xlead) + (tm, k), lambda i, j: tuple(xlead) + (i, 0)),
        pl.BlockSpec((None,) * len(wlead) + (k, tn), lambda i, j: tuple(wlead) + (0, j)),
    ] + [extra_spec(a) for a in extras]
    return _call(kern, grid=(m // tm, n // tn), in_specs=in_specs,
                 out_specs=pl.BlockSpec((tm, tn), lambda i, j: (i, j)),
                 out_shape=jax.ShapeDtypeStruct((m, n), out_dtype),
                 sem=("parallel", "parallel"), name=name)(x, w, *extras)


def _residual_epilogue(acc, x, gate):
    return x + gate[0] * acc


def _ada_kernel(c_ref, w_ref, b_ref, o_ref):
    o_ref[0] = _dot(_silu(c_ref[...]), w_ref[0]) + b_ref[0]


def _ada(cond8, ada_w, ada_b):
    depth, d, n = ada_w.shape
    tn = 1536
    return _call(_ada_kernel, grid=(depth, n // tn),
                 in_specs=[pl.BlockSpec((8, d), lambda l, j: (0, 0)),
                           pl.BlockSpec((1, d, tn), lambda l, j: (l, 0, j)),
                           pl.BlockSpec((1, 1, tn), lambda l, j: (l, 0, j))],
                 out_specs=pl.BlockSpec((1, 8, tn), lambda l, j: (l, 0, j)),
                 out_shape=jax.ShapeDtypeStruct((depth, 8, n), F32),
                 sem=("parallel", "parallel"), name="ada")(cond8, ada_w, ada_b.reshape(depth, 1, n))


def _normmod_kernel(x_ref, g_ref, sc_ref, sh_ref, o_ref):
    x = x_ref[...]
    y = x * lax.rsqrt(jnp.mean(x * x, axis=-1, keepdims=True) + NORM_EPS)
    o_ref[...] = ((y * g_ref[...]) * (1.0 + sc_ref[0]) + sh_ref[0]).astype(o_ref.dtype)


def _resid_normmod_kernel(x_ref, y_ref, gate_ref, g_ref, sc_ref, sh_ref, xo_ref, o_ref):
    x = x_ref[...] + gate_ref[0] * y_ref[...]
    xo_ref[...] = x
    y = x * lax.rsqrt(jnp.mean(x * x, axis=-1, keepdims=True) + NORM_EPS)
    o_ref[...] = ((y * g_ref[...]) * (1.0 + sc_ref[0]) + sh_ref[0]).astype(o_ref.dtype)


def _normmod(x, g, scale, shift, *, resid=None, out_dtype=F32, tm=512):
    m, d = x.shape
    tm = min(tm, m)
    tiles_per_row = (m // scale.shape[0]) // tm
    mod = pl.BlockSpec((1, 1, d), lambda i: (i // tiles_per_row, 0, 0))
    tok = pl.BlockSpec((tm, d), lambda i: (i, 0))
    par = pl.BlockSpec((1, d), lambda i: (0, 0))
    if resid is None:
        return _call(_normmod_kernel, grid=(m // tm,), in_specs=[tok, par, mod, mod], out_specs=tok,
                     out_shape=jax.ShapeDtypeStruct((m, d), out_dtype), sem=("parallel",), name="normmod")(
                         x, g.reshape(1, d), scale, shift)
    y, gate = resid
    return _call(_resid_normmod_kernel, grid=(m // tm,), in_specs=[tok, tok, mod, par, mod, mod], out_specs=[tok, tok],
                 out_shape=[jax.ShapeDtypeStruct((m, d), F32), jax.ShapeDtypeStruct((m, d), out_dtype)],
                 sem=("parallel",), name="resid_normmod")(x, y, gate, g.reshape(1, d), scale, shift)


def _resid_rmsnorm_kernel(x_ref, y_ref, gate_ref, g_ref, o_ref):
    x = x_ref[...] + gate_ref[0] * y_ref[...]
    o_ref[...] = x * lax.rsqrt(jnp.mean(x * x, axis=-1, keepdims=True) + NORM_EPS) * g_ref[...]


def _resid_rmsnorm(x, y, gate, g, *, tm=512):
    m, d = x.shape
    tm = min(tm, m)
    tiles_per_row = (m // gate.shape[0]) // tm
    tok = pl.BlockSpec((tm, d), lambda i: (i, 0))
    return _call(_resid_rmsnorm_kernel, grid=(m // tm,),
                 in_specs=[tok, tok, pl.BlockSpec((1, 1, d), lambda i: (i // tiles_per_row, 0, 0)),
                           pl.BlockSpec((1, d), lambda i: (0, 0))],
                 out_specs=tok, out_shape=jax.ShapeDtypeStruct((m, d), F32), sem=("parallel",), name="final_norm")(
                     x, y, gate, g.reshape(1, d))


def _conv_kernel(x_ref, w_ref, o_ref, *, rows, width):
    x = x_ref[0]
    t_len = x.shape[0]
    t = lax.broadcasted_iota(jnp.int32, x.shape, 0)
    col = jnp.bitwise_and(t, width - 1)
    row = jnp.right_shift(t, width.bit_length() - 1)
    acc = jnp.zeros_like(x)
    for dr in (-1, 0, 1):
        if rows == 1 and dr != 0:
            continue
        for dc in (-1, 0, 1):
            off = dr * width + dc
            xs = x if off == 0 else pltpu.roll(x, (-off) % t_len, axis=0)
            ok = jnp.where(jnp.logical_and(col + dc >= 0, col + dc < width), 1.0, 0.0)
            if dr != 0:
                ok = ok * jnp.where(jnp.logical_and(row + dr >= 0, row + dr < rows), 1.0, 0.0)
            tap = (dr + 1) * 3 + (dc + 1)
            acc = acc + (xs * ok) * w_ref[tap:tap + 1, :]
    o_ref[0] = _silu(acc)


def _conv_silu(big, conv_w, *, rows, width, channels):
    b, t_len, _ = big.shape
    assert width & (width - 1) == 0 and rows * width == t_len
    tc = max(2 * LANES, min(channels, CONV_BLOCK_BYTES // (4 * t_len)))
    return _call(functools.partial(_conv_kernel, rows=rows, width=width), grid=(b, channels // tc),
                 in_specs=[pl.BlockSpec((1, t_len, tc), lambda i, j: (i, 0, j)),
                           pl.BlockSpec((9, tc), lambda i, j: (0, j))],
                 out_specs=pl.BlockSpec((1, t_len, tc), lambda i, j: (i, 0, j)),
                 out_shape=jax.ShapeDtypeStruct((b, t_len, channels), F32),
                 sem=("parallel", "parallel"), name="conv_silu")(big, conv_w.reshape(9, channels))


def _mlstm_gate_kernel(h_ref, wi_ref, wf_ref, bi_ref, bf_ref, gb_ref, gw_ref, ga_ref):
    nh, L = MLSTM_HEADS, MLSTM_CHUNK
    h = h_ref[0]
    tg = h.shape[0]
    ig = _dot(h, wi_ref[...]) + bi_ref[...]
    lf = _logsig(_dot(h, wf_ref[...]) + bf_ref[...])
    lane = lax.broadcasted_iota(jnp.int32, ig.shape, 1)
    b = jnp.where(lane < nh, _dot_exact_lhs(_tri(tg, "lower", L), lf), _dot_exact_lhs(_tri(tg, "upper", L), lf))
    b_last = _dot_exact_lhs(_tri(tg, "all", L), lf)
    gb_ref[0] = b
    gw_ref[0] = b_last - b + ig
    ga_ref[0] = ig - b


def _mlstm_gates(h3, w_gate, b_gate, *, tg=256):
    b, t_len, d = h3.shape
    nh = MLSTM_HEADS
    tg = min(tg, t_len)
    wi = _pad_cols(jnp.concatenate([w_gate[0, :, :nh], w_gate[1, :, :nh]], axis=1), LANES)
    wf = _pad_cols(jnp.concatenate([w_gate[0, :, nh:], w_gate[1, :, nh:]], axis=1), LANES)
    bi = _pad_cols(jnp.concatenate([b_gate[0, :nh], b_gate[1, :nh]]).reshape(1, 2 * nh), LANES)
    bf = _pad_cols(jnp.concatenate([b_gate[0, nh:], b_gate[1, nh:]]).reshape(1, 2 * nh), LANES)
    full = lambda shape: pl.BlockSpec(shape, lambda i, c: (0,) * len(shape))
    tok = pl.BlockSpec((1, tg, LANES), lambda i, c: (i, c, 0))
    return _call(_mlstm_gate_kernel, grid=(b, t_len // tg),
                 in_specs=[pl.BlockSpec((1, tg, d), lambda i, c: (i, c, 0)),
                           full((d, LANES)), full((d, LANES)), full((1, LANES)), full((1, LANES))],
                 out_specs=[tok, tok, tok], out_shape=[jax.ShapeDtypeStruct((b, t_len, LANES), F32)] * 3,
                 sem=("parallel", "parallel"), name="mlstm_gates")(h3, wi, wf, bi, bf)


def _mlstm_scan_kernel(qkf_ref, qkb_ref, vf_ref, vb_ref, gbf_ref, gbb_ref, gwf_ref, gwb_ref, gaf_ref, gab_ref,
                       c0_ref, n0_ref, m0_ref, yf_ref, yb_ref, cn_ref, nn_ref, mn_ref, c_s, n_s, m_s):
    nh, L, dk, dv = MLSTM_HEADS, MLSTM_CHUNK, MLSTM_DK, MLSTM_DV
    c = pl.program_id(1)

    @pl.when(c == 0)
    def _():
        c_s[...] = c0_ref[...]
        n_s[...] = n0_ref[...]
        m_s[...] = m0_ref[...]

    ti = lax.broadcasted_iota(jnp.int32, (L, L), 0)
    si = lax.broadcasted_iota(jnp.int32, (L, L), 1)
    lane = lax.broadcasted_iota(jnp.int32, (L, LANES), 1)
    ones = jnp.ones((L, LANES), BF16)
    refs = ((qkf_ref, vf_ref, gbf_ref, gwf_ref, gaf_ref, yf_ref), (qkb_ref, vb_ref, gbb_ref, gwb_ref, gab_ref, yb_ref))
    chains = [(i, d, h) for i in range(c_s.shape[0]) for d in range(2) for h in range(nh)]
    st = {}
    for i, d, h in chains:
        qk_ref, v_ref, gb_ref, gw_ref, ga_ref, _ = refs[d]
        g = d * nh + h
        q = qk_ref[i, :, h * dk:(h + 1) * dk] * (dk ** -0.5)
        k = qk_ref[i, :, (nh + h) * dk:(nh + h + 1) * dk]
        v = v_ref[i, :, h * dv:(h + 1) * dv]
        a_row = sum(lax.dot_general(ones, piece, (((1,), (1,)), ((), ())), preferred_element_type=F32)
                    for piece in _split_bf16(jnp.where(lane == g, ga_ref[i], 0.0), 3))
        b_col = gb_ref[i, :, g:g + 1]
        wl_col = gw_ref[i, :, g:g + 1]
        st[i, d, h] = dict(q=q, k=k, v=v, a_row=a_row, b_col=b_col, wl_col=wl_col, qk=_dot_nt(q, k))
    for i, d, h in chains:
        z = st[i, d, h]
        m_st = m_s[i, d, h]
        mask = (si <= ti) if d == 0 else (si >= ti)
        log_d = jnp.where(mask, z["b_col"] + z["a_row"], -jnp.inf)
        m_t = jnp.maximum(z["b_col"] + m_st, jnp.max(log_d, axis=-1, keepdims=True))
        z.update(m_t=m_t, s=z["qk"] * jnp.exp(log_d - m_t), w_inter=jnp.exp(z["b_col"] + m_st - m_t))
    for i, d, h in chains:
        z = st[i, d, h]
        q, s, w_inter = z["q"], z["s"], z["w_inter"]
        num = _dot(s, z["v"]) + w_inter * _dot(q, c_s[i, d, h])
        den = jnp.sum(s, axis=-1, keepdims=True) + w_inter * jnp.sum(q * n_s[i, d, h], axis=-1, keepdims=True)
        refs[d][5][i, :, h * dv:(h + 1) * dv] = num / jnp.maximum(jnp.abs(den), jnp.exp(-z["m_t"]))
    for i, d, h in chains:
        z = st[i, d, h]
        last = L - 1 if d == 0 else 0
        m_st = m_s[i, d, h]
        b_last = z["b_col"][last:last + 1, :]
        m_new = jnp.maximum(b_last + m_st, jnp.max(z["wl_col"], axis=0, keepdims=True))
        ks = jnp.exp(z["wl_col"] - m_new) * z["k"]
        decay = jnp.exp(b_last + m_st - m_new)
        c_s[i, d, h] = decay * c_s[i, d, h] + _dot_tn(ks, z["v"])
        n_s[i, d, h] = decay * n_s[i, d, h] + jnp.sum(ks, axis=0, keepdims=True)
        m_s[i, d, h] = m_new

    @pl.when(c == pl.num_programs(1) - 1)
    def _():
        cn_ref[...] = c_s[...]
        nn_ref[...] = n_s[...]
        mn_ref[...] = m_s[...]


def _mlstm_scan(qk, big, gb, gw, ga, c0, n0, m0, *, nb=SCAN_BATCH_ROWS):
    b, t_len, _ = qk.shape
    nh, L, dk, dv = MLSTM_HEADS, MLSTM_CHUNK, MLSTM_DK, MLSTM_DV
    nc = t_len // L
    fwd = lambda i, c: (i, c, 0)
    bwd = lambda i, c: (i, nc - 1 - c, 0)
    state = lambda shape: pl.BlockSpec((nb,) + shape, lambda i, c: (i,) + (0,) * len(shape))
    st_shapes = [(2, nh, dk, dv), (2, nh, 1, dk), (2, nh, 1, 1)]
    return _call(
        _mlstm_scan_kernel, grid=(b // nb, nc),
        in_specs=[pl.BlockSpec((nb, L, 2 * nh * dk), fwd), pl.BlockSpec((nb, L, 2 * nh * dk), bwd),
                  pl.BlockSpec((nb, L, nh * dv), lambda i, c: (i, c, 1)),
                  pl.BlockSpec((nb, L, nh * dv), lambda i, c: (i, nc - 1 - c, 1)),
                  pl.BlockSpec((nb, L, LANES), fwd), pl.BlockSpec((nb, L, LANES), bwd),
                  pl.BlockSpec((nb, L, LANES), fwd), pl.BlockSpec((nb, L, LANES), bwd),
                  pl.BlockSpec((nb, L, LANES), fwd), pl.BlockSpec((nb, L, LANES), bwd)] + [state(s) for s in st_shapes],
        out_specs=[pl.BlockSpec((nb, L, nh * dv), fwd), pl.BlockSpec((nb, L, nh * dv), bwd)] + [state(s) for s in st_shapes],
        out_shape=[jax.ShapeDtypeStruct((b, t_len, nh * dv), F32)] * 2 + [jax.ShapeDtypeStruct((b,) + s, F32) for s in st_shapes],
        scratch=[pltpu.VMEM((nb,) + s, F32) for s in st_shapes],
        sem=("parallel", "arbitrary"), name="mlstm_scan")(qk, qk, big, big, gb, gb, gw, gw, ga, ga, c0, n0, m0)


def _mlstm_post_kernel(yf_ref, yb_ref, o_ref, g_ref, out_ref):
    dv = MLSTM_DV
    for h in range(MLSTM_HEADS):
        sl = slice(h * dv, (h + 1) * dv)
        y = yf_ref[:, sl] + yb_ref[:, sl]
        y = y - jnp.mean(y, axis=-1, keepdims=True)
        y = y * lax.rsqrt(jnp.mean(y * y, axis=-1, keepdims=True) + NORM_EPS)
        out_ref[:, sl] = (_sigmoid(o_ref[:, sl]) * (y * g_ref[:, sl])).astype(BF16)


def _mlstm_post(yf, yb, big, norm_g, *, tm=512):
    m, d = yf.shape
    tm = min(tm, m)
    row = lambda i: (i, 0)
    return _call(_mlstm_post_kernel, grid=(m // tm,),
                 in_specs=[pl.BlockSpec((tm, d), row), pl.BlockSpec((tm, d), row),
                           pl.BlockSpec((tm, d), lambda i: (i, 2)), pl.BlockSpec((1, d), lambda i: (0, 0))],
                 out_specs=pl.BlockSpec((tm, d), row), out_shape=jax.ShapeDtypeStruct((m, d), BF16),
                 sem=("parallel",), name="mlstm_post")(yf, yb, big, norm_g.reshape(1, d))


def _mlstm_mixer(h, b, rows, init, w_in, conv_w, w_gate, b_gate, norm_g, w_out, x, gate1):
    m, d = h.shape
    t_len = m // b
    nh, dk = MLSTM_HEADS, MLSTM_DK
    big = _mm(h, w_in, tm=1024, tn=1024, name="mlstm_in")
    big3 = big.reshape(b, t_len, 3 * d)
    qk = _conv_silu(big3, conv_w, rows=rows, width=t_len // rows, channels=2 * nh * dk)
    gb, gw, ga = _mlstm_gates(h.reshape(b, t_len, d), w_gate, b_gate)
    c0, n0, m0 = init
    yf, yb, cn, nn, mn = _mlstm_scan(qk, big3, gb, gw, ga, c0, n0.reshape(b, 2, nh, 1, dk), m0.reshape(b, 2, nh, 1, 1))
    y = _mlstm_post(yf.reshape(m, d), yb.reshape(m, d), big, norm_g)
    x = _mm(y, w_out, tm=512, tn=1024, extras=[x, gate1],
            epilogue=_residual_epilogue, name="mlstm_out")
    return x, (cn, nn.reshape(b, 2, nh, dk), mn.reshape(b, 2, nh))


def _gla_gate_kernel(h_ref, w1_ref, w2_ref, b_ref, gf_ref, gb_ref):
    half = GLA_HEADS * GLA_DK
    h = h_ref[0]
    z = _dot(_dot(h, w1_ref[...]), w2_ref[...]) + b_ref[...]
    lg = _logsig(z) * (1.0 / GLA_TAU)
    tg = h.shape[0]
    gf_ref[0] = _dot_exact_lhs(_tri(tg, "lower", GLA_CHUNK), lg[:, :half])
    gb_ref[0] = _dot_exact_lhs(_tri(tg, "upper", GLA_CHUNK), lg[:, half:])


def _gla_gates(h3, w_a1, w_a2, b_a, *, tg=256):
    b, t_len, d = h3.shape
    rank, half = w_a1.shape[-1], w_a2.shape[-1]
    w1 = jnp.zeros((d, LANES), F32).at[:, :rank].set(w_a1[0]).at[:, rank:2 * rank].set(w_a1[1])
    w2 = jnp.zeros((LANES, 2 * half), F32).at[:rank, :half].set(w_a2[0]).at[rank:2 * rank, half:].set(w_a2[1])
    bias = jnp.concatenate([b_a[0], b_a[1]]).reshape(1, 2 * half)
    tg = min(tg, t_len)
    full = lambda shape: pl.BlockSpec(shape, lambda i, c: (0,) * len(shape))
    tok = lambda n: pl.BlockSpec((1, tg, n), lambda i, c: (i, c, 0))
    return _call(_gla_gate_kernel, grid=(b, t_len // tg),
                 in_specs=[tok(d), full((d, LANES)), full((LANES, 2 * half)), full((1, 2 * half))],
                 out_specs=[tok(half), tok(half)],
                 out_shape=[jax.ShapeDtypeStruct((b, t_len, half), F32)] * 2,
                 sem=("parallel", "parallel"), name="gla_gates")(h3, w1, w2, bias)


def _gla_scan_kernel(xf_ref, xb_ref, gf_ref, gb_ref, s0_ref, yf_ref, yb_ref, sn_ref, s_s):
    nh, L, dk, dv = GLA_HEADS, GLA_CHUNK, GLA_DK, GLA_DV
    c = pl.program_id(1)

    @pl.when(c == 0)
    def _():
        s_s[...] = s0_ref[0]

    tcol = lax.broadcasted_iota(jnp.int32, (L, 1), 0)
    eye = jnp.where(lax.broadcasted_iota(jnp.int32, (dk, dk), 0) == lax.broadcasted_iota(jnp.int32, (dk, dk), 1), 1.0, 0.0)
    refs = ((xf_ref, gf_ref, yf_ref), (xb_ref, gb_ref, yb_ref))
    chains = [(d, h) for d in range(2) for h in range(nh)]
    st = {}
    for d, h in chains:
        x_ref, g_ref, _ = refs[d]
        q = x_ref[0, :, h * dk:(h + 1) * dk] * (dk ** -0.5)
        k = x_ref[0, :, (nh + h) * dk:(nh + h + 1) * dk]
        v = x_ref[0, :, 2 * nh * dk + h * dv:2 * nh * dk + (h + 1) * dv]
        g = g_ref[0, :, h * dk:(h + 1) * dk]
        st[d, h] = dict(q=q, k=k, v=v, g=g, o=_dot(q * jnp.exp(g), s_s[d, h]))
    sub = 8
    for s in range(L):
        for d, h in chains:
            z = st[d, h]
            q, k, v, g, o = z["q"], z["k"], z["v"], z["g"], z["o"]
            lo, hi = ((s // sub) * sub, L) if d == 0 else (0, (s // sub + 1) * sub)
            dec = jnp.exp(jnp.minimum(g[lo:hi] - g[s:s + 1, :], 0.0))
            col = jnp.sum(q[lo:hi] * k[s:s + 1, :] * dec, axis=-1, keepdims=True)
            col = jnp.where((tcol[lo:hi] >= s) if d == 0 else (tcol[lo:hi] <= s), col, 0.0)
            parts = [o[:lo]] * (lo > 0) + [o[lo:hi] + col * v[s:s + 1, :]] + [o[hi:]] * (hi < L)
            z["o"] = parts[0] if len(parts) == 1 else jnp.concatenate(parts, axis=0)
    for d, h in chains:
        z = st[d, h]
        k, v, g = z["k"], z["v"], z["g"]
        last = L - 1 if d == 0 else 0
        refs[d][2][0, :, h * dv:(h + 1) * dv] = z["o"]
        g_last = g[last:last + 1, :]
        decay_col = jnp.sum(eye * jnp.exp(g_last), axis=-1, keepdims=True)
        s_s[d, h] = decay_col * s_s[d, h] + _dot_tn(k * jnp.exp(g_last - g), v)

    @pl.when(c == pl.num_programs(1) - 1)
    def _():
        sn_ref[0] = s_s[...]


def _gla_scan(qkv, gf, gb, s0):
    b, t_len, width = qkv.shape
    nh, L, dk, dv = GLA_HEADS, GLA_CHUNK, GLA_DK, GLA_DV
    nc = t_len // L
    fwd = lambda i, c: (i, c, 0)
    bwd = lambda i, c: (i, nc - 1 - c, 0)
    st = (2, nh, dk, dv)
    state = pl.BlockSpec((1,) + st, lambda i, c: (i, 0, 0, 0, 0))
    return _call(
        _gla_scan_kernel, grid=(b, nc),
        in_specs=[pl.BlockSpec((1, L, width), fwd), pl.BlockSpec((1, L, width), bwd),
                  pl.BlockSpec((1, L, nh * dk), fwd), pl.BlockSpec((1, L, nh * dk), bwd), state],
        out_specs=[pl.BlockSpec((1, L, nh * dv), fwd), pl.BlockSpec((1, L, nh * dv), bwd), state],
        out_shape=[jax.ShapeDtypeStruct((b, t_len, nh * dv), F32)] * 2 + [jax.ShapeDtypeStruct((b,) + st, F32)],
        scratch=[pltpu.VMEM(st, F32)], sem=("parallel", "arbitrary"), name="gla_scan")(qkv, qkv, gf, gb, s0)


def _gla_post_kernel(yf_ref, yb_ref, r_ref, g_ref, out_ref):
    dv = GLA_DV
    for h in range(GLA_HEADS):
        sl = slice(h * dv, (h + 1) * dv)
        y = yf_ref[:, sl] + yb_ref[:, sl]
        y = y * lax.rsqrt(jnp.mean(y * y, axis=-1, keepdims=True) + NORM_EPS)
        out_ref[:, sl] = ((y * g_ref[:, sl]) * _silu(r_ref[:, sl])).astype(BF16)


def _gla_post(yf, yb, big, norm_g, *, tm=512):
    m, d = yf.shape
    tm = min(tm, m)
    row = lambda i: (i, 0)
    return _call(_gla_post_kernel, grid=(m // tm,),
                 in_specs=[pl.BlockSpec((tm, d), row), pl.BlockSpec((tm, d), row),
                           pl.BlockSpec((tm, d), lambda i: (i, 2)), pl.BlockSpec((1, d), lambda i: (0, 0))],
                 out_specs=pl.BlockSpec((tm, d), row), out_shape=jax.ShapeDtypeStruct((m, d), BF16),
                 sem=("parallel",), name="gla_post")(yf, yb, big, norm_g.reshape(1, d))


def _gla_mixer(h, b, rows, init, w_in, conv_w, w_a1, w_a2, b_a, norm_g, w_out, x, gate1):
    m, d = h.shape
    t_len = m // b
    big = _mm(h, w_in, tm=1024, tn=1024, name="gla_in")
    big3 = big.reshape(b, t_len, 3 * d)
    qkv = _conv_silu(big3, conv_w, rows=rows, width=t_len // rows, channels=2 * d)
    gf, gb = _gla_gates(h.reshape(b, t_len, d), w_a1, w_a2, b_a)
    yf, yb, sn = _gla_scan(qkv, gf, gb, init[0])
    y = _gla_post(yf.reshape(m, d), yb.reshape(m, d), big, norm_g)
    x = _mm(y, w_out, tm=512, tn=1024, extras=[x, gate1],
            epilogue=_residual_epilogue, name="gla_out")
    return x, (sn,)


def _rwkv_mix_kernel(h_ref, hp_ref, hn_ref, mu_ref, o_ref):
    i = pl.program_id(1)
    h = h_ref[0]
    tm = h.shape[0]
    prev_row = jnp.where(i > 0, hp_ref[0, 7:8, :], 0.0)
    next_row = jnp.where(i < pl.num_programs(1) - 1, hn_ref[0, 0:1, :], 0.0)
    row = lax.broadcasted_iota(jnp.int32, (tm, 1), 0)
    x_prev = jnp.where(row == 0, prev_row, pltpu.roll(h, 1, axis=0))
    x_next = jnp.where(row == tm - 1, next_row, pltpu.roll(h, tm - 1, axis=0))
    xx = 0.5 * (x_prev + x_next) - h
    for j in range(6):
        o_ref[j, 0] = (h + xx * mu_ref[j:j + 1, :]).astype(BF16)


def _rwkv_mix(h3, mu, *, tm=256):
    b, t_len, d = h3.shape
    tm = min(tm, t_len)
    nt, n8 = t_len // tm, t_len // 8
    return _call(_rwkv_mix_kernel, grid=(b, nt),
                 in_specs=[pl.BlockSpec((1, tm, d), lambda i, j: (i, j, 0)),
                           pl.BlockSpec((1, 8, d), lambda i, j: (i, jnp.maximum(j * (tm // 8) - 1, 0), 0)),
                           pl.BlockSpec((1, 8, d), lambda i, j: (i, jnp.minimum((j + 1) * (tm // 8), n8 - 1), 0)),
                           pl.BlockSpec((6, d), lambda i, j: (0, 0))],
                 out_specs=pl.BlockSpec((6, 1, tm, d), lambda i, j: (0, i, j, 0)),
                 out_shape=jax.ShapeDtypeStruct((6, b, t_len, d), BF16),
                 sem=("parallel", "parallel"), name="rwkv_mix")(h3, h3, h3, mu)


def _segsum(x):
    lo = lax.broadcasted_iota(jnp.int32, x.shape, 1) < RWKV_N
    s0 = jnp.sum(jnp.where(lo, x, 0.0), axis=-1, keepdims=True)
    s1 = jnp.sum(jnp.where(lo, 0.0, x), axis=-1, keepdims=True)
    return jnp.where(lo, s0, s1)


def _rwkv_prep_kernel(r_ref, k_ref, v_ref, lora_ref, ap_ref, w0_ref, a0_ref, kk_w_ref, ka_w_ref, rk_w_ref,
                      kk_ref, kb_ref, k2_ref, bonus_ref, lw_ref):
    d = D_MODEL
    for j in range(d // LANES):
        sl = slice(j * LANES, (j + 1) * LANES)
        r, k, v = r_ref[:, sl], k_ref[:, sl], v_ref[:, sl]
        a = _sigmoid(a0_ref[:, sl] + ap_ref[:, sl])
        kk = k * kk_w_ref[:, sl]
        kk = kk / jnp.maximum(jnp.sqrt(_segsum(kk * kk)), 1e-12)
        k2 = k * (1.0 + (a - 1.0) * ka_w_ref[:, sl])
        kk_ref[:, sl] = kk
        kb_ref[:, sl] = kk * a
        k2_ref[:, sl] = k2
        bonus_ref[:, sl] = _segsum(r * k2 * rk_w_ref[:, sl]) * v
        for z in range(2):
            lw_ref[z, :, sl] = -jnp.exp(_logsig(w0_ref[z:z + 1, sl] + lora_ref[:, z * d + j * LANES:z * d + (j + 1) * LANES]) - 0.5)


def _rwkv_prep(r, k, v, lora, a_pre, w0, a0, k_k, k_a, r_k, *, tm=256):
    m, d = r.shape
    tm = min(tm, m)
    row = lambda i: (i, 0)
    tok = pl.BlockSpec((tm, d), row)
    par = pl.BlockSpec((1, d), lambda i: (0, 0))
    return _call(_rwkv_prep_kernel, grid=(m // tm,),
                 in_specs=[tok, tok, tok, pl.BlockSpec((tm, 2 * d), row), tok,
                           pl.BlockSpec((2, d), lambda i: (0, 0)), par, par, par, par],
                 out_specs=[tok, tok, tok, tok, pl.BlockSpec((2, tm, d), lambda i: (0, i, 0))],
                 out_shape=[jax.ShapeDtypeStruct((m, d), F32)] * 4 + [jax.ShapeDtypeStruct((2, m, d), F32)],
                 sem=("parallel",), name="rwkv_prep")(
                     r, k, v, lora, a_pre, w0, a0.reshape(1, d), k_k.reshape(1, d), k_a.reshape(1, d), r_k.reshape(1, d))


def _rwkv_scan_kernel(rf, kf, vf, kkf, kbf, lwf, rb, kb_, vb, kkb, kbb, lwb, s0_ref, yf_ref, yb_ref, sn_ref, s_s):
    L, pairs = RWKV_CHUNK, RWKV_PAIRS
    c = pl.program_id(1)

    @pl.when(c == 0)
    def _():
        s_s[...] = s0_ref[...]

    row = lax.broadcasted_iota(jnp.int32, (L, LANES), 0)
    lane = lax.broadcasted_iota(jnp.int32, (L, LANES), 1)
    lane_tok = jnp.bitwise_and(lane, L - 1)
    lane_head0 = lane - lane_tok
    lo = lane < RWKV_N
    sq_r = lax.broadcasted_iota(jnp.int32, (LANES, LANES), 0) < RWKV_N
    sq_c = lax.broadcasted_iota(jnp.int32, (LANES, LANES), 1) < RWKV_N
    same_head = sq_r == sq_c
    zeros64 = jnp.zeros((2 * L, LANES), F32)
    eye_cat = jnp.where(lane_tok == row, 1.0, 0.0)

    def head_rows(a):
        return jnp.concatenate([jnp.where(lo, a, 0.0), jnp.where(lo, 0.0, a)], axis=0)

    def group_rows(a0, a1):
        return jnp.concatenate([jnp.concatenate([head_rows(a0), zeros64], axis=1),
                                jnp.concatenate([zeros64, head_rows(a1)], axis=1)], axis=0)

    refs = ((rf, kf, vf, kkf, kbf, lwf, yf_ref), (rb, kb_, vb, kkb, kbb, lwb, yb_ref))
    groups = [(i, d, grp) for i in range(s_s.shape[0]) for d in range(2) for grp in range(pairs // 2)]
    st = {}
    for i, d, grp in groups:
        r_ref, k_ref, v_ref, kk_ref, kb_ref, lw_ref, _ = refs[d]
        tri = _tri(L, "lower" if d == 0 else "upper")
        last = L - 1 if d == 0 else 0
        rt, kt, kd, bd, v, e_last, m_b, m_k = [], [], [], [], [], [], 0.0, 0.0
        for q in range(2):
            sl = slice((2 * grp + q) * LANES, (2 * grp + q + 1) * LANES)
            r, k, kk, kb, lw = r_ref[i, :, sl], k_ref[i, :, sl], kk_ref[i, :, sl], kb_ref[i, :, sl], lw_ref[0, i, :, sl]
            cum = _dot_exact_lhs(tri, lw)
            c_last = cum[last:last + 1, :]
            e_neg, e_end = jnp.exp(-cum), jnp.exp(c_last - cum)
            rt.append(r * jnp.exp(cum))
            kt.append(kk * jnp.exp(cum - lw))
            kd.append(k * e_end)
            bd.append(kb * e_end)
            v.append(v_ref[i, :, sl])
            e_last.append(jnp.exp(c_last))
            lhs = jnp.concatenate([kt[q], rt[q]], axis=0)
            pad = lambda a: jnp.concatenate([head_rows(a), zeros64] if q == 0 else [zeros64, head_rows(a)], axis=0)
            m_b = m_b + _dot_nt(lhs, pad(kb * e_neg))
            m_k = m_k + _dot_nt(lhs, pad(k * e_neg))
        st[i, d, grp] = dict(rt=rt, kt=kt, kd=kd, bd=bd, v=v, e_last=e_last, m_b=m_b, m_k=m_k)
    for i, d, grp in groups:
        z = st[i, d, grp]
        strict = (lane_tok < row) if d == 0 else (lane_tok > row)
        incl = (lane_tok <= row) if d == 0 else (lane_tok >= row)
        m_b, m_k = z["m_b"], z["m_k"]
        z["n_kb"] = jnp.where(strict, m_b[:L], 0.0)
        z["m_rb"] = jnp.where(incl, m_b[L:], 0.0)
        m_kk_rk = jnp.concatenate([jnp.where(strict, m_k[:L], 0.0), jnp.where(incl, m_k[L:], 0.0)], axis=0)
        z["kv"] = _dot(m_kk_rk, group_rows(z["v"][0], z["v"][1]))
        z["x"] = eye_cat
    for step in range(L - 1):
        for i, d, grp in groups:
            z = st[i, d, grp]
            u = step if d == 0 else L - 1 - step
            coef = jnp.take_along_axis(z["n_kb"], lane_head0 + u, axis=1)
            z["x"] = z["x"] - coef * z["x"][u:u + 1, :]
    for i, d, grp in groups:
        z = st[i, d, grp]
        x, kt, kv = z["x"], z["kt"], z["kv"]
        x_hi = x.astype(BF16)
        x_lo = x - x_hi.astype(F32)
        rhs = jnp.concatenate([group_rows(kt[0], kt[1]), group_rows(kv[:L, :LANES], kv[:L, LANES:])], axis=1)
        z["w"] = _dot(x_hi, rhs) + _dot(x_lo, rhs)
    for i, d, grp in groups:
        z = st[i, d, grp]
        w = z["w"]
        z["sa"] = [_dot_nt(w[:, q * LANES:(q + 1) * LANES], s_s[i, d, 2 * grp + q]) + w[:, (2 + q) * LANES:(3 + q) * LANES]
                   for q in range(2)]
    for i, d, grp in groups:
        z = st[i, d, grp]
        rb_sa = _dot(z["m_rb"], group_rows(z["sa"][0], z["sa"][1]))
        for q in range(2):
            p = 2 * grp + q
            cols = slice(q * LANES, (q + 1) * LANES)
            refs[d][6][i, :, p * LANES:(p + 1) * LANES] = _dot_nt(z["rt"][q], s_s[i, d, p]) + z["kv"][L:, cols] - rb_sa[:, cols]
    for i, d, grp in groups:
        z = st[i, d, grp]
        for q in range(2):
            p = 2 * grp + q
            s_new = s_s[i, d, p] * z["e_last"][q] + _dot_tn(z["v"][q], z["kd"][q]) - _dot_tn(z["sa"][q], z["bd"][q])
            s_s[i, d, p] = jnp.where(same_head, s_new, 0.0)

    @pl.when(c == pl.num_programs(1) - 1)
    def _():
        sn_ref[...] = s_s[...]


def _rwkv_scan(r, k2, v, kk, kb, lw, s0, *, nb=SCAN_BATCH_ROWS):
    b, t_len, d = r.shape
    L, pairs = RWKV_CHUNK, RWKV_PAIRS
    nc = t_len // L
    fwd = pl.BlockSpec((nb, L, d), lambda i, c: (i, c, 0))
    bwd = pl.BlockSpec((nb, L, d), lambda i, c: (i, nc - 1 - c, 0))
    st = (2, pairs, LANES, LANES)
    state = pl.BlockSpec((nb,) + st, lambda i, c: (i, 0, 0, 0, 0))
    return _call(
        _rwkv_scan_kernel, grid=(b // nb, nc),
        in_specs=[fwd] * 5 + [pl.BlockSpec((1, nb, L, d), lambda i, c: (0, i, c, 0))]
        + [bwd] * 5 + [pl.BlockSpec((1, nb, L, d), lambda i, c: (1, i, nc - 1 - c, 0)), state],
        out_specs=[fwd, bwd, state],
        out_shape=[jax.ShapeDtypeStruct((b, t_len, d), F32)] * 2 + [jax.ShapeDtypeStruct((b,) + st, F32)],
        scratch=[pltpu.VMEM((nb,) + st, F32)], sem=("parallel", "arbitrary"), name="rwkv_scan")(
            r, k2, v, kk, kb, lw, r, k2, v, kk, kb, lw, s0)


def _rwkv_post_kernel(yf_ref, yb_ref, bonus_ref, g_ref, lng_ref, lnb_ref, out_ref):
    for j in range(D_MODEL // LANES):
        sl = slice(j * LANES, (j + 1) * LANES)
        y = yf_ref[:, sl] + yb_ref[:, sl]
        y = y - _segsum(y) * (1.0 / RWKV_N)
        y = y * lax.rsqrt(_segsum(y * y) * (1.0 / RWKV_N) + RWKV_GN_EPS)
        out_ref[:, sl] = ((y * lng_ref[:, sl] + lnb_ref[:, sl] + bonus_ref[:, sl]) * g_ref[:, sl]).astype(BF16)


def _rwkv_post(yf, yb, bonus, g, ln_g, ln_b, *, tm=256):
    m, d = yf.shape
    tm = min(tm, m)
    tok = pl.BlockSpec((tm, d), lambda i: (i, 0))
    par = pl.BlockSpec((1, d), lambda i: (0, 0))
    return _call(_rwkv_post_kernel, grid=(m // tm,), in_specs=[tok, tok, tok, tok, par, par], out_specs=tok,
                 out_shape=jax.ShapeDtypeStruct((m, d), BF16), sem=("parallel",), name="rwkv_post")(
                     yf, yb, bonus, g, ln_g.reshape(1, d), ln_b.reshape(1, d))


def _pad_cols(w, n):
    return jnp.zeros(w.shape[:-1] + (n,), w.dtype).at[..., :w.shape[-1]].set(w)


def _pad_rows(w, n):
    return jnp.zeros((n,) + w.shape[1:], w.dtype).at[:w.shape[0]].set(w)


def _rwkv_mixer(h, b, init, j, mu, w_rkv, w0, w1, w2, a0, a1, a2, g1, g2, k_k, k_a, r_k, ln_g, ln_b, w_out, x, gate1):
    m, d = h.shape
    t_len = m // b
    xmix = _rwkv_mix(h.reshape(b, t_len, d), mu[j]).reshape(6, m, d)
    proj = lambda idx, w, wlead, name, **kw: _mm(xmix, w, xlead=(idx,), wlead=wlead, tm=512, tn=1024, name=name, **kw)
    r = proj(0, w_rkv, (j, 0), "rwkv_r")
    k = proj(2, w_rkv, (j, 1), "rwkv_k")
    v = proj(3, w_rkv, (j, 2), "rwkv_v")
    rank_w = w1.shape[-1]
    w1cat = jnp.concatenate([w1[j, 0], w1[j, 1]], axis=1)
    w2bd = jnp.zeros((2 * rank_w, 2 * d), F32).at[:rank_w, :d].set(w2[j, 0]).at[rank_w:, d:].set(w2[j, 1])
    tw = proj(1, w1cat, (), "rwkv_w1", epilogue=jnp.tanh)
    lora = _mm(tw, w2bd, tm=512, tn=1024, name="rwkv_w2")
    a_pre = _mm(proj(4, _pad_cols(a1[j], LANES), (), "rwkv_a1"), _pad_rows(a2[j], LANES), tm=512, tn=1024, name="rwkv_a2")
    g = _mm(proj(5, g1[j], (), "rwkv_g1", epilogue=_sigmoid), g2[j], tm=512, tn=1024, name="rwkv_g2")
    kk, kb, k2, bonus, lw = _rwkv_prep(r, k, v, lora, a_pre, w0[j], a0[j], k_k[j], k_a[j], r_k[j].reshape(d))
    s3 = lambda z: z.reshape(b, t_len, d)
    yf, yb, sn = _rwkv_scan(s3(r), s3(k2), s3(v), s3(kk), s3(kb), lw.reshape(2, b, t_len, d), init)
    y = _rwkv_post(yf.reshape(m, d), yb.reshape(m, d), bonus, g, ln_g[j], ln_b[j])
    x = _mm(y, w_out, wlead=(j,), tm=512, tn=1024, extras=[x, gate1],
            epilogue=_residual_epilogue, name="rwkv_out")
    return x, sn


def _rwkv_state_to_pairs(s):
    b = s.shape[0]
    n = RWKV_N
    out = jnp.zeros((b, 2, RWKV_PAIRS, 2 * n, 2 * n), F32)
    return out.at[..., :n, :n].set(s[:, :, 0::2]).at[..., n:, n:].set(s[:, :, 1::2])


def _rwkv_state_from_pairs(sp):
    n = RWKV_N
    b = sp.shape[0]
    return jnp.stack([sp[..., :n, :n], sp[..., n:, n:]], axis=3).reshape(b, 2, RWKV_HEADS, n, n)


def _router_kernel(x_ref, w_ref, b_ref, o_ref, cnt_ref, seen, *, tiles_per_block):
    e0, per = ROUTER_EXPERT_LANE0, MOE_PER_GROUP

    @pl.when(pl.program_id(0) % tiles_per_block == 0)
    def _():
        seen[...] = jnp.zeros_like(seen)

    logits = _dot_3pass(x_ref[...], w_ref[...]) + b_ref[...]
    lane = lax.broadcasted_iota(jnp.int32, logits.shape, 1).astype(F32)
    big = float(LANES)
    rmax = lambda z: jnp.max(z, axis=-1, keepdims=True)
    rsum = lambda z: jnp.sum(z, axis=-1, keepdims=True)
    first = lambda m: jnp.min(jnp.where(m, lane, big), axis=-1, keepdims=True)
    is_g = lane < MOE_GROUPS
    gmax = rmax(jnp.where(is_g, logits, -jnp.inf))
    gsel = first(jnp.logical_and(is_g, logits >= gmax))
    gw = 1.0 / rsum(jnp.where(is_g, jnp.exp(logits - gmax), 0.0))
    in_grp = jnp.logical_and(lane >= e0 + gsel * per, lane < e0 + (gsel + 1) * per)
    emax = rmax(jnp.where(in_grp, logits, -jnp.inf))
    p = jnp.where(in_grp, jnp.exp(logits - emax), 0.0)
    p = p / rsum(p)
    p1 = rmax(jnp.where(in_grp, p, -1.0))
    i1 = first(jnp.logical_and(in_grp, p >= p1))
    rest = jnp.logical_and(in_grp, lane != i1)
    p2 = rmax(jnp.where(rest, p, -1.0))
    i2 = first(jnp.logical_and(rest, p >= p2))
    tot = p1 + p2
    comb = jnp.where(lane == i1, p1 / tot * gw, jnp.where(lane == i2, p2 / tot * gw, 0.0))
    onehot = jnp.where(lane == gsel, 1.0, 0.0)
    csum = seen[...] + jnp.dot(_tri(onehot.shape[0], "lower").astype(BF16), onehot.astype(BF16), preferred_element_type=F32)
    rank = rsum(onehot * csum) - 1.0
    seen[...] = csum[onehot.shape[0] - 1:, :]
    cnt_ref[0] = seen[...]
    o_ref[...] = jnp.where(lane == ROUTER_GROUP_LANE, gsel, jnp.where(lane == ROUTER_RANK_LANE, rank, comb))


def _router(x, w_group, b_group, w_expert, b_expert, *, tm=512):
    m, d = x.shape
    tm = min(tm, m)
    tiles_per_block = min(MOE_BLOCK_TOKENS, m) // tm
    e0 = ROUTER_EXPERT_LANE0
    w = jnp.zeros((d, LANES), F32).at[:, :MOE_GROUPS].set(w_group).at[:, e0:e0 + MOE_EXPERTS].set(w_expert)
    bias = jnp.zeros((1, LANES), F32).at[0, :MOE_GROUPS].set(b_group).at[0, e0:e0 + MOE_EXPERTS].set(b_expert)
    return _call(functools.partial(_router_kernel, tiles_per_block=tiles_per_block), grid=(m // tm,),
                 in_specs=[pl.BlockSpec((tm, d), lambda i: (i, 0)), pl.BlockSpec((d, LANES), lambda i: (0, 0)),
                           pl.BlockSpec((1, LANES), lambda i: (0, 0))],
                 out_specs=[pl.BlockSpec((tm, LANES), lambda i: (i, 0)),
                            pl.BlockSpec((1, 1, LANES), lambda i: (i // tiles_per_block, 0, 0))],
                 out_shape=[jax.ShapeDtypeStruct((m, LANES), F32),
                            jax.ShapeDtypeStruct((m // tm // tiles_per_block, 1, LANES), F32)],
                 scratch=[pltpu.VMEM((1, LANES), F32)], sem=("arbitrary",), name="moe_router")(x, w, bias)


def _moe_dispatch(comb, counts):
    n = comb.shape[0]
    tb, tr = min(MOE_BLOCK_TOKENS, n), MOE_ROW_TILE
    nblk = n // tb
    cnt = counts[:, 0, :MOE_GROUPS].astype(jnp.int32)
    padded = (cnt + tr - 1) // tr * tr
    start = jnp.cumsum(padded, axis=1) - padded
    grp = comb[:, ROUTER_GROUP_LANE].astype(jnp.int32).reshape(nblk, tb)
    rank = comb[:, ROUTER_RANK_LANE].astype(jnp.int32).reshape(nblk, tb)
    onehot = grp[..., None] == jnp.arange(MOE_GROUPS, dtype=jnp.int32)
    pos = jnp.sum(jnp.where(onehot, start[:, None, :], 0), axis=-1) + rank
    meta = jnp.concatenate([start, padded // tr], axis=1).astype(jnp.int32)
    return pos.reshape(-1), meta.reshape(-1)


def _moe_sparse_kernel(pos_ref, meta_ref, h_ref, comb_ref, wg_ref, wu_ref, wd_ref, y_ref, xs, cs, acc):
    blk, e = pl.program_id(0), pl.program_id(1)
    tr, per = MOE_ROW_TILE, MOE_PER_GROUP
    tb = h_ref.shape[0]
    g, i = e // per, e % per
    base, mbase = blk * tb, blk * 2 * MOE_GROUPS

    @pl.when(jnp.logical_and(blk == 0, e == 0))
    def _():
        xs[...] = jnp.zeros_like(xs)
        cs[...] = jnp.zeros_like(cs)

    @pl.when(e == 0)
    def _():
        def body(j, carry):
            t0 = pl.multiple_of(j * 8, 8)
            rows8, comb8 = h_ref[pl.ds(t0, 8), :], comb_ref[pl.ds(t0, 8), :]
            for u in range(8):
                r = pos_ref[base + t0 + u]
                xs[pl.ds(r, 1), :] = rows8[u:u + 1, :]
                cs[pl.ds(r, 1), :] = comb8[u:u + 1, :]
            return carry

        lax.fori_loop(0, tb // 8, body, 0)

    lo, n_tiles = meta_ref[mbase + g], meta_ref[mbase + MOE_GROUPS + g]
    lane = lax.broadcasted_iota(jnp.int32, (tr, LANES), 1)

    def tile_body(t, carry):
        rows = pl.ds(pl.multiple_of(lo + t * tr, tr), tr)
        x = xs[rows, :]
        ce = jnp.sum(jnp.where(lane == e + ROUTER_EXPERT_LANE0, cs[rows, :], 0.0), axis=-1, keepdims=True)
        hid = _silu(_dot(x, wg_ref[0, 0])) * _dot(x, wu_ref[0, 0])
        contrib = _dot(hid * ce, wd_ref[0, 0])

        @pl.when(i == 0)
        def _():
            acc[rows, :] = contrib

        @pl.when(i != 0)
        def _():
            acc[rows, :] += contrib

        return carry

    lax.fori_loop(0, n_tiles, tile_body, 0)

    @pl.when(e == pl.num_programs(1) - 1)
    def _():
        def body(j, carry):
            t0 = pl.multiple_of(j * 8, 8)
            for u in range(8):
                y_ref[pl.ds(t0 + u, 1), :] = acc[pl.ds(pos_ref[base + t0 + u], 1), :]
            return carry

        lax.fori_loop(0, tb // 8, body, 0)


def _moe_sparse(h, comb, counts, w_gate, w_up, w_down, layer):
    n, d = h.shape
    f = w_gate.shape[-1]
    tb, tr = min(MOE_BLOCK_TOKENS, n), MOE_ROW_TILE
    nblk, r_blk = n // tb, tb + MOE_GROUPS * tr
    tok, meta = _moe_dispatch(comb, counts)
    wspec = lambda shape: pl.BlockSpec((1, 1) + shape, lambda b, e, *_: (layer, e, 0, 0))
    once = pl.Buffered(1)
    grid_spec = pltpu.PrefetchScalarGridSpec(
        num_scalar_prefetch=2, grid=(nblk, MOE_EXPERTS),
        in_specs=[pl.BlockSpec((tb, d), lambda b, e, *_: (b, 0), pipeline_mode=once),
                  pl.BlockSpec((tb, LANES), lambda b, e, *_: (b, 0), pipeline_mode=once),
                  wspec((d, f)), wspec((d, f)), wspec((f, d))],
        out_specs=pl.BlockSpec((tb, d), lambda b, e, *_: (b, 0), pipeline_mode=once),
        scratch_shapes=[pltpu.VMEM((r_blk, d), F32), pltpu.VMEM((r_blk, LANES), F32), pltpu.VMEM((r_blk, d), F32)])
    return pl.pallas_call(
        _moe_sparse_kernel, grid_spec=grid_spec, out_shape=jax.ShapeDtypeStruct((n, d), F32), name="moe_sparse",
        compiler_params=pltpu.CompilerParams(dimension_semantics=("arbitrary", "arbitrary"),
                                             vmem_limit_bytes=VMEM_LIMIT_BYTES))(tok, meta, h, comb, w_gate, w_up, w_down)


def _run_trunk(x3, mods, rows, init_a, init_b, init_c, p):
    b, t_len, d = x3.shape
    m = b * t_len
    x = x3.reshape(m, d)
    new_a, new_b, new_c = [], [], []
    resid = None
    for i in range(p["ada_w"].shape[0]):
        j = i // 3
        shift1, scale1, gate1, shift2, scale2, gate2 = mods[i]
        h = _normmod(x, p["norm1_g"][i], scale1, shift1, resid=resid, out_dtype=F32 if i % 3 == 2 else BF16)
        if resid is not None:
            x, h = h
        if i % 3 == 0:
            x, st = _mlstm_mixer(h, b, rows, tuple(s[:, j] for s in init_a), p["mlstm_w_in"][j], p["mlstm_conv_w"][j],
                                 p["mlstm_w_gate"][j], p["mlstm_b_gate"][j], p["mlstm_norm_g"][j], p["mlstm_w_out"][j],
                                 x, gate1)
            new_a.append(st)
        elif i % 3 == 1:
            x, st = _gla_mixer(h, b, rows, tuple(s[:, j] for s in init_b), p["gla_w_in"][j], p["gla_conv_w"][j],
                               p["gla_w_a1"][j], p["gla_w_a2"][j], p["gla_b_a"][j], p["gla_norm_g"][j], p["gla_w_out"][j],
                               x, gate1)
            new_b.append(st)
        else:
            s0 = (jnp.zeros((b, 2, RWKV_PAIRS, LANES, LANES), F32) if init_c is None
                  else _rwkv_state_to_pairs(init_c[0][:, j]))
            x, st = _rwkv_mixer(h, b, s0, j,p["rwkv_mu"], p["rwkv_w_rkv"], p["rwkv_w0"],
                                p["rwkv_w1"], p["rwkv_w2"], p["rwkv_a0"], p["rwkv_a1"], p["rwkv_a2"], p["rwkv_g1"],
                                p["rwkv_g2"], p["rwkv_k_k"], p["rwkv_k_a"], p["rwkv_r_k"], p["rwkv_ln_g"], p["rwkv_ln_b"],
                                p["rwkv_w_out"], x, gate1)
            new_c.append((_rwkv_state_from_pairs(st),))
        hff = _normmod(x, p["norm2_g"][i], scale2, shift2)
        comb, counts = _router(hff, p["moe_w_group"][i], p["moe_b_group"][i], p["moe_w_expert"][i], p["moe_b_expert"][i])
        resid = (_moe_sparse(hff, comb, counts, p["moe_w_gate"], p["moe_w_up"], p["moe_w_down"], i), gate2)
    stack = lambda per_layer: tuple(jnp.stack(parts, axis=1) for parts in zip(*per_layer))
    y = _resid_rmsnorm(x, resid[0], resid[1], p["final_norm_g"]).reshape(b, t_len, d)
    return y, stack(new_a), stack(new_b), stack(new_c)


def kernel(x_prompt, x_sample, state_mlstm_C, state_mlstm_n, state_mlstm_m, state_gla_S, state_rwkv_S, c, c_ctx, ada_w, ada_b, norm1_g, norm2_g, moe_w_group, moe_b_group, moe_w_expert, moe_b_expert, moe_w_gate, moe_w_up, moe_w_down, final_norm_g, mlstm_w_in, mlstm_conv_w, mlstm_w_gate, mlstm_b_gate, mlstm_norm_g, mlstm_w_out, gla_w_in, gla_conv_w, gla_w_a1, gla_w_a2, gla_b_a, gla_norm_g, gla_w_out, rwkv_mu, rwkv_w_rkv, rwkv_w0, rwkv_w1, rwkv_w2, rwkv_a0, rwkv_a1, rwkv_a2, rwkv_g1, rwkv_g2, rwkv_k_k, rwkv_k_a, rwkv_r_k, rwkv_ln_g, rwkv_ln_b, rwkv_w_out):
    p = dict(ada_w=ada_w, norm1_g=norm1_g, norm2_g=norm2_g, moe_w_group=moe_w_group, moe_b_group=moe_b_group,
             moe_w_expert=moe_w_expert, moe_b_expert=moe_b_expert, moe_w_gate=moe_w_gate, moe_w_up=moe_w_up,
             moe_w_down=moe_w_down, final_norm_g=final_norm_g, mlstm_w_in=mlstm_w_in, mlstm_conv_w=mlstm_conv_w,
             mlstm_w_gate=mlstm_w_gate, mlstm_b_gate=mlstm_b_gate, mlstm_norm_g=mlstm_norm_g, mlstm_w_out=mlstm_w_out,
             gla_w_in=gla_w_in, gla_conv_w=gla_conv_w, gla_w_a1=gla_w_a1, gla_w_a2=gla_w_a2, gla_b_a=gla_b_a,
             gla_norm_g=gla_norm_g, gla_w_out=gla_w_out, rwkv_mu=rwkv_mu, rwkv_w_rkv=rwkv_w_rkv, rwkv_w0=rwkv_w0,
             rwkv_w1=rwkv_w1, rwkv_w2=rwkv_w2, rwkv_a0=rwkv_a0, rwkv_a1=rwkv_a1, rwkv_a2=rwkv_a2, rwkv_g1=rwkv_g1,
             rwkv_g2=rwkv_g2, rwkv_k_k=rwkv_k_k, rwkv_k_a=rwkv_k_a, rwkv_r_k=rwkv_r_k, rwkv_ln_g=rwkv_ln_g,
             rwkv_ln_b=rwkv_ln_b, rwkv_w_out=rwkv_w_out)
    for name in ("moe_w_gate", "moe_w_up", "moe_w_down", "mlstm_w_in", "mlstm_w_out", "gla_w_in", "gla_w_out",
                 "rwkv_w_rkv", "rwkv_w_out"):
        p[name] = p[name].astype(BF16)
    depth, d = ada_w.shape[0], ada_w.shape[1]
    n_dec = c.shape[0]
    cond8 = jnp.zeros((8, d), F32).at[0].set(c_ctx).at[1:1 + n_dec].set(c)
    mod = _ada(cond8, ada_w, ada_b)
    split = lambda rows: [tuple(rows[i][:, None, k * d:(k + 1) * d] for k in range(6)) for i in range(depth)]
    mods_ctx = split(mod[:, 0:1])
    mods_dec = split(mod[:, 1:1 + n_dec])

    bp = x_prompt.shape[0]
    zeros_like_ctx = lambda s: jnp.zeros((bp,) + s.shape[1:], F32)
    y_prompt, st_a, st_b, st_c = _run_trunk(
        x_prompt, mods_ctx, 1, tuple(zeros_like_ctx(s) for s in (state_mlstm_C, state_mlstm_n, state_mlstm_m)),
        (zeros_like_ctx(state_gla_S),), None, p)
    y_sample = _run_trunk(x_sample, mods_dec, x_sample.shape[1] // GRID_W,
                          (state_mlstm_C, state_mlstm_n, state_mlstm_m), (state_gla_S,), (state_rwkv_S,), p)[0]
    return (y_prompt, y_sample, st_a[0], st_a[1], st_a[2], st_b[0], st_c[0])
```

```python
import functools

import jax
import jax.numpy as jnp
from jax import lax
from jax.experimental import pallas as pl
from jax.experimental.pallas import tpu as pltpu

F32 = jnp.float32
BF16 = jnp.bfloat16

D_MODEL = 1024
GRID_W = 64
NORM_EPS = 1e-6

MLSTM_HEADS = 4
MLSTM_DK = 128
MLSTM_DV = 256
MLSTM_CHUNK = 128

GLA_HEADS = 4
GLA_DK = 128
GLA_DV = 256
GLA_TAU = 16.0
GLA_CHUNK = 16
GLA_BLOCK = 32

RWKV_N = 64
RWKV_HEADS = 16
RWKV_PAIRS = RWKV_HEADS // 2
RWKV_GN_EPS = 64e-5
RWKV_CHUNK = 32

MOE_GROUPS = 4
MOE_PER_GROUP = 4
MOE_EXPERTS = 16
ROUTER_EXPERT_LANE0 = 8
ROUTER_GROUP_LANE = 127
ROUTER_RANK_LANE = 126
MOE_BLOCK_TOKENS = 2048
MOE_ROW_TILE = 256

CONV_BLOCK_BYTES = 2 * 1024 * 1024
SCAN_BATCH_ROWS = 2

LANES = 128
VMEM_LIMIT_BYTES = 56 * 1024 * 1024


def _call(kernel, *, grid, in_specs, out_specs, out_shape, scratch=(), sem, name):
    return pl.pallas_call(
        kernel, grid=grid, in_specs=in_specs, out_specs=out_specs, out_shape=out_shape,
        scratch_shapes=list(scratch), name=name,
        compiler_params=pltpu.CompilerParams(dimension_semantics=sem, vmem_limit_bytes=VMEM_LIMIT_BYTES))


def _sigmoid(x):
    return 1.0 / (1.0 + jnp.exp(-x))


def _silu(x):
    return x * _sigmoid(x)


def _logsig(x):
    return jnp.minimum(x, 0.0) - jnp.log1p(jnp.exp(-jnp.abs(x)))


def _dot(a, b):
    return jnp.dot(a.astype(BF16), b.astype(BF16), preferred_element_type=F32)


def _dot_nt(a, b):
    return lax.dot_general(a.astype(BF16), b.astype(BF16), (((1,), (1,)), ((), ())), preferred_element_type=F32)


def _dot_tn(a, b):
    return lax.dot_general(a.astype(BF16), b.astype(BF16), (((0,), (0,)), ((), ())), preferred_element_type=F32)


def _split_bf16(x, parts):
    out = []
    for _ in range(parts):
        piece = x.astype(BF16)
        out.append(piece)
        x = x - piece.astype(F32)
    return out


def _dot_exact_lhs(a, b):
    a = a.astype(BF16)
    return sum(jnp.dot(a, piece, preferred_element_type=F32) for piece in _split_bf16(b, 3))


def _dot_3pass(a, b):
    (a_hi, a_lo), (b_hi, b_lo) = _split_bf16(a, 2), _split_bf16(b, 2)
    dot = lambda p, q: jnp.dot(p, q, preferred_element_type=F32)
    return dot(a_hi, b_hi) + (dot(a_hi, b_lo) + dot(a_lo, b_hi))


def _tri(n, kind, block=None):
    r = lax.broadcasted_iota(jnp.int32, (n, n), 0)
    c = lax.broadcasted_iota(jnp.int32, (n, n), 1)
    m = {"lower": c <= r, "upper": c >= r, "all": c >= 0}[kind]
    if block is not None:
        sh = block.bit_length() - 1
        m = jnp.logical_and(m, jnp.right_shift(r, sh) == jnp.right_shift(c, sh))
    return jnp.where(m, 1.0, 0.0).astype(F32)


def _mm(x, w, *, tm, tn, xlead=(), wlead=(), extras=(), epilogue=None, out_dtype=F32, name):
    m, k = x.shape[-2:]
    n = w.shape[-1]
    tm, tn = min(tm, m), min(tn, n)

    def kern(x_ref, w_ref, *rest):
        acc = _dot(x_ref[...], w_ref[...])
        if epilogue is not None:
            acc = epilogue(acc, *[r[...] for r in rest[:-1]])
        rest[-1][...] = acc.astype(out_dtype)

    def extra_spec(arr):
        if arr.ndim == 2:
            return pl.BlockSpec((tm, tn), lambda i, j: (i, j))
        tiles_per_row = (m // arr.shape[0]) // tm
        return pl.BlockSpec((1, 1, tn), lambda i, j: (i // tiles_per_row, 0, j))

    in_specs = [
        pl.BlockSpec((None,) * len(xlead) + (tm, k), lambda i, j: tuple(xlead) + (i, 0)),
        pl.BlockSpec((None,) * len(wlead) + (k, tn), lambda i, j: tuple(wlead) + (0, j)),
    ] + [extra_spec(a) for a in extras]
    return _call(kern, grid=(m // tm, n // tn), in_specs=in_specs,
                 out_specs=pl.BlockSpec((tm, tn), lambda i, j: (i, j)),
                 out_shape=jax.ShapeDtypeStruct((m, n), out_dtype),
                 sem=("parallel", "parallel"), name=name)(x, w, *extras)


def _residual_epilogue(acc, x, gate):
    return x + gate[0] * acc


def _ada_kernel(c_ref, w_ref, b_ref, o_ref):
    o_ref[0] = _dot(_silu(c_ref[...]), w_ref[0]) + b_ref[0]


def _ada(cond8, ada_w, ada_b):
    depth, d, n = ada_w.shape
    tn = 1536
    return _call(_ada_kernel, grid=(depth, n // tn),
                 in_specs=[pl.BlockSpec((8, d), lambda l, j: (0, 0)),
                           pl.BlockSpec((1, d, tn), lambda l, j: (l, 0, j)),
                           pl.BlockSpec((1, 1, tn), lambda l, j: (l, 0, j))],
                 out_specs=pl.BlockSpec((1, 8, tn), lambda l, j: (l, 0, j)),
                 out_shape=jax.ShapeDtypeStruct((depth, 8, n), F32),
                 sem=("parallel", "parallel"), name="ada")(cond8, ada_w, ada_b.reshape(depth, 1, n))


def _normmod_kernel(x_ref, g_ref, sc_ref, sh_ref, o_ref):
    x = x_ref[...]
    y = x * lax.rsqrt(jnp.mean(x * x, axis=-1, keepdims=True) + NORM_EPS)
    o_ref[...] = ((y * g_ref[...]) * (1.0 + sc_ref[0]) + sh_ref[0]).astype(o_ref.dtype)


def _resid_normmod_kernel(x_ref, y_ref, gate_ref, g_ref, sc_ref, sh_ref, xo_ref, o_ref):
    x = x_ref[...] + gate_ref[0] * y_ref[...]
    xo_ref[...] = x
    y = x * lax.rsqrt(jnp.mean(x * x, axis=-1, keepdims=True) + NORM_EPS)
    o_ref[...] = ((y * g_ref[...]) * (1.0 + sc_ref[0]) + sh_ref[0]).astype(o_ref.dtype)


def _normmod(x, g, scale, shift, *, resid=None, out_dtype=F32, tm=512):
    m, d = x.shape
    tm = min(tm, m)
    tiles_per_row = (m // scale.shape[0]) // tm
    mod = pl.BlockSpec((1, 1, d), lambda i: (i // tiles_per_row, 0, 0))
    tok = pl.BlockSpec((tm, d), lambda i: (i, 0))
    par = pl.BlockSpec((1, d), lambda i: (0, 0))
    if resid is None:
        return _call(_normmod_kernel, grid=(m // tm,), in_specs=[tok, par, mod, mod], out_specs=tok,
                     out_shape=jax.ShapeDtypeStruct((m, d), out_dtype), sem=("parallel",), name="normmod")(
                         x, g.reshape(1, d), scale, shift)
    y, gate = resid
    return _call(_resid_normmod_kernel, grid=(m // tm,), in_specs=[tok, tok, mod, par, mod, mod], out_specs=[tok, tok],
                 out_shape=[jax.ShapeDtypeStruct((m, d), F32), jax.ShapeDtypeStruct((m, d), out_dtype)],
                 sem=("parallel",), name="resid_normmod")(x, y, gate, g.reshape(1, d), scale, shift)


def _resid_rmsnorm_kernel(x_ref, y_ref, gate_ref, g_ref, o_ref):
    x = x_ref[...] + gate_ref[0] * y_ref[...]
    o_ref[...] = x * lax.rsqrt(jnp.mean(x * x, axis=-1, keepdims=True) + NORM_EPS) * g_ref[...]


def _resid_rmsnorm(x, y, gate, g, *, tm=512):
    m, d = x.shape
    tm = min(tm, m)
    tiles_per_row = (m // gate.shape[0]) // tm
    tok = pl.BlockSpec((tm, d), lambda i: (i, 0))
    return _call(_resid_rmsnorm_kernel, grid=(m // tm,),
                 in_specs=[tok, tok, pl.BlockSpec((1, 1, d), lambda i: (i // tiles_per_row, 0, 0)),
                           pl.BlockSpec((1, d), lambda i: (0, 0))],
                 out_specs=tok, out_shape=jax.ShapeDtypeStruct((m, d), F32), sem=("parallel",), name="final_norm")(
                     x, y, gate, g.reshape(1, d))


def _conv_kernel(x_ref, w_ref, o_ref, *, rows, width):
    x = x_ref[0].astype(F32)
    t_len = x.shape[0]
    t = lax.broadcasted_iota(jnp.int32, x.shape, 0)
    col = jnp.bitwise_and(t, width - 1)
    row = jnp.right_shift(t, width.bit_length() - 1)
    acc = jnp.zeros_like(x)
    for dr in (-1, 0, 1):
        if rows == 1 and dr != 0:
            continue
        for dc in (-1, 0, 1):
            off = dr * width + dc
            xs = x if off == 0 else pltpu.roll(x, (-off) % t_len, axis=0)
            ok = jnp.where(jnp.logical_and(col + dc >= 0, col + dc < width), 1.0, 0.0)
            if dr != 0:
                ok = ok * jnp.where(jnp.logical_and(row + dr >= 0, row + dr < rows), 1.0, 0.0)
            tap = (dr + 1) * 3 + (dc + 1)
            acc = acc + (xs * ok) * w_ref[tap:tap + 1, :]
    o_ref[0] = _silu(acc)


def _conv_silu(big, conv_w, *, rows, width, channels):
    b, t_len, _ = big.shape
    assert width & (width - 1) == 0 and rows * width == t_len
    tc = max(2 * LANES, min(channels, CONV_BLOCK_BYTES // (4 * t_len)))
    return _call(functools.partial(_conv_kernel, rows=rows, width=width), grid=(b, channels // tc),
                 in_specs=[pl.BlockSpec((1, t_len, tc), lambda i, j: (i, 0, j)),
                           pl.BlockSpec((9, tc), lambda i, j: (0, j))],
                 out_specs=pl.BlockSpec((1, t_len, tc), lambda i, j: (i, 0, j)),
                 out_shape=jax.ShapeDtypeStruct((b, t_len, channels), F32),
                 sem=("parallel", "parallel"), name="conv_silu")(big, conv_w.reshape(9, channels))


def _mlstm_gate_kernel(h_ref, wi_ref, wf_ref, bi_ref, bf_ref, gb_ref, gw_ref, ga_ref):
    nh, L = MLSTM_HEADS, MLSTM_CHUNK
    h = h_ref[0]
    tg = h.shape[0]
    ig = _dot(h, wi_ref[...]) + bi_ref[...]
    lf = _logsig(_dot(h, wf_ref[...]) + bf_ref[...])
    lane = lax.broadcasted_iota(jnp.int32, ig.shape, 1)
    b = jnp.where(lane < nh, _dot_exact_lhs(_tri(tg, "lower", L), lf), _dot_exact_lhs(_tri(tg, "upper", L), lf))
    b_last = _dot_exact_lhs(_tri(tg, "all", L), lf)
    gb_ref[0] = b
    gw_ref[0] = b_last - b + ig
    ga_ref[0] = ig - b


def _mlstm_gates(h3, w_gate, b_gate, *, tg=256):
    b, t_len, d = h3.shape
    nh = MLSTM_HEADS
    tg = min(tg, t_len)
    wi = _pad_cols(jnp.concatenate([w_gate[0, :, :nh], w_gate[1, :, :nh]], axis=1), LANES)
    wf = _pad_cols(jnp.concatenate([w_gate[0, :, nh:], w_gate[1, :, nh:]], axis=1), LANES)
    bi = _pad_cols(jnp.concatenate([b_gate[0, :nh], b_gate[1, :nh]]).reshape(1, 2 * nh), LANES)
    bf = _pad_cols(jnp.concatenate([b_gate[0, nh:], b_gate[1, nh:]]).reshape(1, 2 * nh), LANES)
    full = lambda shape: pl.BlockSpec(shape, lambda i, c: (0,) * len(shape))
    tok = pl.BlockSpec((1, tg, LANES), lambda i, c: (i, c, 0))
    return _call(_mlstm_gate_kernel, grid=(b, t_len // tg),
                 in_specs=[pl.BlockSpec((1, tg, d), lambda i, c: (i, c, 0)),
                           full((d, LANES)), full((d, LANES)), full((1, LANES)), full((1, LANES))],
                 out_specs=[tok, tok, tok], out_shape=[jax.ShapeDtypeStruct((b, t_len, LANES), F32)] * 3,
                 sem=("parallel", "parallel"), name="mlstm_gates")(h3, wi, wf, bi, bf)


def _mlstm_scan_kernel(qkf_ref, qkb_ref, vf_ref, vb_ref, gbf_ref, gbb_ref, gwf_ref, gwb_ref, gaf_ref, gab_ref,
                       c0_ref, n0_ref, m0_ref, yf_ref, yb_ref, cn_ref, nn_ref, mn_ref, c_s, n_s, m_s):
    nh, L, dk, dv = MLSTM_HEADS, MLSTM_CHUNK, MLSTM_DK, MLSTM_DV
    c = pl.program_id(1)

    @pl.when(c == 0)
    def _():
        c_s[...] = c0_ref[...]
        n_s[...] = n0_ref[...]
        m_s[...] = m0_ref[...]

    ti = lax.broadcasted_iota(jnp.int32, (L, L), 0)
    si = lax.broadcasted_iota(jnp.int32, (L, L), 1)
    lane = lax.broadcasted_iota(jnp.int32, (L, LANES), 1)
    ones = jnp.ones((L, LANES), BF16)
    refs = ((qkf_ref, vf_ref, gbf_ref, gwf_ref, gaf_ref, yf_ref), (qkb_ref, vb_ref, gbb_ref, gwb_ref, gab_ref, yb_ref))
    chains = [(i, d, h) for i in range(c_s.shape[0]) for d in range(2) for h in range(nh)]
    st = {}
    for i, d, h in chains:
        qk_ref, v_ref, gb_ref, gw_ref, ga_ref, _ = refs[d]
        g = d * nh + h
        q = qk_ref[i, :, h * dk:(h + 1) * dk] * (dk ** -0.5)
        k = qk_ref[i, :, (nh + h) * dk:(nh + h + 1) * dk]
        v = v_ref[i, :, h * dv:(h + 1) * dv]
        a_row = sum(lax.dot_general(ones, piece, (((1,), (1,)), ((), ())), preferred_element_type=F32)
                    for piece in _split_bf16(jnp.where(lane == g, ga_ref[i], 0.0), 3))
        b_col = gb_ref[i, :, g:g + 1]
        wl_col = gw_ref[i, :, g:g + 1]
        st[i, d, h] = dict(q=q, k=k, v=v, a_row=a_row, b_col=b_col, wl_col=wl_col, qk=_dot_nt(q, k))
    for i, d, h in chains:
        z = st[i, d, h]
        m_st = m_s[i, d, h]
        mask = (si <= ti) if d == 0 else (si >= ti)
        log_d = jnp.where(mask, z["b_col"] + z["a_row"], -jnp.inf)
        m_t = jnp.maximum(z["b_col"] + m_st, jnp.max(log_d, axis=-1, keepdims=True))
        z.update(m_t=m_t, s=z["qk"] * jnp.exp(log_d - m_t), w_inter=jnp.exp(z["b_col"] + m_st - m_t))
    for i, d, h in chains:
        z = st[i, d, h]
        q, s, w_inter = z["q"], z["s"], z["w_inter"]
        num = _dot(s, z["v"]) + w_inter * _dot(q, c_s[i, d, h])
        den = jnp.sum(s, axis=-1, keepdims=True) + w_inter * jnp.sum(q * n_s[i, d, h], axis=-1, keepdims=True)
        refs[d][5][i, :, h * dv:(h + 1) * dv] = num / jnp.maximum(jnp.abs(den), jnp.exp(-z["m_t"]))
    for i, d, h in chains:
        z = st[i, d, h]
        last = L - 1 if d == 0 else 0
        m_st = m_s[i, d, h]
        b_last = z["b_col"][last:last + 1, :]
        m_new = jnp.maximum(b_last + m_st, jnp.max(z["wl_col"], axis=0, keepdims=True))
        ks = jnp.exp(z["wl_col"] - m_new) * z["k"]
        decay = jnp.exp(b_last + m_st - m_new)
        c_s[i, d, h] = decay * c_s[i, d, h] + _dot_tn(ks, z["v"])
        n_s[i, d, h] = decay * n_s[i, d, h] + jnp.sum(ks, axis=0, keepdims=True)
        m_s[i, d, h] = m_new

    @pl.when(c == pl.num_programs(1) - 1)
    def _():
        cn_ref[...] = c_s[...]
        nn_ref[...] = n_s[...]
        mn_ref[...] = m_s[...]


def _mlstm_scan(qk, big, gb, gw, ga, c0, n0, m0, *, nb=SCAN_BATCH_ROWS):
    b, t_len, _ = qk.shape
    nh, L, dk, dv = MLSTM_HEADS, MLSTM_CHUNK, MLSTM_DK, MLSTM_DV
    nc = t_len // L
    fwd = lambda i, c: (i, c, 0)
    bwd = lambda i, c: (i, nc - 1 - c, 0)
    state = lambda shape: pl.BlockSpec((nb,) + shape, lambda i, c: (i,) + (0,) * len(shape))
    st_shapes = [(2, nh, dk, dv), (2, nh, 1, dk), (2, nh, 1, 1)]
    return _call(
        _mlstm_scan_kernel, grid=(b // nb, nc),
        in_specs=[pl.BlockSpec((nb, L, 2 * nh * dk), fwd), pl.BlockSpec((nb, L, 2 * nh * dk), bwd),
                  pl.BlockSpec((nb, L, nh * dv), lambda i, c: (i, c, 1)),
                  pl.BlockSpec((nb, L, nh * dv), lambda i, c: (i, nc - 1 - c, 1)),
                  pl.BlockSpec((nb, L, LANES), fwd), pl.BlockSpec((nb, L, LANES), bwd),
                  pl.BlockSpec((nb, L, LANES), fwd), pl.BlockSpec((nb, L, LANES), bwd),
                  pl.BlockSpec((nb, L, LANES), fwd), pl.BlockSpec((nb, L, LANES), bwd)] + [state(s) for s in st_shapes],
        out_specs=[pl.BlockSpec((nb, L, nh * dv), fwd), pl.BlockSpec((nb, L, nh * dv), bwd)] + [state(s) for s in st_shapes],
        out_shape=[jax.ShapeDtypeStruct((b, t_len, nh * dv), F32)] * 2 + [jax.ShapeDtypeStruct((b,) + s, F32) for s in st_shapes],
        scratch=[pltpu.VMEM((nb,) + s, F32) for s in st_shapes],
        sem=("parallel", "arbitrary"), name="mlstm_scan")(qk, qk, big, big, gb, gb, gw, gw, ga, ga, c0, n0, m0)


def _mlstm_post_kernel(yf_ref, yb_ref, o_ref, g_ref, out_ref):
    dv = MLSTM_DV
    for h in range(MLSTM_HEADS):
        sl = slice(h * dv, (h + 1) * dv)
        y = yf_ref[:, sl] + yb_ref[:, sl]
        y = y - jnp.mean(y, axis=-1, keepdims=True)
        y = y * lax.rsqrt(jnp.mean(y * y, axis=-1, keepdims=True) + NORM_EPS)
        out_ref[:, sl] = (_sigmoid(o_ref[:, sl].astype(F32)) * (y * g_ref[:, sl])).astype(BF16)


def _mlstm_post(yf, yb, big, norm_g, *, tm=512):
    m, d = yf.shape
    tm = min(tm, m)
    row = lambda i: (i, 0)
    return _call(_mlstm_post_kernel, grid=(m // tm,),
                 in_specs=[pl.BlockSpec((tm, d), row), pl.BlockSpec((tm, d), row),
                           pl.BlockSpec((tm, d), lambda i: (i, 2)), pl.BlockSpec((1, d), lambda i: (0, 0))],
                 out_specs=pl.BlockSpec((tm, d), row), out_shape=jax.ShapeDtypeStruct((m, d), BF16),
                 sem=("parallel",), name="mlstm_post")(yf, yb, big, norm_g.reshape(1, d))


def _mlstm_mixer(h, b, rows, init, w_in, conv_w, w_gate, b_gate, norm_g, w_out, x, gate1):
    m, d = h.shape
    t_len = m // b
    nh, dk = MLSTM_HEADS, MLSTM_DK
    big = _mm(h, w_in, tm=1024, tn=1024, out_dtype=BF16, name="mlstm_in")
    big3 = big.reshape(b, t_len, 3 * d)
    qk = _conv_silu(big3, conv_w, rows=rows, width=t_len // rows, channels=2 * nh * dk)
    gb, gw, ga = _mlstm_gates(h.reshape(b, t_len, d), w_gate, b_gate)
    c0, n0, m0 = init
    yf, yb, cn, nn, mn = _mlstm_scan(qk, big3, gb, gw, ga, c0, n0.reshape(b, 2, nh, 1, dk), m0.reshape(b, 2, nh, 1, 1))
    y = _mlstm_post(yf.reshape(m, d), yb.reshape(m, d), big, norm_g)
    x = _mm(y, w_out, tm=512, tn=1024, extras=[x, gate1],
            epilogue=_residual_epilogue, name="mlstm_out")
    return x, (cn, nn.reshape(b, 2, nh, dk), mn.reshape(b, 2, nh))


def _gla_gate_kernel(h_ref, w1_ref, w2_ref, b_ref, gf_ref, gb_ref):
    half = GLA_HEADS * GLA_DK
    h = h_ref[0]
    z = _dot(_dot(h, w1_ref[...]), w2_ref[...]) + b_ref[...]
    lg = _logsig(z) * (1.0 / GLA_TAU)
    tg = h.shape[0]
    gf_ref[0] = _dot_exact_lhs(_tri(tg, "lower", GLA_CHUNK), lg[:, :half])
    gb_ref[0] = _dot_exact_lhs(_tri(tg, "upper", GLA_CHUNK), lg[:, half:])


def _gla_gates(h3, w_a1, w_a2, b_a, *, tg=256):
    b, t_len, d = h3.shape
    rank, half = w_a1.shape[-1], w_a2.shape[-1]
    w1 = jnp.zeros((d, LANES), F32).at[:, :rank].set(w_a1[0]).at[:, rank:2 * rank].set(w_a1[1])
    w2 = jnp.zeros((LANES, 2 * half), F32).at[:rank, :half].set(w_a2[0]).at[rank:2 * rank, half:].set(w_a2[1])
    bias = jnp.concatenate([b_a[0], b_a[1]]).reshape(1, 2 * half)
    tg = min(tg, t_len)
    full = lambda shape: pl.BlockSpec(shape, lambda i, c: (0,) * len(shape))
    tok = lambda n: pl.BlockSpec((1, tg, n), lambda i, c: (i, c, 0))
    return _call(_gla_gate_kernel, grid=(b, t_len // tg),
                 in_specs=[tok(d), full((d, LANES)), full((LANES, 2 * half)), full((1, 2 * half))],
                 out_specs=[tok(half), tok(half)],
                 out_shape=[jax.ShapeDtypeStruct((b, t_len, half), F32)] * 2,
                 sem=("parallel", "parallel"), name="gla_gates")(h3, w1, w2, bias)


def _gla_scan_kernel(xf_ref, xb_ref, gf_ref, gb_ref, s0_ref, yf_ref, yb_ref, sn_ref, s_s):
    nh, L, dk, dv = GLA_HEADS, GLA_CHUNK, GLA_DK, GLA_DV
    c = pl.program_id(1)

    @pl.when(c == 0)
    def _():
        s_s[...] = s0_ref[0]

    tcol = lax.broadcasted_iota(jnp.int32, (L, 1), 0)
    eye = jnp.where(lax.broadcasted_iota(jnp.int32, (dk, dk), 0) == lax.broadcasted_iota(jnp.int32, (dk, dk), 1), 1.0, 0.0)
    refs = ((xf_ref, gf_ref, yf_ref), (xb_ref, gb_ref, yb_ref))
    chains = [(d, h) for d in range(2) for h in range(nh)]
    n_sub = xf_ref.shape[1] // L
    for step in range(n_sub):
        st = {}
        for d, h in chains:
            x_ref, g_ref, _ = refs[d]
            rows = slice((step if d == 0 else n_sub - 1 - step) * L, (step + 1 if d == 0 else n_sub - step) * L)
            q = x_ref[0, rows, h * dk:(h + 1) * dk] * (dk ** -0.5)
            k = x_ref[0, rows, (nh + h) * dk:(nh + h + 1) * dk]
            v = x_ref[0, rows, 2 * nh * dk + h * dv:2 * nh * dk + (h + 1) * dv]
            g = g_ref[0, rows, h * dk:(h + 1) * dk]
            st[d, h] = dict(rows=rows, q=q, k=k, v=v, g=g, o=_dot(q * jnp.exp(g), s_s[d, h]))
        sub = 8
        for s in range(L):
            for d, h in chains:
                z = st[d, h]
                q, k, v, g, o = z["q"], z["k"], z["v"], z["g"], z["o"]
                lo, hi = ((s // sub) * sub, L) if d == 0 else (0, (s // sub + 1) * sub)
                dec = jnp.exp(jnp.minimum(g[lo:hi] - g[s:s + 1, :], 0.0))
                col = jnp.sum(q[lo:hi] * k[s:s + 1, :] * dec, axis=-1, keepdims=True)
                col = jnp.where((tcol[lo:hi] >= s) if d == 0 else (tcol[lo:hi] <= s), col, 0.0)
                parts = [o[:lo]] * (lo > 0) + [o[lo:hi] + col * v[s:s + 1, :]] + [o[hi:]] * (hi < L)
                z["o"] = parts[0] if len(parts) == 1 else jnp.concatenate(parts, axis=0)
        for d, h in chains:
            z = st[d, h]
            k, v, g = z["k"], z["v"], z["g"]
            last = L - 1 if d == 0 else 0
            refs[d][2][0, z["rows"], h * dv:(h + 1) * dv] = z["o"]
            g_last = g[last:last + 1, :]
            decay_col = jnp.sum(eye * jnp.exp(g_last), axis=-1, keepdims=True)
            s_s[d, h] = decay_col * s_s[d, h] + _dot_tn(k * jnp.exp(g_last - g), v)

    @pl.when(c == pl.num_programs(1) - 1)
    def _():
        sn_ref[0] = s_s[...]


def _gla_scan(qkv, gf, gb, s0):
    b, t_len, width = qkv.shape
    nh, L, dk, dv = GLA_HEADS, GLA_BLOCK, GLA_DK, GLA_DV
    nc = t_len // L
    fwd = lambda i, c: (i, c, 0)
    bwd = lambda i, c: (i, nc - 1 - c, 0)
    st = (2, nh, dk, dv)
    state = pl.BlockSpec((1,) + st, lambda i, c: (i, 0, 0, 0, 0))
    return _call(
        _gla_scan_kernel, grid=(b, nc),
        in_specs=[pl.BlockSpec((1, L, width), fwd), pl.BlockSpec((1, L, width), bwd),
                  pl.BlockSpec((1, L, nh * dk), fwd), pl.BlockSpec((1, L, nh * dk), bwd), state],
        out_specs=[pl.BlockSpec((1, L, nh * dv), fwd), pl.BlockSpec((1, L, nh * dv), bwd), state],
        out_shape=[jax.ShapeDtypeStruct((b, t_len, nh * dv), F32)] * 2 + [jax.ShapeDtypeStruct((b,) + st, F32)],
        scratch=[pltpu.VMEM(st, F32)], sem=("parallel", "arbitrary"), name="gla_scan")(qkv, qkv, gf, gb, s0)


def _gla_post_kernel(yf_ref, yb_ref, r_ref, g_ref, out_ref):
    dv = GLA_DV
    for h in range(GLA_HEADS):
        sl = slice(h * dv, (h + 1) * dv)
        y = yf_ref[:, sl] + yb_ref[:, sl]
        y = y * lax.rsqrt(jnp.mean(y * y, axis=-1, keepdims=True) + NORM_EPS)
        out_ref[:, sl] = ((y * g_ref[:, sl]) * _silu(r_ref[:, sl].astype(F32))).astype(BF16)


def _gla_post(yf, yb, big, norm_g, *, tm=512):
    m, d = yf.shape
    tm = min(tm, m)
    row = lambda i: (i, 0)
    return _call(_gla_post_kernel, grid=(m // tm,),
                 in_specs=[pl.BlockSpec((tm, d), row), pl.BlockSpec((tm, d), row),
                           pl.BlockSpec((tm, d), lambda i: (i, 2)), pl.BlockSpec((1, d), lambda i: (0, 0))],
                 out_specs=pl.BlockSpec((tm, d), row), out_shape=jax.ShapeDtypeStruct((m, d), BF16),
                 sem=("parallel",), name="gla_post")(yf, yb, big, norm_g.reshape(1, d))


def _gla_mixer(h, b, rows, init, w_in, conv_w, w_a1, w_a2, b_a, norm_g, w_out, x, gate1):
    m, d = h.shape
    t_len = m // b
    big = _mm(h, w_in, tm=1024, tn=1024, out_dtype=BF16, name="gla_in")
    big3 = big.reshape(b, t_len, 3 * d)
    qkv = _conv_silu(big3, conv_w, rows=rows, width=t_len // rows, channels=2 * d)
    gf, gb = _gla_gates(h.reshape(b, t_len, d), w_a1, w_a2, b_a)
    yf, yb, sn = _gla_scan(qkv, gf, gb, init[0])
    y = _gla_post(yf.reshape(m, d), yb.reshape(m, d), big, norm_g)
    x = _mm(y, w_out, tm=512, tn=1024, extras=[x, gate1],
            epilogue=_residual_epilogue, name="gla_out")
    return x, (sn,)


def _rwkv_mix_kernel(h_ref, hp_ref, hn_ref, mu_ref, o_ref):
    i = pl.program_id(1)
    h = h_ref[0]
    tm = h.shape[0]
    prev_row = jnp.where(i > 0, hp_ref[0, 7:8, :], 0.0)
    next_row = jnp.where(i < pl.num_programs(1) - 1, hn_ref[0, 0:1, :], 0.0)
    row = lax.broadcasted_iota(jnp.int32, (tm, 1), 0)
    x_prev = jnp.where(row == 0, prev_row, pltpu.roll(h, 1, axis=0))
    x_next = jnp.where(row == tm - 1, next_row, pltpu.roll(h, tm - 1, axis=0))
    xx = 0.5 * (x_prev + x_next) - h
    for j in range(6):
        o_ref[j, 0] = (h + xx * mu_ref[j:j + 1, :]).astype(BF16)


def _rwkv_mix(h3, mu, *, tm=256):
    b, t_len, d = h3.shape
    tm = min(tm, t_len)
    nt, n8 = t_len // tm, t_len // 8
    return _call(_rwkv_mix_kernel, grid=(b, nt),
                 in_specs=[pl.BlockSpec((1, tm, d), lambda i, j: (i, j, 0)),
                           pl.BlockSpec((1, 8, d), lambda i, j: (i, jnp.maximum(j * (tm // 8) - 1, 0), 0)),
                           pl.BlockSpec((1, 8, d), lambda i, j: (i, jnp.minimum((j + 1) * (tm // 8), n8 - 1), 0)),
                           pl.BlockSpec((6, d), lambda i, j: (0, 0))],
                 out_specs=pl.BlockSpec((6, 1, tm, d), lambda i, j: (0, i, j, 0)),
                 out_shape=jax.ShapeDtypeStruct((6, b, t_len, d), BF16),
                 sem=("parallel", "parallel"), name="rwkv_mix")(h3, h3, h3, mu)


def _segsum(x):
    lo = lax.broadcasted_iota(jnp.int32, x.shape, 1) < RWKV_N
    s0 = jnp.sum(jnp.where(lo, x, 0.0), axis=-1, keepdims=True)
    s1 = jnp.sum(jnp.where(lo, 0.0, x), axis=-1, keepdims=True)
    return jnp.where(lo, s0, s1)


def _rwkv_prep_kernel(r_ref, k_ref, v_ref, lora_ref, ap_ref, w0_ref, a0_ref, kk_w_ref, ka_w_ref, rk_w_ref,
                      kk_ref, kb_ref, k2_ref, bonus_ref, lw_ref):
    d = D_MODEL
    for j in range(d // LANES):
        sl = slice(j * LANES, (j + 1) * LANES)
        r, k, v = r_ref[:, sl], k_ref[:, sl], v_ref[:, sl]
        a = _sigmoid(a0_ref[:, sl] + ap_ref[:, sl])
        kk = k * kk_w_ref[:, sl]
        kk = kk / jnp.maximum(jnp.sqrt(_segsum(kk * kk)), 1e-12)
        k2 = k * (1.0 + (a - 1.0) * ka_w_ref[:, sl])
        kk_ref[:, sl] = kk
        kb_ref[:, sl] = kk * a
        k2_ref[:, sl] = k2
        bonus_ref[:, sl] = _segsum(r * k2 * rk_w_ref[:, sl]) * v
        for z in range(2):
            lw_ref[z, :, sl] = -jnp.exp(_logsig(w0_ref[z:z + 1, sl] + lora_ref[:, z * d + j * LANES:z * d + (j + 1) * LANES]) - 0.5)


def _rwkv_prep(r, k, v, lora, a_pre, w0, a0, k_k, k_a, r_k, *, tm=256):
    m, d = r.shape
    tm = min(tm, m)
    row = lambda i: (i, 0)
    tok = pl.BlockSpec((tm, d), row)
    par = pl.BlockSpec((1, d), lambda i: (0, 0))
    return _call(_rwkv_prep_kernel, grid=(m // tm,),
                 in_specs=[tok, tok, tok, pl.BlockSpec((tm, 2 * d), row), tok,
                           pl.BlockSpec((2, d), lambda i: (0, 0)), par, par, par, par],
                 out_specs=[tok, tok, tok, tok, pl.BlockSpec((2, tm, d), lambda i: (0, i, 0))],
                 out_shape=[jax.ShapeDtypeStruct((m, d), F32)] * 4 + [jax.ShapeDtypeStruct((2, m, d), F32)],
                 sem=("parallel",), name="rwkv_prep")(
                     r, k, v, lora, a_pre, w0, a0.reshape(1, d), k_k.reshape(1, d), k_a.reshape(1, d), r_k.reshape(1, d))


def _rwkv_scan_kernel(rf, kf, vf, kkf, kbf, lwf, rb, kb_, vb, kkb, kbb, lwb, s0_ref, yf_ref, yb_ref, sn_ref, s_s):
    L, pairs = RWKV_CHUNK, RWKV_PAIRS
    c = pl.program_id(1)

    @pl.when(c == 0)
    def _():
        s_s[...] = s0_ref[...]

    row = lax.broadcasted_iota(jnp.int32, (L, LANES), 0)
    lane = lax.broadcasted_iota(jnp.int32, (L, LANES), 1)
    lane_tok = jnp.bitwise_and(lane, L - 1)
    lane_head0 = lane - lane_tok
    lo = lane < RWKV_N
    sq_r = lax.broadcasted_iota(jnp.int32, (LANES, LANES), 0) < RWKV_N
    sq_c = lax.broadcasted_iota(jnp.int32, (LANES, LANES), 1) < RWKV_N
    same_head = sq_r == sq_c
    zeros64 = jnp.zeros((2 * L, LANES), F32)
    eye_cat = jnp.where(lane_tok == row, 1.0, 0.0)

    def head_rows(a):
        return jnp.concatenate([jnp.where(lo, a, 0.0), jnp.where(lo, 0.0, a)], axis=0)

    def group_rows(a0, a1):
        return jnp.concatenate([jnp.concatenate([head_rows(a0), zeros64], axis=1),
                                jnp.concatenate([zeros64, head_rows(a1)], axis=1)], axis=0)

    refs = ((rf, kf, vf, kkf, kbf, lwf, yf_ref), (rb, kb_, vb, kkb, kbb, lwb, yb_ref))
    groups = [(i, d, grp) for i in range(s_s.shape[0]) for d in range(2) for grp in range(pairs // 2)]
    st = {}
    for i, d, grp in groups:
        r_ref, k_ref, v_ref, kk_ref, kb_ref, lw_ref, _ = refs[d]
        tri = _tri(L, "lower" if d == 0 else "upper")
        last = L - 1 if d == 0 else 0
        rt, kt, kd, bd, v, e_last, m_b, m_k = [], [], [], [], [], [], 0.0, 0.0
        for q in range(2):
            sl = slice((2 * grp + q) * LANES, (2 * grp + q + 1) * LANES)
            r, k, kk, kb, lw = r_ref[i, :, sl], k_ref[i, :, sl], kk_ref[i, :, sl], kb_ref[i, :, sl], lw_ref[0, i, :, sl]
            cum = _dot_exact_lhs(tri, lw)
            c_last = cum[last:last + 1, :]
            e_neg, e_end = jnp.exp(-cum), jnp.exp(c_last - cum)
            rt.append(r * jnp.exp(cum))
            kt.append(kk * jnp.exp(cum - lw))
            kd.append(k * e_end)
            bd.append(kb * e_end)
            v.append(v_ref[i, :, sl])
            e_last.append(jnp.exp(c_last))
            lhs = jnp.concatenate([kt[q], rt[q]], axis=0)
            pad = lambda a: jnp.concatenate([head_rows(a), zeros64] if q == 0 else [zeros64, head_rows(a)], axis=0)
            m_b = m_b + _dot_nt(lhs, pad(kb * e_neg))
            m_k = m_k + _dot_nt(lhs, pad(k * e_neg))
        st[i, d, grp] = dict(rt=rt, kt=kt, kd=kd, bd=bd, v=v, e_last=e_last, m_b=m_b, m_k=m_k)
    for i, d, grp in groups:
        z = st[i, d, grp]
        strict = (lane_tok < row) if d == 0 else (lane_tok > row)
        incl = (lane_tok <= row) if d == 0 else (lane_tok >= row)
        m_b, m_k = z["m_b"], z["m_k"]
        z["n_kb"] = jnp.where(strict, m_b[:L], 0.0)
        z["m_rb"] = jnp.where(incl, m_b[L:], 0.0)
        m_kk_rk = jnp.concatenate([jnp.where(strict, m_k[:L], 0.0), jnp.where(incl, m_k[L:], 0.0)], axis=0)
        z["kv"] = _dot(m_kk_rk, group_rows(z["v"][0], z["v"][1]))
        z["x"] = eye_cat
    for step in range(L - 1):
        for i, d, grp in groups:
            z = st[i, d, grp]
            u = step if d == 0 else L - 1 - step
            coef = jnp.take_along_axis(z["n_kb"], lane_head0 + u, axis=1)
            z["x"] = z["x"] - coef * z["x"][u:u + 1, :]
    for i, d, grp in groups:
        z = st[i, d, grp]
        x, kt, kv = z["x"], z["kt"], z["kv"]
        x_hi = x.astype(BF16)
        x_lo = x - x_hi.astype(F32)
        rhs = jnp.concatenate([group_rows(kt[0], kt[1]), group_rows(kv[:L, :LANES], kv[:L, LANES:])], axis=1)
        z["w"] = _dot(x_hi, rhs) + _dot(x_lo, rhs)
    for i, d, grp in groups:
        z = st[i, d, grp]
        w = z["w"]
        z["sa"] = [_dot_nt(w[:, q * LANES:(q + 1) * LANES], s_s[i, d, 2 * grp + q]) + w[:, (2 + q) * LANES:(3 + q) * LANES]
                   for q in range(2)]
    for i, d, grp in groups:
        z = st[i, d, grp]
        rb_sa = _dot(z["m_rb"], group_rows(z["sa"][0], z["sa"][1]))
        for q in range(2):
            p = 2 * grp + q
            cols = slice(q * LANES, (q + 1) * LANES)
            refs[d][6][i, :, p * LANES:(p + 1) * LANES] = _dot_nt(z["rt"][q], s_s[i, d, p]) + z["kv"][L:, cols] - rb_sa[:, cols]
    for i, d, grp in groups:
        z = st[i, d, grp]
        for q in range(2):
            p = 2 * grp + q
            s_new = s_s[i, d, p] * z["e_last"][q] + _dot_tn(z["v"][q], z["kd"][q]) - _dot_tn(z["sa"][q], z["bd"][q])
            s_s[i, d, p] = jnp.where(same_head, s_new, 0.0)

    @pl.when(c == pl.num_programs(1) - 1)
    def _():
        sn_ref[...] = s_s[...]


def _rwkv_scan(r, k2, v, kk, kb, lw, s0, *, nb=SCAN_BATCH_ROWS):
    b, t_len, d = r.shape
    L, pairs = RWKV_CHUNK, RWKV_PAIRS
    nc = t_len // L
    fwd = pl.BlockSpec((nb, L, d), lambda i, c: (i, c, 0))
    bwd = pl.BlockSpec((nb, L, d), lambda i, c: (i, nc - 1 - c, 0))
    st = (2, pairs, LANES, LANES)
    state = pl.BlockSpec((nb,) + st, lambda i, c: (i, 0, 0, 0, 0))
    return _call(
        _rwkv_scan_kernel, grid=(b // nb, nc),
        in_specs=[fwd] * 5 + [pl.BlockSpec((1, nb, L, d), lambda i, c: (0, i, c, 0))]
        + [bwd] * 5 + [pl.BlockSpec((1, nb, L, d), lambda i, c: (1, i, nc - 1 - c, 0)), state],
        out_specs=[fwd, bwd, state],
        out_shape=[jax.ShapeDtypeStruct((b, t_len, d), F32)] * 2 + [jax.ShapeDtypeStruct((b,) + st, F32)],
        scratch=[pltpu.VMEM((nb,) + st, F32)], sem=("parallel", "arbitrary"), name="rwkv_scan")(
            r, k2, v, kk, kb, lw, r, k2, v, kk, kb, lw, s0)


def _rwkv_post_kernel(yf_ref, yb_ref, bonus_ref, g_ref, lng_ref, lnb_ref, out_ref):
    for j in range(D_MODEL // LANES):
        sl = slice(j * LANES, (j + 1) * LANES)
        y = yf_ref[:, sl] + yb_ref[:, sl]
        y = y - _segsum(y) * (1.0 / RWKV_N)
        y = y * lax.rsqrt(_segsum(y * y) * (1.0 / RWKV_N) + RWKV_GN_EPS)
        out_ref[:, sl] = ((y * lng_ref[:, sl] + lnb_ref[:, sl] + bonus_ref[:, sl]) * g_ref[:, sl]).astype(BF16)


def _rwkv_post(yf, yb, bonus, g, ln_g, ln_b, *, tm=256):
    m, d = yf.shape
    tm = min(tm, m)
    tok = pl.BlockSpec((tm, d), lambda i: (i, 0))
    par = pl.BlockSpec((1, d), lambda i: (0, 0))
    return _call(_rwkv_post_kernel, grid=(m // tm,), in_specs=[tok, tok, tok, tok, par, par], out_specs=tok,
                 out_shape=jax.ShapeDtypeStruct((m, d), BF16), sem=("parallel",), name="rwkv_post")(
                     yf, yb, bonus, g, ln_g.reshape(1, d), ln_b.reshape(1, d))


def _pad_cols(w, n):
    return jnp.zeros(w.shape[:-1] + (n,), w.dtype).at[..., :w.shape[-1]].set(w)


def _pad_rows(w, n):
    return jnp.zeros((n,) + w.shape[1:], w.dtype).at[:w.shape[0]].set(w)


def _rwkv_mixer(h, b, init, j, mu, w_rkv, w0, w1, w2, a0, a1, a2, g1, g2, k_k, k_a, r_k, ln_g, ln_b, w_out, x, gate1):
    m, d = h.shape
    t_len = m // b
    xmix = _rwkv_mix(h.reshape(b, t_len, d), mu[j]).reshape(6, m, d)
    proj = lambda idx, w, wlead, name, **kw: _mm(xmix, w, xlead=(idx,), wlead=wlead, tm=512, tn=1024, name=name, **kw)
    r = proj(0, w_rkv, (j, 0), "rwkv_r")
    k = proj(2, w_rkv, (j, 1), "rwkv_k")
    v = proj(3, w_rkv, (j, 2), "rwkv_v")
    rank_w = w1.shape[-1]
    w1cat = jnp.concatenate([w1[j, 0], w1[j, 1]], axis=1)
    w2bd = jnp.zeros((2 * rank_w, 2 * d), F32).at[:rank_w, :d].set(w2[j, 0]).at[rank_w:, d:].set(w2[j, 1])
    tw = proj(1, w1cat, (), "rwkv_w1", epilogue=jnp.tanh)
    lora = _mm(tw, w2bd, tm=512, tn=1024, name="rwkv_w2")
    a_pre = _mm(proj(4, _pad_cols(a1[j], LANES), (), "rwkv_a1"), _pad_rows(a2[j], LANES), tm=512, tn=1024, name="rwkv_a2")
    g = _mm(proj(5, g1[j], (), "rwkv_g1", epilogue=_sigmoid), g2[j], tm=512, tn=1024, name="rwkv_g2")
    kk, kb, k2, bonus, lw = _rwkv_prep(r, k, v, lora, a_pre, w0[j], a0[j], k_k[j], k_a[j], r_k[j].reshape(d))
    s3 = lambda z: z.reshape(b, t_len, d)
    yf, yb, sn = _rwkv_scan(s3(r), s3(k2), s3(v), s3(kk), s3(kb), lw.reshape(2, b, t_len, d), init)
    y = _rwkv_post(yf.reshape(m, d), yb.reshape(m, d), bonus, g, ln_g[j], ln_b[j])
    x = _mm(y, w_out, wlead=(j,), tm=512, tn=1024, extras=[x, gate1],
            epilogue=_residual_epilogue, name="rwkv_out")
    return x, sn


def _rwkv_state_to_pairs(s):
    b = s.shape[0]
    n = RWKV_N
    out = jnp.zeros((b, 2, RWKV_PAIRS, 2 * n, 2 * n), F32)
    return out.at[..., :n, :n].set(s[:, :, 0::2]).at[..., n:, n:].set(s[:, :, 1::2])


def _rwkv_state_from_pairs(sp):
    n = RWKV_N
    b = sp.shape[0]
    return jnp.stack([sp[..., :n, :n], sp[..., n:, n:]], axis=3).reshape(b, 2, RWKV_HEADS, n, n)


def _router_kernel(x_ref, w_ref, b_ref, o_ref, cnt_ref, seen, *, tiles_per_block):
    e0, per = ROUTER_EXPERT_LANE0, MOE_PER_GROUP

    @pl.when(pl.program_id(0) % tiles_per_block == 0)
    def _():
        seen[...] = jnp.zeros_like(seen)

    logits = _dot_3pass(x_ref[...], w_ref[...]) + b_ref[...]
    lane = lax.broadcasted_iota(jnp.int32, logits.shape, 1).astype(F32)
    big = float(LANES)
    rmax = lambda z: jnp.max(z, axis=-1, keepdims=True)
    rsum = lambda z: jnp.sum(z, axis=-1, keepdims=True)
    first = lambda m: jnp.min(jnp.where(m, lane, big), axis=-1, keepdims=True)
    is_g = lane < MOE_GROUPS
    gmax = rmax(jnp.where(is_g, logits, -jnp.inf))
    gsel = first(jnp.logical_and(is_g, logits >= gmax))
    gw = 1.0 / rsum(jnp.where(is_g, jnp.exp(logits - gmax), 0.0))
    in_grp = jnp.logical_and(lane >= e0 + gsel * per, lane < e0 + (gsel + 1) * per)
    emax = rmax(jnp.where(in_grp, logits, -jnp.inf))
    p = jnp.where(in_grp, jnp.exp(logits - emax), 0.0)
    p = p / rsum(p)
    p1 = rmax(jnp.where(in_grp, p, -1.0))
    i1 = first(jnp.logical_and(in_grp, p >= p1))
    rest = jnp.logical_and(in_grp, lane != i1)
    p2 = rmax(jnp.where(rest, p, -1.0))
    i2 = first(jnp.logical_and(rest, p >= p2))
    tot = p1 + p2
    comb = jnp.where(lane == i1, p1 / tot * gw, jnp.where(lane == i2, p2 / tot * gw, 0.0))
    onehot = jnp.where(lane == gsel, 1.0, 0.0)
    csum = seen[...] + jnp.dot(_tri(onehot.shape[0], "lower").astype(BF16), onehot.astype(BF16), preferred_element_type=F32)
    rank = rsum(onehot * csum) - 1.0
    seen[...] = csum[onehot.shape[0] - 1:, :]
    cnt_ref[0] = seen[...]
    o_ref[...] = jnp.where(lane == ROUTER_GROUP_LANE, gsel, jnp.where(lane == ROUTER_RANK_LANE, rank, comb))


def _router(x, w_group, b_group, w_expert, b_expert, *, tm=512):
    m, d = x.shape
    tm = min(tm, m)
    tiles_per_block = min(MOE_BLOCK_TOKENS, m) // tm
    e0 = ROUTER_EXPERT_LANE0
    w = jnp.zeros((d, LANES), F32).at[:, :MOE_GROUPS].set(w_group).at[:, e0:e0 + MOE_EXPERTS].set(w_expert)
    bias = jnp.zeros((1, LANES), F32).at[0, :MOE_GROUPS].set(b_group).at[0, e0:e0 + MOE_EXPERTS].set(b_expert)
    return _call(functools.partial(_router_kernel, tiles_per_block=tiles_per_block), grid=(m // tm,),
                 in_specs=[pl.BlockSpec((tm, d), lambda i: (i, 0)), pl.BlockSpec((d, LANES), lambda i: (0, 0)),
                           pl.BlockSpec((1, LANES), lambda i: (0, 0))],
                 out_specs=[pl.BlockSpec((tm, LANES), lambda i: (i, 0)),
                            pl.BlockSpec((1, 1, LANES), lambda i: (i // tiles_per_block, 0, 0))],
                 out_shape=[jax.ShapeDtypeStruct((m, LANES), F32),
                            jax.ShapeDtypeStruct((m // tm // tiles_per_block, 1, LANES), F32)],
                 scratch=[pltpu.VMEM((1, LANES), F32)], sem=("arbitrary",), name="moe_router")(x, w, bias)


def _moe_dispatch(comb, counts):
    n = comb.shape[0]
    tb, tr = min(MOE_BLOCK_TOKENS, n), MOE_ROW_TILE
    nblk = n // tb
    cnt = counts[:, 0, :MOE_GROUPS].astype(jnp.int32)
    padded = (cnt + tr - 1) // tr * tr
    start = jnp.cumsum(padded, axis=1) - padded
    grp = comb[:, ROUTER_GROUP_LANE].astype(jnp.int32).reshape(nblk, tb)
    rank = comb[:, ROUTER_RANK_LANE].astype(jnp.int32).reshape(nblk, tb)
    onehot = grp[..., None] == jnp.arange(MOE_GROUPS, dtype=jnp.int32)
    pos = jnp.sum(jnp.where(onehot, start[:, None, :], 0), axis=-1) + rank
    meta = jnp.concatenate([start, padded // tr], axis=1).astype(jnp.int32)
    return pos.reshape(-1), meta.reshape(-1)


def _moe_sparse_kernel(pos_ref, meta_ref, h_ref, comb_ref, wg_ref, wu_ref, wd_ref, y_ref, xs, cs, acc):
    blk, e = pl.program_id(0), pl.program_id(1)
    tr, per = MOE_ROW_TILE, MOE_PER_GROUP
    tb = h_ref.shape[0]
    g, i = e // per, e % per
    base, mbase = blk * tb, blk * 2 * MOE_GROUPS

    @pl.when(jnp.logical_and(blk == 0, e == 0))
    def _():
        xs[...] = jnp.zeros_like(xs)
        cs[...] = jnp.zeros_like(cs)

    @pl.when(e == 0)
    def _():
        def body(j, carry):
            t0 = pl.multiple_of(j * 8, 8)
            rows8, comb8 = h_ref[pl.ds(t0, 8), :], comb_ref[pl.ds(t0, 8), :]
            for u in range(8):
                r = pos_ref[base + t0 + u]
                xs[pl.ds(r, 1), :] = rows8[u:u + 1, :]
                cs[pl.ds(r, 1), :] = comb8[u:u + 1, :]
            return carry

        lax.fori_loop(0, tb // 8, body, 0)

    lo, n_tiles = meta_ref[mbase + g], meta_ref[mbase + MOE_GROUPS + g]
    lane = lax.broadcasted_iota(jnp.int32, (tr, LANES), 1)

    def tile_body(t, carry):
        rows = pl.ds(pl.multiple_of(lo + t * tr, tr), tr)
        x = xs[rows, :]
        ce = jnp.sum(jnp.where(lane == e + ROUTER_EXPERT_LANE0, cs[rows, :], 0.0), axis=-1, keepdims=True)
        hid = _silu(_dot(x, wg_ref[0, 0])) * _dot(x, wu_ref[0, 0])
        contrib = _dot(hid * ce, wd_ref[0, 0])

        @pl.when(i == 0)
        def _():
            acc[rows, :] = contrib

        @pl.when(i != 0)
        def _():
            acc[rows, :] += contrib

        return carry

    lax.fori_loop(0, n_tiles, tile_body, 0)

    @pl.when(e == pl.num_programs(1) - 1)
    def _():
        def body(j, carry):
            t0 = pl.multiple_of(j * 8, 8)
            for u in range(8):
                y_ref[pl.ds(t0 + u, 1), :] = acc[pl.ds(pos_ref[base + t0 + u], 1), :]
            return carry

        lax.fori_loop(0, tb // 8, body, 0)


def _moe_sparse(h, comb, counts, w_gate, w_up, w_down, layer):
    n, d = h.shape
    f = w_gate.shape[-1]
    tb, tr = min(MOE_BLOCK_TOKENS, n), MOE_ROW_TILE
    nblk, r_blk = n // tb, tb + MOE_GROUPS * tr
    tok, meta = _moe_dispatch(comb, counts)
    wspec = lambda shape: pl.BlockSpec((1, 1) + shape, lambda b, e, *_: (layer, e, 0, 0))
    once = pl.Buffered(1)
    grid_spec = pltpu.PrefetchScalarGridSpec(
        num_scalar_prefetch=2, grid=(nblk, MOE_EXPERTS),
        in_specs=[pl.BlockSpec((tb, d), lambda b, e, *_: (b, 0), pipeline_mode=once),
                  pl.BlockSpec((tb, LANES), lambda b, e, *_: (b, 0), pipeline_mode=once),
                  wspec((d, f)), wspec((d, f)), wspec((f, d))],
        out_specs=pl.BlockSpec((tb, d), lambda b, e, *_: (b, 0), pipeline_mode=once),
        scratch_shapes=[pltpu.VMEM((r_blk, d), F32), pltpu.VMEM((r_blk, LANES), F32), pltpu.VMEM((r_blk, d), F32)])
    return pl.pallas_call(
        _moe_sparse_kernel, grid_spec=grid_spec, out_shape=jax.ShapeDtypeStruct((n, d), F32), name="moe_sparse",
        compiler_params=pltpu.CompilerParams(dimension_semantics=("arbitrary", "arbitrary"),
                                             vmem_limit_bytes=VMEM_LIMIT_BYTES))(tok, meta, h, comb, w_gate, w_up, w_down)


def _run_trunk(x3, mods, rows, init_a, init_b, init_c, p):
    b, t_len, d = x3.shape
    m = b * t_len
    x = x3.reshape(m, d)
    new_a, new_b, new_c = [], [], []
    resid = None
    for i in range(p["ada_w"].shape[0]):
        j = i // 3
        shift1, scale1, gate1, shift2, scale2, gate2 = mods[i]
        h = _normmod(x, p["norm1_g"][i], scale1, shift1, resid=resid, out_dtype=F32 if i % 3 == 2 else BF16)
        if resid is not None:
            x, h = h
        if i % 3 == 0:
            x, st = _mlstm_mixer(h, b, rows, tuple(s[:, j] for s in init_a), p["mlstm_w_in"][j], p["mlstm_conv_w"][j],
                                 p["mlstm_w_gate"][j], p["mlstm_b_gate"][j], p["mlstm_norm_g"][j], p["mlstm_w_out"][j],
                                 x, gate1)
            new_a.append(st)
        elif i % 3 == 1:
            x, st = _gla_mixer(h, b, rows, tuple(s[:, j] for s in init_b), p["gla_w_in"][j], p["gla_conv_w"][j],
                               p["gla_w_a1"][j], p["gla_w_a2"][j], p["gla_b_a"][j], p["gla_norm_g"][j], p["gla_w_out"][j],
                               x, gate1)
            new_b.append(st)
        else:
            s0 = (jnp.zeros((b, 2, RWKV_PAIRS, LANES, LANES), F32) if init_c is None
                  else _rwkv_state_to_pairs(init_c[0][:, j]))
            x, st = _rwkv_mixer(h, b, s0, j,p["rwkv_mu"], p["rwkv_w_rkv"], p["rwkv_w0"],
                                p["rwkv_w1"], p["rwkv_w2"], p["rwkv_a0"], p["rwkv_a1"], p["rwkv_a2"], p["rwkv_g1"],
                                p["rwkv_g2"], p["rwkv_k_k"], p["rwkv_k_a"], p["rwkv_r_k"], p["rwkv_ln_g"], p["rwkv_ln_b"],
                                p["rwkv_w_out"], x, gate1)
            new_c.append((_rwkv_state_from_pairs(st),))
        hff = _normmod(x, p["norm2_g"][i], scale2, shift2)
        comb, counts = _router(hff, p["moe_w_group"][i], p["moe_b_group"][i], p["moe_w_expert"][i], p["moe_b_expert"][i])
        resid = (_moe_sparse(hff, comb, counts, p["moe_w_gate"], p["moe_w_up"], p["moe_w_down"], i), gate2)
    stack = lambda per_layer: tuple(jnp.stack(parts, axis=1) for parts in zip(*per_layer))
    y = _resid_rmsnorm(x, resid[0], resid[1], p["final_norm_g"]).reshape(b, t_len, d)
    return y, stack(new_a), stack(new_b), stack(new_c)


def kernel(x_prompt, x_sample, state_mlstm_C, state_mlstm_n, state_mlstm_m, state_gla_S, state_rwkv_S, c, c_ctx, ada_w, ada_b, norm1_g, norm2_g, moe_w_group, moe_b_group, moe_w_expert, moe_b_expert, moe_w_gate, moe_w_up, moe_w_down, final_norm_g, mlstm_w_in, mlstm_conv_w, mlstm_w_gate, mlstm_b_gate, mlstm_norm_g, mlstm_w_out, gla_w_in, gla_conv_w, gla_w_a1, gla_w_a2, gla_b_a, gla_norm_g, gla_w_out, rwkv_mu, rwkv_w_rkv, rwkv_w0, rwkv_w1, rwkv_w2, rwkv_a0, rwkv_a1, rwkv_a2, rwkv_g1, rwkv_g2, rwkv_k_k, rwkv_k_a, rwkv_r_k, rwkv_ln_g, rwkv_ln_b, rwkv_w_out):
    p = dict(ada_w=ada_w, norm1_g=norm1_g, norm2_g=norm2_g, moe_w_group=moe_w_group, moe_b_group=moe_b_group,
             moe_w_expert=moe_w_expert, moe_b_expert=moe_b_expert, moe_w_gate=moe_w_gate, moe_w_up=moe_w_up,
             moe_w_down=moe_w_down, final_norm_g=final_norm_g, mlstm_w_in=mlstm_w_in, mlstm_conv_w=mlstm_conv_w,
             mlstm_w_gate=mlstm_w_gate, mlstm_b_gate=mlstm_b_gate, mlstm_norm_g=mlstm_norm_g, mlstm_w_out=mlstm_w_out,
             gla_w_in=gla_w_in, gla_conv_w=gla_conv_w, gla_w_a1=gla_w_a1, gla_w_a2=gla_w_a2, gla_b_a=gla_b_a,
             gla_norm_g=gla_norm_g, gla_w_out=gla_w_out, rwkv_mu=rwkv_mu, rwkv_w_rkv=rwkv_w_rkv, rwkv_w0=rwkv_w0,
             rwkv_w1=rwkv_w1, rwkv_w2=rwkv_w2, rwkv_a0=rwkv_a0, rwkv_a1=rwkv_a1, rwkv_a2=rwkv_a2, rwkv_g1=rwkv_g1,
             rwkv_g2=rwkv_g2, rwkv_k_k=rwkv_k_k, rwkv_k_a=rwkv_k_a, rwkv_r_k=rwkv_r_k, rwkv_ln_g=rwkv_ln_g,
             rwkv_ln_b=rwkv_ln_b, rwkv_w_out=rwkv_w_out)
    for name in ("moe_w_gate", "moe_w_up", "moe_w_down", "mlstm_w_in", "mlstm_w_out", "gla_w_in", "gla_w_out",
                 "rwkv_w_rkv", "rwkv_w_out"):
        p[name] = p[name].astype(BF16)
    depth, d = ada_w.shape[0], ada_w.shape[1]
    n_dec = c.shape[0]
    cond8 = jnp.zeros((8, d), F32).at[0].set(c_ctx).at[1:1 + n_dec].set(c)
    mod = _ada(cond8, ada_w, ada_b)
    split = lambda rows: [tuple(rows[i][:, None, k * d:(k + 1) * d] for k in range(6)) for i in range(depth)]
    mods_ctx = split(mod[:, 0:1])
    mods_dec = split(mod[:, 1:1 + n_dec])

    bp = x_prompt.shape[0]
    zeros_like_ctx = lambda s: jnp.zeros((bp,) + s.shape[1:], F32)
    y_prompt, st_a, st_b, st_c = _run_trunk(
        x_prompt, mods_ctx, 1, tuple(zeros_like_ctx(s) for s in (state_mlstm_C, state_mlstm_n, state_mlstm_m)),
        (zeros_like_ctx(state_gla_S),), None, p)
    y_sample = _run_trunk(x_sample, mods_dec, x_sample.shape[1] // GRID_W,
                          (state_mlstm_C, state_mlstm_n, state_mlstm_m), (state_gla_S,), (state_rwkv_S,), p)[0]
    return (y_prompt, y_sample, st_a[0], st_a[1], st_a[2], st_b[0], st_c[0])
```

```python
import functools

import jax
import jax.numpy as jnp
from jax import lax
from jax.experimental import pallas as pl
from jax.experimental.pallas import tpu as pltpu

F32 = jnp.float32
BF16 = jnp.bfloat16

D_MODEL = 1024
GRID_W = 64
NORM_EPS = 1e-6

MLSTM_HEADS = 4
MLSTM_DK = 128
MLSTM_DV = 256
MLSTM_CHUNK = 128

GLA_HEADS = 4
GLA_DK = 128
GLA_DV = 256
GLA_TAU = 16.0
GLA_CHUNK = 16
GLA_BLOCK = 32

RWKV_N = 64
RWKV_HEADS = 16
RWKV_PAIRS = RWKV_HEADS // 2
RWKV_GN_EPS = 64e-5
RWKV_CHUNK = 32

MOE_GROUPS = 4
MOE_PER_GROUP = 4
MOE_EXPERTS = 16
ROUTER_EXPERT_LANE0 = 8
ROUTER_GROUP_LANE = 127
ROUTER_RANK_LANE = 126
MOE_BLOCK_TOKENS = 2048
MOE_ROW_TILE = 256

CONV_BLOCK_BYTES = 2 * 1024 * 1024
SCAN_BATCH_ROWS = 2

LANES = 128
VMEM_LIMIT_BYTES = 56 * 1024 * 1024


def _call(kernel, *, grid, in_specs, out_specs, out_shape, scratch=(), sem, name):
    return pl.pallas_call(
        kernel, grid=grid, in_specs=in_specs, out_specs=out_specs, out_shape=out_shape,
        scratch_shapes=list(scratch), name=name,
        compiler_params=pltpu.CompilerParams(dimension_semantics=sem, vmem_limit_bytes=VMEM_LIMIT_BYTES))


def _sigmoid(x):
    return 1.0 / (1.0 + jnp.exp(-x))


def _silu(x):
    return x * _sigmoid(x)


def _logsig(x):
    return jnp.minimum(x, 0.0) - jnp.log1p(jnp.exp(-jnp.abs(x)))


def _dot(a, b):
    return jnp.dot(a.astype(BF16), b.astype(BF16), preferred_element_type=F32)


def _dot_nt(a, b):
    return lax.dot_general(a.astype(BF16), b.astype(BF16), (((1,), (1,)), ((), ())), preferred_element_type=F32)


def _dot_tn(a, b):
    return lax.dot_general(a.astype(BF16), b.astype(BF16), (((0,), (0,)), ((), ())), preferred_element_type=F32)


def _split_bf16(x, parts):
    out = []
    for _ in range(parts):
        piece = x.astype(BF16)
        out.append(piece)
        x = x - piece.astype(F32)
    return out


def _dot_exact_lhs(a, b):
    a = a.astype(BF16)
    return sum(jnp.dot(a, piece, preferred_element_type=F32) for piece in _split_bf16(b, 3))


def _dot_3pass(a, b):
    (a_hi, a_lo), (b_hi, b_lo) = _split_bf16(a, 2), _split_bf16(b, 2)
    dot = lambda p, q: jnp.dot(p, q, preferred_element_type=F32)
    return dot(a_hi, b_hi) + (dot(a_hi, b_lo) + dot(a_lo, b_hi))


def _tri(n, kind, block=None):
    r = lax.broadcasted_iota(jnp.int32, (n, n), 0)
    c = lax.broadcasted_iota(jnp.int32, (n, n), 1)
    m = {"lower": c <= r, "upper": c >= r, "all": c >= 0}[kind]
    if block is not None:
        sh = block.bit_length() - 1
        m = jnp.logical_and(m, jnp.right_shift(r, sh) == jnp.right_shift(c, sh))
    return jnp.where(m, 1.0, 0.0).astype(F32)


def _mm(x, w, *, tm, tn, xlead=(), wlead=(), extras=(), epilogue=None, out_dtype=F32, name):
    m, k = x.shape[-2:]
    n = w.shape[-1]
    tm, tn = min(tm, m), min(tn, n)

    def kern(x_ref, w_ref, *rest):
        acc = _dot(x_ref[...], w_ref[...])
        if epilogue is not None:
            acc = epilogue(acc, *[r[...] for r in rest[:-1]])
        rest[-1][...] = acc.astype(out_dtype)

    def extra_spec(arr):
        if arr.ndim == 2:
            return pl.BlockSpec((tm, tn), lambda i, j: (i, j))
        tiles_per_row = (m // arr.shape[0]) // tm
        return pl.BlockSpec((1, 1, tn), lambda i, j: (i // tiles_per_row, 0, j))

    in_specs = [
        pl.BlockSpec((None,) * len(xlead) + (tm, k), lambda i, j: tuple(xlead) + (i, 0)),
        pl.BlockSpec((None,) * len(wlead) + (k, tn), lambda i, j: tuple(wlead) + (0, j)),
    ] + [extra_spec(a) for a in extras]
    return _call(kern, grid=(m // tm, n // tn), in_specs=in_specs,
                 out_specs=pl.BlockSpec((tm, tn), lambda i, j: (i, j)),
                 out_shape=jax.ShapeDtypeStruct((m, n), out_dtype),
                 sem=("parallel", "parallel"), name=name)(x, w, *extras)


def _residual_epilogue(acc, x, gate):
    return x + gate[0] * acc


def _ada_kernel(c_ref, w_ref, b_ref, o_ref):
    o_ref[0] = _dot(_silu(c_ref[...]), w_ref[0]) + b_ref[0]


def _ada(cond8, ada_w, ada_b):
    depth, d, n = ada_w.shape
    tn = 1536
    return _call(_ada_kernel, grid=(depth, n // tn),
                 in_specs=[pl.BlockSpec((8, d), lambda l, j: (0, 0)),
                           pl.BlockSpec((1, d, tn), lambda l, j: (l, 0, j)),
                           pl.BlockSpec((1, 1, tn), lambda l, j: (l, 0, j))],
                 out_specs=pl.BlockSpec((1, 8, tn), lambda l, j: (l, 0, j)),
                 out_shape=jax.ShapeDtypeStruct((depth, 8, n), F32),
                 sem=("parallel", "parallel"), name="ada")(cond8, ada_w, ada_b.reshape(depth, 1, n))


def _normmod_kernel(x_ref, g_ref, sc_ref, sh_ref, o_ref):
    x = x_ref[...]
    y = x * lax.rsqrt(jnp.mean(x * x, axis=-1, keepdims=True) + NORM_EPS)
    o_ref[...] = ((y * g_ref[...]) * (1.0 + sc_ref[0]) + sh_ref[0]).astype(o_ref.dtype)


def _resid_normmod_kernel(x_ref, y_ref, gate_ref, g_ref, sc_ref, sh_ref, xo_ref, o_ref):
    x = x_ref[...] + gate_ref[0] * y_ref[...]
    xo_ref[...] = x
    y = x * lax.rsqrt(jnp.mean(x * x, axis=-1, keepdims=True) + NORM_EPS)
    o_ref[...] = ((y * g_ref[...]) * (1.0 + sc_ref[0]) + sh_ref[0]).astype(o_ref.dtype)


def _normmod(x, g, scale, shift, *, resid=None, out_dtype=F32, tm=512):
    m, d = x.shape
    tm = min(tm, m)
    tiles_per_row = (m // scale.shape[0]) // tm
    mod = pl.BlockSpec((1, 1, d), lambda i: (i // tiles_per_row, 0, 0))
    tok = pl.BlockSpec((tm, d), lambda i: (i, 0))
    par = pl.BlockSpec((1, d), lambda i: (0, 0))
    if resid is None:
        return _call(_normmod_kernel, grid=(m // tm,), in_specs=[tok, par, mod, mod], out_specs=tok,
                     out_shape=jax.ShapeDtypeStruct((m, d), out_dtype), sem=("parallel",), name="normmod")(
                         x, g.reshape(1, d), scale, shift)
    y, gate = resid
    return _call(_resid_normmod_kernel, grid=(m // tm,), in_specs=[tok, tok, mod, par, mod, mod], out_specs=[tok, tok],
                 out_shape=[jax.ShapeDtypeStruct((m, d), F32), jax.ShapeDtypeStruct((m, d), out_dtype)],
                 sem=("parallel",), name="resid_normmod")(x, y, gate, g.reshape(1, d), scale, shift)


def _resid_rmsnorm_kernel(x_ref, y_ref, gate_ref, g_ref, o_ref):
    x = x_ref[...] + gate_ref[0] * y_ref[...]
    o_ref[...] = x * lax.rsqrt(jnp.mean(x * x, axis=-1, keepdims=True) + NORM_EPS) * g_ref[...]


def _resid_rmsnorm(x, y, gate, g, *, tm=512):
    m, d = x.shape
    tm = min(tm, m)
    tiles_per_row = (m // gate.shape[0]) // tm
    tok = pl.BlockSpec((tm, d), lambda i: (i, 0))
    return _call(_resid_rmsnorm_kernel, grid=(m // tm,),
                 in_specs=[tok, tok, pl.BlockSpec((1, 1, d), lambda i: (i // tiles_per_row, 0, 0)),
                           pl.BlockSpec((1, d), lambda i: (0, 0))],
                 out_specs=tok, out_shape=jax.ShapeDtypeStruct((m, d), F32), sem=("parallel",), name="final_norm")(
                     x, y, gate, g.reshape(1, d))


def _conv_kernel(x_ref, w_ref, o_ref, *, rows, width):
    x = x_ref[0].astype(F32)
    t_len = x.shape[0]
    t = lax.broadcasted_iota(jnp.int32, x.shape, 0)
    col = jnp.bitwise_and(t, width - 1)
    row = jnp.right_shift(t, width.bit_length() - 1)
    acc = jnp.zeros_like(x)
    for dr in (-1, 0, 1):
        if rows == 1 and dr != 0:
            continue
        for dc in (-1, 0, 1):
            off = dr * width + dc
            xs = x if off == 0 else pltpu.roll(x, (-off) % t_len, axis=0)
            ok = jnp.where(jnp.logical_and(col + dc >= 0, col + dc < width), 1.0, 0.0)
            if dr != 0:
                ok = ok * jnp.where(jnp.logical_and(row + dr >= 0, row + dr < rows), 1.0, 0.0)
            tap = (dr + 1) * 3 + (dc + 1)
            acc = acc + (xs * ok) * w_ref[tap:tap + 1, :]
    o_ref[0] = _silu(acc)


def _conv_silu(big, conv_w, *, rows, width, channels):
    b, t_len, _ = big.shape
    assert width & (width - 1) == 0 and rows * width == t_len
    tc = max(2 * LANES, min(channels, CONV_BLOCK_BYTES // (4 * t_len)))
    return _call(functools.partial(_conv_kernel, rows=rows, width=width), grid=(b, channels // tc),
                 in_specs=[pl.BlockSpec((1, t_len, tc), lambda i, j: (i, 0, j)),
                           pl.BlockSpec((9, tc), lambda i, j: (0, j))],
                 out_specs=pl.BlockSpec((1, t_len, tc), lambda i, j: (i, 0, j)),
                 out_shape=jax.ShapeDtypeStruct((b, t_len, channels), F32),
                 sem=("parallel", "parallel"), name="conv_silu")(big, conv_w.reshape(9, channels))


def _mlstm_gate_kernel(h_ref, wi_ref, wf_ref, bi_ref, bf_ref, gb_ref, gw_ref, ga_ref):
    nh, L = MLSTM_HEADS, MLSTM_CHUNK
    h = h_ref[0]
    tg = h.shape[0]
    ig = _dot(h, wi_ref[...]) + bi_ref[...]
    lf = _logsig(_dot(h, wf_ref[...]) + bf_ref[...])
    lane = lax.broadcasted_iota(jnp.int32, ig.shape, 1)
    b = jnp.where(lane < nh, _dot_exact_lhs(_tri(tg, "lower", L), lf), _dot_exact_lhs(_tri(tg, "upper", L), lf))
    b_last = _dot_exact_lhs(_tri(tg, "all", L), lf)
    gb_ref[0] = b
    gw_ref[0] = b_last - b + ig
    ga_ref[0] = ig - b


def _mlstm_gates(h3, w_gate, b_gate, *, tg=256):
    b, t_len, d = h3.shape
    nh = MLSTM_HEADS
    tg = min(tg, t_len)
    wi = _pad_cols(jnp.concatenate([w_gate[0, :, :nh], w_gate[1, :, :nh]], axis=1), LANES)
    wf = _pad_cols(jnp.concatenate([w_gate[0, :, nh:], w_gate[1, :, nh:]], axis=1), LANES)
    bi = _pad_cols(jnp.concatenate([b_gate[0, :nh], b_gate[1, :nh]]).reshape(1, 2 * nh), LANES)
    bf = _pad_cols(jnp.concatenate([b_gate[0, nh:], b_gate[1, nh:]]).reshape(1, 2 * nh), LANES)
    full = lambda shape: pl.BlockSpec(shape, lambda i, c: (0,) * len(shape))
    tok = pl.BlockSpec((1, tg, LANES), lambda i, c: (i, c, 0))
    return _call(_mlstm_gate_kernel, grid=(b, t_len // tg),
                 in_specs=[pl.BlockSpec((1, tg, d), lambda i, c: (i, c, 0)),
                           full((d, LANES)), full((d, LANES)), full((1, LANES)), full((1, LANES))],
                 out_specs=[tok, tok, tok], out_shape=[jax.ShapeDtypeStruct((b, t_len, LANES), F32)] * 3,
                 sem=("parallel", "parallel"), name="mlstm_gates")(h3, wi, wf, bi, bf)


def _mlstm_scan_kernel(qkf_ref, qkb_ref, vf_ref, vb_ref, gbf_ref, gbb_ref, gwf_ref, gwb_ref, gaf_ref, gab_ref,
                       c0_ref, n0_ref, m0_ref, yf_ref, yb_ref, cn_ref, nn_ref, mn_ref, c_s, n_s, m_s):
    nh, L, dk, dv = MLSTM_HEADS, MLSTM_CHUNK, MLSTM_DK, MLSTM_DV
    c = pl.program_id(1)

    @pl.when(c == 0)
    def _():
        c_s[...] = c0_ref[...]
        n_s[...] = n0_ref[...]
        m_s[...] = m0_ref[...]

    ti = lax.broadcasted_iota(jnp.int32, (L, L), 0)
    si = lax.broadcasted_iota(jnp.int32, (L, L), 1)
    lane = lax.broadcasted_iota(jnp.int32, (L, LANES), 1)
    ones = jnp.ones((L, LANES), BF16)
    refs = ((qkf_ref, vf_ref, gbf_ref, gwf_ref, gaf_ref, yf_ref), (qkb_ref, vb_ref, gbb_ref, gwb_ref, gab_ref, yb_ref))
    chains = [(i, d, h) for i in range(c_s.shape[0]) for d in range(2) for h in range(nh)]
    st = {}
    for i, d, h in chains:
        qk_ref, v_ref, gb_ref, gw_ref, ga_ref, _ = refs[d]
        g = d * nh + h
        q = qk_ref[i, :, h * dk:(h + 1) * dk] * (dk ** -0.5)
        k = qk_ref[i, :, (nh + h) * dk:(nh + h + 1) * dk]
        v = v_ref[i, :, h * dv:(h + 1) * dv]
        a_row = sum(lax.dot_general(ones, piece, (((1,), (1,)), ((), ())), preferred_element_type=F32)
                    for piece in _split_bf16(jnp.where(lane == g, ga_ref[i], 0.0), 3))
        b_col = gb_ref[i, :, g:g + 1]
        wl_col = gw_ref[i, :, g:g + 1]
        st[i, d, h] = dict(q=q, k=k, v=v, a_row=a_row, b_col=b_col, wl_col=wl_col, qk=_dot_nt(q, k))
    for i, d, h in chains:
        z = st[i, d, h]
        m_st = m_s[i, d, h]
        mask = (si <= ti) if d == 0 else (si >= ti)
        log_d = jnp.where(mask, z["b_col"] + z["a_row"], -jnp.inf)
        m_t = jnp.maximum(z["b_col"] + m_st, jnp.max(log_d, axis=-1, keepdims=True))
        z.update(m_t=m_t, s=z["qk"] * jnp.exp(log_d - m_t), w_inter=jnp.exp(z["b_col"] + m_st - m_t))
    for i, d, h in chains:
        z = st[i, d, h]
        q, s, w_inter = z["q"], z["s"], z["w_inter"]
        num = _dot(s, z["v"]) + w_inter * _dot(q, c_s[i, d, h])
        den = jnp.sum(s, axis=-1, keepdims=True) + w_inter * jnp.sum(q * n_s[i, d, h], axis=-1, keepdims=True)
        refs[d][5][i, :, h * dv:(h + 1) * dv] = num / jnp.maximum(jnp.abs(den), jnp.exp(-z["m_t"]))
    for i, d, h in chains:
        z = st[i, d, h]
        last = L - 1 if d == 0 else 0
        m_st = m_s[i, d, h]
        b_last = z["b_col"][last:last + 1, :]
        m_new = jnp.maximum(b_last + m_st, jnp.max(z["wl_col"], axis=0, keepdims=True))
        ks = jnp.exp(z["wl_col"] - m_new) * z["k"]
        decay = jnp.exp(b_last + m_st - m_new)
        c_s[i, d, h] = decay * c_s[i, d, h] + _dot_tn(ks, z["v"])
        n_s[i, d, h] = decay * n_s[i, d, h] + jnp.sum(ks, axis=0, keepdims=True)
        m_s[i, d, h] = m_new

    @pl.when(c == pl.num_programs(1) - 1)
    def _():
        cn_ref[...] = c_s[...]
        nn_ref[...] = n_s[...]
        mn_ref[...] = m_s[...]


def _mlstm_scan(qk, big, gb, gw, ga, c0, n0, m0, *, nb=SCAN_BATCH_ROWS):
    b, t_len, _ = qk.shape
    nh, L, dk, dv = MLSTM_HEADS, MLSTM_CHUNK, MLSTM_DK, MLSTM_DV
    nc = t_len // L
    fwd = lambda i, c: (i, c, 0)
    bwd = lambda i, c: (i, nc - 1 - c, 0)
    state = lambda shape: pl.BlockSpec((nb,) + shape, lambda i, c: (i,) + (0,) * len(shape))
    st_shapes = [(2, nh, dk, dv), (2, nh, 1, dk), (2, nh, 1, 1)]
    return _call(
        _mlstm_scan_kernel, grid=(b // nb, nc),
        in_specs=[pl.BlockSpec((nb, L, 2 * nh * dk), fwd), pl.BlockSpec((nb, L, 2 * nh * dk), bwd),
                  pl.BlockSpec((nb, L, nh * dv), lambda i, c: (i, c, 1)),
                  pl.BlockSpec((nb, L, nh * dv), lambda i, c: (i, nc - 1 - c, 1)),
                  pl.BlockSpec((nb, L, LANES), fwd), pl.BlockSpec((nb, L, LANES), bwd),
                  pl.BlockSpec((nb, L, LANES), fwd), pl.BlockSpec((nb, L, LANES), bwd),
                  pl.BlockSpec((nb, L, LANES), fwd), pl.BlockSpec((nb, L, LANES), bwd)] + [state(s) for s in st_shapes],
        out_specs=[pl.BlockSpec((nb, L, nh * dv), fwd), pl.BlockSpec((nb, L, nh * dv), bwd)] + [state(s) for s in st_shapes],
        out_shape=[jax.ShapeDtypeStruct((b, t_len, nh * dv), F32)] * 2 + [jax.ShapeDtypeStruct((b,) + s, F32) for s in st_shapes],
        scratch=[pltpu.VMEM((nb,) + s, F32) for s in st_shapes],
        sem=("parallel", "arbitrary"), name="mlstm_scan")(qk, qk, big, big, gb, gb, gw, gw, ga, ga, c0, n0, m0)


def _mlstm_post_kernel(yf_ref, yb_ref, o_ref, g_ref, out_ref):
    dv = MLSTM_DV
    for h in range(MLSTM_HEADS):
        sl = slice(h * dv, (h + 1) * dv)
        y = yf_ref[:, sl] + yb_ref[:, sl]
        y = y - jnp.mean(y, axis=-1, keepdims=True)
        y = y * lax.rsqrt(jnp.mean(y * y, axis=-1, keepdims=True) + NORM_EPS)
        out_ref[:, sl] = (_sigmoid(o_ref[:, sl].astype(F32)) * (y * g_ref[:, sl])).astype(BF16)


def _mlstm_post(yf, yb, big, norm_g, *, tm=512):
    m, d = yf.shape
    tm = min(tm, m)
    row = lambda i: (i, 0)
    return _call(_mlstm_post_kernel, grid=(m // tm,),
                 in_specs=[pl.BlockSpec((tm, d), row), pl.BlockSpec((tm, d), row),
                           pl.BlockSpec((tm, d), lambda i: (i, 2)), pl.BlockSpec((1, d), lambda i: (0, 0))],
                 out_specs=pl.BlockSpec((tm, d), row), out_shape=jax.ShapeDtypeStruct((m, d), BF16),
                 sem=("parallel",), name="mlstm_post")(yf, yb, big, norm_g.reshape(1, d))


def _mlstm_mixer(h, b, rows, init, w_in, conv_w, w_gate, b_gate, norm_g, w_out, x, gate1):
    m, d = h.shape
    t_len = m // b
    nh, dk = MLSTM_HEADS, MLSTM_DK
    big = _mm(h, w_in, tm=1024, tn=1024, out_dtype=BF16, name="mlstm_in")
    big3 = big.reshape(b, t_len, 3 * d)
    qk = _conv_silu(big3, conv_w, rows=rows, width=t_len // rows, channels=2 * nh * dk)
    gb, gw, ga = _mlstm_gates(h.reshape(b, t_len, d), w_gate, b_gate)
    c0, n0, m0 = init
    yf, yb, cn, nn, mn = _mlstm_scan(qk, big3, gb, gw, ga, c0, n0.reshape(b, 2, nh, 1, dk), m0.reshape(b, 2, nh, 1, 1))
    y = _mlstm_post(yf.reshape(m, d), yb.reshape(m, d), big, norm_g)
    x = _mm(y, w_out, tm=512, tn=1024, extras=[x, gate1],
            epilogue=_residual_epilogue, name="mlstm_out")
    return x, (cn, nn.reshape(b, 2, nh, dk), mn.reshape(b, 2, nh))


def _gla_gate_kernel(h_ref, w1_ref, w2_ref, b_ref, gf_ref, gb_ref):
    half = GLA_HEADS * GLA_DK
    h = h_ref[0]
    z = _dot(_dot(h, w1_ref[...]), w2_ref[...]) + b_ref[...]
    lg = _logsig(z) * (1.0 / GLA_TAU)
    tg = h.shape[0]
    gf_ref[0] = _dot_exact_lhs(_tri(tg, "lower", GLA_CHUNK), lg[:, :half])
    gb_ref[0] = _dot_exact_lhs(_tri(tg, "upper", GLA_CHUNK), lg[:, half:])


def _gla_gates(h3, w_a1, w_a2, b_a, *, tg=256):
    b, t_len, d = h3.shape
    rank, half = w_a1.shape[-1], w_a2.shape[-1]
    w1 = jnp.zeros((d, LANES), F32).at[:, :rank].set(w_a1[0]).at[:, rank:2 * rank].set(w_a1[1])
    w2 = jnp.zeros((LANES, 2 * half), F32).at[:rank, :half].set(w_a2[0]).at[rank:2 * rank, half:].set(w_a2[1])
    bias = jnp.concatenate([b_a[0], b_a[1]]).reshape(1, 2 * half)
    tg = min(tg, t_len)
    full = lambda shape: pl.BlockSpec(shape, lambda i, c: (0,) * len(shape))
    tok = lambda n: pl.BlockSpec((1, tg, n), lambda i, c: (i, c, 0))
    return _call(_gla_gate_kernel, grid=(b, t_len // tg),
                 in_specs=[tok(d), full((d, LANES)), full((LANES, 2 * half)), full((1, 2 * half))],
                 out_specs=[tok(half), tok(half)],
                 out_shape=[jax.ShapeDtypeStruct((b, t_len, half), F32)] * 2,
                 sem=("parallel", "parallel"), name="gla_gates")(h3, w1, w2, bias)


def _gla_scan_kernel(xf_ref, xb_ref, gf_ref, gb_ref, s0_ref, yf_ref, yb_ref, sn_ref, s_s):
    nh, L, dk, dv = GLA_HEADS, GLA_CHUNK, GLA_DK, GLA_DV
    c = pl.program_id(1)

    @pl.when(c == 0)
    def _():
        s_s[...] = s0_ref[0]

    tcol = lax.broadcasted_iota(jnp.int32, (L, 1), 0)
    eye = jnp.where(lax.broadcasted_iota(jnp.int32, (dk, dk), 0) == lax.broadcasted_iota(jnp.int32, (dk, dk), 1), 1.0, 0.0)
    refs = ((xf_ref, gf_ref, yf_ref), (xb_ref, gb_ref, yb_ref))
    chains = [(d, h) for d in range(2) for h in range(nh)]
    n_sub = xf_ref.shape[1] // L
    for step in range(n_sub):
        st = {}
        for d, h in chains:
            x_ref, g_ref, _ = refs[d]
            rows = slice((step if d == 0 else n_sub - 1 - step) * L, (step + 1 if d == 0 else n_sub - step) * L)
            q = x_ref[0, rows, h * dk:(h + 1) * dk] * (dk ** -0.5)
            k = x_ref[0, rows, (nh + h) * dk:(nh + h + 1) * dk]
            v = x_ref[0, rows, 2 * nh * dk + h * dv:2 * nh * dk + (h + 1) * dv]
            g = g_ref[0, rows, h * dk:(h + 1) * dk]
            st[d, h] = dict(rows=rows, q=q, k=k, v=v, g=g, o=_dot(q * jnp.exp(g), s_s[d, h]))
        sub = 8
        for s in range(L):
            for d, h in chains:
                z = st[d, h]
                q, k, v, g, o = z["q"], z["k"], z["v"], z["g"], z["o"]
                lo, hi = ((s // sub) * sub, L) if d == 0 else (0, (s // sub + 1) * sub)
                dec = jnp.exp(jnp.minimum(g[lo:hi] - g[s:s + 1, :], 0.0))
                col = jnp.sum(q[lo:hi] * k[s:s + 1, :] * dec, axis=-1, keepdims=True)
                col = jnp.where((tcol[lo:hi] >= s) if d == 0 else (tcol[lo:hi] <= s), col, 0.0)
                parts = [o[:lo]] * (lo > 0) + [o[lo:hi] + col * v[s:s + 1, :]] + [o[hi:]] * (hi < L)
                z["o"] = parts[0] if len(parts) == 1 else jnp.concatenate(parts, axis=0)
        for d, h in chains:
            z = st[d, h]
            k, v, g = z["k"], z["v"], z["g"]
            last = L - 1 if d == 0 else 0
            refs[d][2][0, z["rows"], h * dv:(h + 1) * dv] = z["o"]
            g_last = g[last:last + 1, :]
            decay_col = jnp.sum(eye * jnp.exp(g_last), axis=-1, keepdims=True)
            s_s[d, h] = decay_col * s_s[d, h] + _dot_tn(k * jnp.exp(g_last - g), v)

    @pl.when(c == pl.num_programs(1) - 1)
    def _():
        sn_ref[0] = s_s[...]


def _gla_scan(qkv, gf, gb, s0):
    b, t_len, width = qkv.shape
    nh, L, dk, dv = GLA_HEADS, GLA_BLOCK, GLA_DK, GLA_DV
    nc = t_len // L
    fwd = lambda i, c: (i, c, 0)
    bwd = lambda i, c: (i, nc - 1 - c, 0)
    st = (2, nh, dk, dv)
    state = pl.BlockSpec((1,) + st, lambda i, c: (i, 0, 0, 0, 0))
    return _call(
        _gla_scan_kernel, grid=(b, nc),
        in_specs=[pl.BlockSpec((1, L, width), fwd), pl.BlockSpec((1, L, width), bwd),
                  pl.BlockSpec((1, L, nh * dk), fwd), pl.BlockSpec((1, L, nh * dk), bwd), state],
        out_specs=[pl.BlockSpec((1, L, nh * dv), fwd), pl.BlockSpec((1, L, nh * dv), bwd), state],
        out_shape=[jax.ShapeDtypeStruct((b, t_len, nh * dv), F32)] * 2 + [jax.ShapeDtypeStruct((b,) + st, F32)],
        scratch=[pltpu.VMEM(st, F32)], sem=("parallel", "arbitrary"), name="gla_scan")(qkv, qkv, gf, gb, s0)


def _gla_post_kernel(yf_ref, yb_ref, r_ref, g_ref, out_ref):
    dv = GLA_DV
    for h in range(GLA_HEADS):
        sl = slice(h * dv, (h + 1) * dv)
        y = yf_ref[:, sl] + yb_ref[:, sl]
        y = y * lax.rsqrt(jnp.mean(y * y, axis=-1, keepdims=True) + NORM_EPS)
        out_ref[:, sl] = ((y * g_ref[:, sl]) * _silu(r_ref[:, sl].astype(F32))).astype(BF16)


def _gla_post(yf, yb, big, norm_g, *, tm=512):
    m, d = yf.shape
    tm = min(tm, m)
    row = lambda i: (i, 0)
    return _call(_gla_post_kernel, grid=(m // tm,),
                 in_specs=[pl.BlockSpec((tm, d), row), pl.BlockSpec((tm, d), row),
                           pl.BlockSpec((tm, d), lambda i: (i, 2)), pl.BlockSpec((1, d), lambda i: (0, 0))],
                 out_specs=pl.BlockSpec((tm, d), row), out_shape=jax.ShapeDtypeStruct((m, d), BF16),
                 sem=("parallel",), name="gla_post")(yf, yb, big, norm_g.reshape(1, d))


def _gla_mixer(h, b, rows, init, w_in, conv_w, w_a1, w_a2, b_a, norm_g, w_out, x, gate1):
    m, d = h.shape
    t_len = m // b
    big = _mm(h, w_in, tm=1024, tn=1024, out_dtype=BF16, name="gla_in")
    big3 = big.reshape(b, t_len, 3 * d)
    qkv = _conv_silu(big3, conv_w, rows=rows, width=t_len // rows, channels=2 * d)
    gf, gb = _gla_gates(h.reshape(b, t_len, d), w_a1, w_a2, b_a)
    yf, yb, sn = _gla_scan(qkv, gf, gb, init[0])
    y = _gla_post(yf.reshape(m, d), yb.reshape(m, d), big, norm_g)
    x = _mm(y, w_out, tm=512, tn=1024, extras=[x, gate1],
            epilogue=_residual_epilogue, name="gla_out")
    return x, (sn,)


def _rwkv_mix_kernel(h_ref, hp_ref, hn_ref, mu_ref, o_ref):
    i = pl.program_id(1)
    h = h_ref[0]
    tm = h.shape[0]
    prev_row = jnp.where(i > 0, hp_ref[0, 7:8, :], 0.0)
    next_row = jnp.where(i < pl.num_programs(1) - 1, hn_ref[0, 0:1, :], 0.0)
    row = lax.broadcasted_iota(jnp.int32, (tm, 1), 0)
    x_prev = jnp.where(row == 0, prev_row, pltpu.roll(h, 1, axis=0))
    x_next = jnp.where(row == tm - 1, next_row, pltpu.roll(h, tm - 1, axis=0))
    xx = 0.5 * (x_prev + x_next) - h
    for j in range(6):
        o_ref[j, 0] = (h + xx * mu_ref[j:j + 1, :]).astype(BF16)


def _rwkv_mix(h3, mu, *, tm=256):
    b, t_len, d = h3.shape
    tm = min(tm, t_len)
    nt, n8 = t_len // tm, t_len // 8
    return _call(_rwkv_mix_kernel, grid=(b, nt),
                 in_specs=[pl.BlockSpec((1, tm, d), lambda i, j: (i, j, 0)),
                           pl.BlockSpec((1, 8, d), lambda i, j: (i, jnp.maximum(j * (tm // 8) - 1, 0), 0)),
                           pl.BlockSpec((1, 8, d), lambda i, j: (i, jnp.minimum((j + 1) * (tm // 8), n8 - 1), 0)),
                           pl.BlockSpec((6, d), lambda i, j: (0, 0))],
                 out_specs=pl.BlockSpec((6, 1, tm, d), lambda i, j: (0, i, j, 0)),
                 out_shape=jax.ShapeDtypeStruct((6, b, t_len, d), BF16),
                 sem=("parallel", "parallel"), name="rwkv_mix")(h3, h3, h3, mu)


def _segsum(x):
    lo = lax.broadcasted_iota(jnp.int32, x.shape, 1) < RWKV_N
    s0 = jnp.sum(jnp.where(lo, x, 0.0), axis=-1, keepdims=True)
    s1 = jnp.sum(jnp.where(lo, 0.0, x), axis=-1, keepdims=True)
    return jnp.where(lo, s0, s1)


def _rwkv_prep_kernel(r_ref, k_ref, v_ref, lora_ref, ap_ref, w0_ref, a0_ref, kk_w_ref, ka_w_ref, rk_w_ref,
                      kk_ref, kb_ref, k2_ref, bonus_ref, lw_ref):
    d = D_MODEL
    for j in range(d // LANES):
        sl = slice(j * LANES, (j + 1) * LANES)
        r, k, v = r_ref[:, sl], k_ref[:, sl], v_ref[:, sl]
        a = _sigmoid(a0_ref[:, sl] + ap_ref[:, sl])
        kk = k * kk_w_ref[:, sl]
        kk = kk / jnp.maximum(jnp.sqrt(_segsum(kk * kk)), 1e-12)
        k2 = k * (1.0 + (a - 1.0) * ka_w_ref[:, sl])
        kk_ref[:, sl] = kk
        kb_ref[:, sl] = kk * a
        k2_ref[:, sl] = k2
        bonus_ref[:, sl] = _segsum(r * k2 * rk_w_ref[:, sl]) * v
        for z in range(2):
            lw_ref[z, :, sl] = -jnp.exp(_logsig(w0_ref[z:z + 1, sl] + lora_ref[:, z * d + j * LANES:z * d + (j + 1) * LANES]) - 0.5)


def _rwkv_prep(r, k, v, lora, a_pre, w0, a0, k_k, k_a, r_k, *, tm=256):
    m, d = r.shape
    tm = min(tm, m)
    row = lambda i: (i, 0)
    tok = pl.BlockSpec((tm, d), row)
    par = pl.BlockSpec((1, d), lambda i: (0, 0))
    return _call(_rwkv_prep_kernel, grid=(m // tm,),
                 in_specs=[tok, tok, tok, pl.BlockSpec((tm, 2 * d), row), tok,
                           pl.BlockSpec((2, d), lambda i: (0, 0)), par, par, par, par],
                 out_specs=[tok, tok, tok, tok, pl.BlockSpec((2, tm, d), lambda i: (0, i, 0))],
                 out_shape=[jax.ShapeDtypeStruct((m, d), F32)] * 4 + [jax.ShapeDtypeStruct((2, m, d), F32)],
                 sem=("parallel",), name="rwkv_prep")(
                     r, k, v, lora, a_pre, w0, a0.reshape(1, d), k_k.reshape(1, d), k_a.reshape(1, d), r_k.reshape(1, d))


def _rwkv_scan_kernel(rf, kf, vf, kkf, kbf, lwf, rb, kb_, vb, kkb, kbb, lwb, s0_ref, yf_ref, yb_ref, sn_ref, s_s):
    L, pairs = RWKV_CHUNK, RWKV_PAIRS
    c = pl.program_id(1)

    @pl.when(c == 0)
    def _():
        s_s[...] = s0_ref[...]

    row = lax.broadcasted_iota(jnp.int32, (L, LANES), 0)
    lane = lax.broadcasted_iota(jnp.int32, (L, LANES), 1)
    lane_tok = jnp.bitwise_and(lane, L - 1)
    lane_head0 = lane - lane_tok
    lo = lane < RWKV_N
    sq_r = lax.broadcasted_iota(jnp.int32, (LANES, LANES), 0) < RWKV_N
    sq_c = lax.broadcasted_iota(jnp.int32, (LANES, LANES), 1) < RWKV_N
    same_head = sq_r == sq_c
    zeros64 = jnp.zeros((2 * L, LANES), F32)
    eye_cat = jnp.where(lane_tok == row, 1.0, 0.0)

    def head_rows(a):
        return jnp.concatenate([jnp.where(lo, a, 0.0), jnp.where(lo, 0.0, a)], axis=0)

    def group_rows(a0, a1):
        return jnp.concatenate([jnp.concatenate([head_rows(a0), zeros64], axis=1),
                                jnp.concatenate([zeros64, head_rows(a1)], axis=1)], axis=0)

    refs = ((rf, kf, vf, kkf, kbf, lwf, yf_ref), (rb, kb_, vb, kkb, kbb, lwb, yb_ref))
    groups = [(i, d, grp) for i in range(s_s.shape[0]) for d in range(2) for grp in range(pairs // 2)]
    st = {}
    for i, d, grp in groups:
        r_ref, k_ref, v_ref, kk_ref, kb_ref, lw_ref, _ = refs[d]
        tri = _tri(L, "lower" if d == 0 else "upper")
        last = L - 1 if d == 0 else 0
        rt, kt, kd, bd, v, e_last, m_b, m_k = [], [], [], [], [], [], 0.0, 0.0
        for q in range(2):
            sl = slice((2 * grp + q) * LANES, (2 * grp + q + 1) * LANES)
            r, k, kk, kb, lw = r_ref[i, :, sl], k_ref[i, :, sl], kk_ref[i, :, sl], kb_ref[i, :, sl], lw_ref[0, i, :, sl]
            cum = _dot_exact_lhs(tri, lw)
            c_last = cum[last:last + 1, :]
            e_neg, e_end = jnp.exp(-cum), jnp.exp(c_last - cum)
            rt.append(r * jnp.exp(cum))
            kt.append(kk * jnp.exp(cum - lw))
            kd.append(k * e_end)
            bd.append(kb * e_end)
            v.append(v_ref[i, :, sl])
            e_last.append(jnp.exp(c_last))
            lhs = jnp.concatenate([kt[q], rt[q]], axis=0)
            pad = lambda a: jnp.concatenate([head_rows(a), zeros64] if q == 0 else [zeros64, head_rows(a)], axis=0)
            m_b = m_b + _dot_nt(lhs, pad(kb * e_neg))
            m_k = m_k + _dot_nt(lhs, pad(k * e_neg))
        st[i, d, grp] = dict(rt=rt, kt=kt, kd=kd, bd=bd, v=v, e_last=e_last, m_b=m_b, m_k=m_k)
    for i, d, grp in groups:
        z = st[i, d, grp]
        strict = (lane_tok < row) if d == 0 else (lane_tok > row)
        incl = (lane_tok <= row) if d == 0 else (lane_tok >= row)
        m_b, m_k = z["m_b"], z["m_k"]
        z["n_kb"] = jnp.where(strict, m_b[:L], 0.0)
        z["m_rb"] = jnp.where(incl, m_b[L:], 0.0)
        m_kk_rk = jnp.concatenate([jnp.where(strict, m_k[:L], 0.0), jnp.where(incl, m_k[L:], 0.0)], axis=0)
        z["kv"] = _dot(m_kk_rk, group_rows(z["v"][0], z["v"][1]))
        z["x"] = eye_cat
    for step in range(L - 1):
        for i, d, grp in groups:
            z = st[i, d, grp]
            u = step if d == 0 else L - 1 - step
            coef = jnp.take_along_axis(z["n_kb"], lane_head0 + u, axis=1)
            z["x"] = z["x"] - coef * z["x"][u:u + 1, :]
    for i, d, grp in groups:
        z = st[i, d, grp]
        x, kt, kv = z["x"], z["kt"], z["kv"]
        x_hi = x.astype(BF16)
        x_lo = x - x_hi.astype(F32)
        rhs = jnp.concatenate([group_rows(kt[0], kt[1]), group_rows(kv[:L, :LANES], kv[:L, LANES:])], axis=1)
        z["w"] = _dot(x_hi, rhs) + _dot(x_lo, rhs)
    for i, d, grp in groups:
        z = st[i, d, grp]
        w = z["w"]
        z["sa"] = [_dot_nt(w[:, q * LANES:(q + 1) * LANES], s_s[i, d, 2 * grp + q]) + w[:, (2 + q) * LANES:(3 + q) * LANES]
                   for q in range(2)]
    for i, d, grp in groups:
        z = st[i, d, grp]
        rb_sa = _dot(z["m_rb"], group_rows(z["sa"][0], z["sa"][1]))
        for q in range(2):
            p = 2 * grp + q
            cols = slice(q * LANES, (q + 1) * LANES)
            refs[d][6][i, :, p * LANES:(p + 1) * LANES] = _dot_nt(z["rt"][q], s_s[i, d, p]) + z["kv"][L:, cols] - rb_sa[:, cols]
    for i, d, grp in groups:
        z = st[i, d, grp]
        for q in range(2):
            p = 2 * grp + q
            s_new = s_s[i, d, p] * z["e_last"][q] + _dot_tn(z["v"][q], z["kd"][q]) - _dot_tn(z["sa"][q], z["bd"][q])
            s_s[i, d, p] = jnp.where(same_head, s_new, 0.0)

    @pl.when(c == pl.num_programs(1) - 1)
    def _():
        sn_ref[...] = s_s[...]


def _rwkv_scan(r, k2, v, kk, kb, lw, s0, *, nb=SCAN_BATCH_ROWS):
    b, t_len, d = r.shape
    L, pairs = RWKV_CHUNK, RWKV_PAIRS
    nc = t_len // L
    fwd = pl.BlockSpec((nb, L, d), lambda i, c: (i, c, 0))
    bwd = pl.BlockSpec((nb, L, d), lambda i, c: (i, nc - 1 - c, 0))
    st = (2, pairs, LANES, LANES)
    state = pl.BlockSpec((nb,) + st, lambda i, c: (i, 0, 0, 0, 0))
    return _call(
        _rwkv_scan_kernel, grid=(b // nb, nc),
        in_specs=[fwd] * 5 + [pl.BlockSpec((1, nb, L, d), lambda i, c: (0, i, c, 0))]
        + [bwd] * 5 + [pl.BlockSpec((1, nb, L, d), lambda i, c: (1, i, nc - 1 - c, 0)), state],
        out_specs=[fwd, bwd, state],
        out_shape=[jax.ShapeDtypeStruct((b, t_len, d), F32)] * 2 + [jax.ShapeDtypeStruct((b,) + st, F32)],
        scratch=[pltpu.VMEM((nb,) + st, F32)], sem=("parallel", "arbitrary"), name="rwkv_scan")(
            r, k2, v, kk, kb, lw, r, k2, v, kk, kb, lw, s0)


def _rwkv_post_kernel(yf_ref, yb_ref, bonus_ref, g_ref, lng_ref, lnb_ref, out_ref):
    for j in range(D_MODEL // LANES):
        sl = slice(j * LANES, (j + 1) * LANES)
        y = yf_ref[:, sl] + yb_ref[:, sl]
        y = y - _segsum(y) * (1.0 / RWKV_N)
        y = y * lax.rsqrt(_segsum(y * y) * (1.0 / RWKV_N) + RWKV_GN_EPS)
        out_ref[:, sl] = ((y * lng_ref[:, sl] + lnb_ref[:, sl] + bonus_ref[:, sl]) * g_ref[:, sl]).astype(BF16)


def _rwkv_post(yf, yb, bonus, g, ln_g, ln_b, *, tm=256):
    m, d = yf.shape
    tm = min(tm, m)
    tok = pl.BlockSpec((tm, d), lambda i: (i, 0))
    par = pl.BlockSpec((1, d), lambda i: (0, 0))
    return _call(_rwkv_post_kernel, grid=(m // tm,), in_specs=[tok, tok, tok, tok, par, par], out_specs=tok,
                 out_shape=jax.ShapeDtypeStruct((m, d), BF16), sem=("parallel",), name="rwkv_post")(
                     yf, yb, bonus, g, ln_g.reshape(1, d), ln_b.reshape(1, d))


def _pad_cols(w, n):
    return jnp.zeros(w.shape[:-1] + (n,), w.dtype).at[..., :w.shape[-1]].set(w)


def _pad_rows(w, n):
    return jnp.zeros((n,) + w.shape[1:], w.dtype).at[:w.shape[0]].set(w)


def _rwkv_lowrank_kernel(xw_ref, xa_ref, xg_ref, w1_ref, w2_ref, a1_ref, a2_ref, g1_ref, g2_ref, lora_ref, ap_ref, g_ref):
    lora_ref[...] = _dot(jnp.tanh(_dot(xw_ref[...], w1_ref[...])), w2_ref[...])
    ap_ref[...] = _dot(_dot(xa_ref[...], a1_ref[...]), a2_ref[...])
    g_ref[...] = _dot(_sigmoid(_dot(xg_ref[...], g1_ref[...])), g2_ref[...])


def _rwkv_lowrank(xmix, w1, w2, a1, a2, g1, g2, *, tm=512):
    _, m, d = xmix.shape
    tm = min(tm, m)
    mix = lambda idx: pl.BlockSpec((None, tm, d), lambda i: (idx, i, 0))
    full = lambda w: pl.BlockSpec(w.shape, lambda i: (0, 0))
    tok = lambda n: pl.BlockSpec((tm, n), lambda i: (i, 0))
    return _call(_rwkv_lowrank_kernel, grid=(m // tm,),
                 in_specs=[mix(1), mix(4), mix(5)] + [full(w) for w in (w1, w2, a1, a2, g1, g2)],
                 out_specs=[tok(2 * d), tok(d), tok(d)],
                 out_shape=[jax.ShapeDtypeStruct((m, 2 * d), F32), jax.ShapeDtypeStruct((m, d), F32),
                            jax.ShapeDtypeStruct((m, d), F32)],
                 sem=("parallel",), name="rwkv_lowrank")(xmix, xmix, xmix, w1, w2, a1, a2, g1, g2)


def _rwkv_mixer(h, b, init, j, mu, w_rkv, w0, w1, w2, a0, a1, a2, g1, g2, k_k, k_a, r_k, ln_g, ln_b, w_out, x, gate1):
    m, d = h.shape
    t_len = m // b
    xmix = _rwkv_mix(h.reshape(b, t_len, d), mu[j]).reshape(6, m, d)
    proj = lambda idx, w, wlead, name, **kw: _mm(xmix, w, xlead=(idx,), wlead=wlead, tm=512, tn=1024, name=name, **kw)
    r = proj(0, w_rkv, (j, 0), "rwkv_r")
    k = proj(2, w_rkv, (j, 1), "rwkv_k")
    v = proj(3, w_rkv, (j, 2), "rwkv_v")
    rank_w = w1.shape[-1]
    w1cat = jnp.concatenate([w1[j, 0], w1[j, 1]], axis=1)
    w2bd = jnp.zeros((2 * rank_w, 2 * d), F32).at[:rank_w, :d].set(w2[j, 0]).at[rank_w:, d:].set(w2[j, 1])
    lora, a_pre, g = _rwkv_lowrank(xmix, w1cat, w2bd, _pad_cols(a1[j], LANES), _pad_rows(a2[j], LANES), g1[j], g2[j])
    kk, kb, k2, bonus, lw = _rwkv_prep(r, k, v, lora, a_pre, w0[j], a0[j], k_k[j], k_a[j], r_k[j].reshape(d))
    s3 = lambda z: z.reshape(b, t_len, d)
    yf, yb, sn = _rwkv_scan(s3(r), s3(k2), s3(v), s3(kk), s3(kb), lw.reshape(2, b, t_len, d), init)
    y = _rwkv_post(yf.reshape(m, d), yb.reshape(m, d), bonus, g, ln_g[j], ln_b[j])
    x = _mm(y, w_out, wlead=(j,), tm=512, tn=1024, extras=[x, gate1],
            epilogue=_residual_epilogue, name="rwkv_out")
    return x, sn


def _rwkv_state_to_pairs(s):
    b = s.shape[0]
    n = RWKV_N
    out = jnp.zeros((b, 2, RWKV_PAIRS, 2 * n, 2 * n), F32)
    return out.at[..., :n, :n].set(s[:, :, 0::2]).at[..., n:, n:].set(s[:, :, 1::2])


def _rwkv_state_from_pairs(sp):
    n = RWKV_N
    b = sp.shape[0]
    return jnp.stack([sp[..., :n, :n], sp[..., n:, n:]], axis=3).reshape(b, 2, RWKV_HEADS, n, n)


def _router_kernel(x_ref, w_ref, b_ref, o_ref, cnt_ref, seen, *, tiles_per_block):
    e0, per = ROUTER_EXPERT_LANE0, MOE_PER_GROUP

    @pl.when(pl.program_id(0) % tiles_per_block == 0)
    def _():
        seen[...] = jnp.zeros_like(seen)

    logits = _dot_3pass(x_ref[...], w_ref[...]) + b_ref[...]
    lane = lax.broadcasted_iota(jnp.int32, logits.shape, 1).astype(F32)
    big = float(LANES)
    rmax = lambda z: jnp.max(z, axis=-1, keepdims=True)
    rsum = lambda z: jnp.sum(z, axis=-1, keepdims=True)
    first = lambda m: jnp.min(jnp.where(m, lane, big), axis=-1, keepdims=True)
    is_g = lane < MOE_GROUPS
    gmax = rmax(jnp.where(is_g, logits, -jnp.inf))
    gsel = first(jnp.logical_and(is_g, logits >= gmax))
    gw = 1.0 / rsum(jnp.where(is_g, jnp.exp(logits - gmax), 0.0))
    in_grp = jnp.logical_and(lane >= e0 + gsel * per, lane < e0 + (gsel + 1) * per)
    emax = rmax(jnp.where(in_grp, logits, -jnp.inf))
    p = jnp.where(in_grp, jnp.exp(logits - emax), 0.0)
    p = p / rsum(p)
    p1 = rmax(jnp.where(in_grp, p, -1.0))
    i1 = first(jnp.logical_and(in_grp, p >= p1))
    rest = jnp.logical_and(in_grp, lane != i1)
    p2 = rmax(jnp.where(rest, p, -1.0))
    i2 = first(jnp.logical_and(rest, p >= p2))
    tot = p1 + p2
    comb = jnp.where(lane == i1, p1 / tot * gw, jnp.where(lane == i2, p2 / tot * gw, 0.0))
    onehot = jnp.where(lane == gsel, 1.0, 0.0)
    csum = seen[...] + jnp.dot(_tri(onehot.shape[0], "lower").astype(BF16), onehot.astype(BF16), preferred_element_type=F32)
    rank = rsum(onehot * csum) - 1.0
    seen[...] = csum[onehot.shape[0] - 1:, :]
    cnt_ref[0] = seen[...]
    o_ref[...] = jnp.where(lane == ROUTER_GROUP_LANE, gsel, jnp.where(lane == ROUTER_RANK_LANE, rank, comb))


def _router(x, w_group, b_group, w_expert, b_expert, *, tm=512):
    m, d = x.shape
    tm = min(tm, m)
    tiles_per_block = min(MOE_BLOCK_TOKENS, m) // tm
    e0 = ROUTER_EXPERT_LANE0
    w = jnp.zeros((d, LANES), F32).at[:, :MOE_GROUPS].set(w_group).at[:, e0:e0 + MOE_EXPERTS].set(w_expert)
    bias = jnp.zeros((1, LANES), F32).at[0, :MOE_GROUPS].set(b_group).at[0, e0:e0 + MOE_EXPERTS].set(b_expert)
    return _call(functools.partial(_router_kernel, tiles_per_block=tiles_per_block), grid=(m // tm,),
                 in_specs=[pl.BlockSpec((tm, d), lambda i: (i, 0)), pl.BlockSpec((d, LANES), lambda i: (0, 0)),
                           pl.BlockSpec((1, LANES), lambda i: (0, 0))],
                 out_specs=[pl.BlockSpec((tm, LANES), lambda i: (i, 0)),
                            pl.BlockSpec((1, 1, LANES), lambda i: (i // tiles_per_block, 0, 0))],
                 out_shape=[jax.ShapeDtypeStruct((m, LANES), F32),
                            jax.ShapeDtypeStruct((m // tm // tiles_per_block, 1, LANES), F32)],
                 scratch=[pltpu.VMEM((1, LANES), F32)], sem=("arbitrary",), name="moe_router")(x, w, bias)


def _moe_dispatch(comb, counts):
    n = comb.shape[0]
    tb, tr = min(MOE_BLOCK_TOKENS, n), MOE_ROW_TILE
    nblk = n // tb
    cnt = counts[:, 0, :MOE_GROUPS].astype(jnp.int32)
    padded = (cnt + tr - 1) // tr * tr
    start = jnp.cumsum(padded, axis=1) - padded
    grp = comb[:, ROUTER_GROUP_LANE].astype(jnp.int32).reshape(nblk, tb)
    rank = comb[:, ROUTER_RANK_LANE].astype(jnp.int32).reshape(nblk, tb)
    onehot = grp[..., None] == jnp.arange(MOE_GROUPS, dtype=jnp.int32)
    pos = jnp.sum(jnp.where(onehot, start[:, None, :], 0), axis=-1) + rank
    meta = jnp.concatenate([start, padded // tr], axis=1).astype(jnp.int32)
    return pos.reshape(-1), meta.reshape(-1)


def _moe_sparse_kernel(pos_ref, meta_ref, h_ref, comb_ref, wg_ref, wu_ref, wd_ref, y_ref, xs, cs, acc):
    blk, e = pl.program_id(0), pl.program_id(1)
    tr, per = MOE_ROW_TILE, MOE_PER_GROUP
    tb = h_ref.shape[0]
    g, i = e // per, e % per
    base, mbase = blk * tb, blk * 2 * MOE_GROUPS

    @pl.when(jnp.logical_and(blk == 0, e == 0))
    def _():
        xs[...] = jnp.zeros_like(xs)
        cs[...] = jnp.zeros_like(cs)

    @pl.when(e == 0)
    def _():
        def body(j, carry):
            t0 = pl.multiple_of(j * 8, 8)
            rows8, comb8 = h_ref[pl.ds(t0, 8), :], comb_ref[pl.ds(t0, 8), :]
            for u in range(8):
                r = pos_ref[base + t0 + u]
                xs[pl.ds(r, 1), :] = rows8[u:u + 1, :]
                cs[pl.ds(r, 1), :] = comb8[u:u + 1, :]
            return carry

        lax.fori_loop(0, tb // 8, body, 0)

    lo, n_tiles = meta_ref[mbase + g], meta_ref[mbase + MOE_GROUPS + g]
    lane = lax.broadcasted_iota(jnp.int32, (tr, LANES), 1)

    def tile_body(t, carry):
        rows = pl.ds(pl.multiple_of(lo + t * tr, tr), tr)
        x = xs[rows, :]
        ce = jnp.sum(jnp.where(lane == e + ROUTER_EXPERT_LANE0, cs[rows, :], 0.0), axis=-1, keepdims=True)
        hid = _silu(_dot(x, wg_ref[0, 0])) * _dot(x, wu_ref[0, 0])
        contrib = _dot(hid * ce, wd_ref[0, 0])

        @pl.when(i == 0)
        def _():
            acc[rows, :] = contrib

        @pl.when(i != 0)
        def _():
            acc[rows, :] += contrib

        return carry

    lax.fori_loop(0, n_tiles, tile_body, 0)

    @pl.when(e == pl.num_programs(1) - 1)
    def _():
        def body(j, carry):
            t0 = pl.multiple_of(j * 8, 8)
            for u in range(8):
                y_ref[pl.ds(t0 + u, 1), :] = acc[pl.ds(pos_ref[base + t0 + u], 1), :]
            return carry

        lax.fori_loop(0, tb // 8, body, 0)


def _moe_sparse(h, comb, counts, w_gate, w_up, w_down, layer):
    n, d = h.shape
    f = w_gate.shape[-1]
    tb, tr = min(MOE_BLOCK_TOKENS, n), MOE_ROW_TILE
    nblk, r_blk = n // tb, tb + MOE_GROUPS * tr
    tok, meta = _moe_dispatch(comb, counts)
    wspec = lambda shape: pl.BlockSpec((1, 1) + shape, lambda b, e, *_: (layer, e, 0, 0))
    once = pl.Buffered(1)
    grid_spec = pltpu.PrefetchScalarGridSpec(
        num_scalar_prefetch=2, grid=(nblk, MOE_EXPERTS),
        in_specs=[pl.BlockSpec((tb, d), lambda b, e, *_: (b, 0), pipeline_mode=once),
                  pl.BlockSpec((tb, LANES), lambda b, e, *_: (b, 0), pipeline_mode=once),
                  wspec((d, f)), wspec((d, f)), wspec((f, d))],
        out_specs=pl.BlockSpec((tb, d), lambda b, e, *_: (b, 0), pipeline_mode=once),
        scratch_shapes=[pltpu.VMEM((r_blk, d), F32), pltpu.VMEM((r_blk, LANES), F32), pltpu.VMEM((r_blk, d), F32)])
    return pl.pallas_call(
        _moe_sparse_kernel, grid_spec=grid_spec, out_shape=jax.ShapeDtypeStruct((n, d), F32), name="moe_sparse",
        compiler_params=pltpu.CompilerParams(dimension_semantics=("arbitrary", "arbitrary"),
                                             vmem_limit_bytes=VMEM_LIMIT_BYTES))(tok, meta, h, comb, w_gate, w_up, w_down)


def _run_trunk(x3, mods, rows, init_a, init_b, init_c, p):
    b, t_len, d = x3.shape
    m = b * t_len
    x = x3.reshape(m, d)
    new_a, new_b, new_c = [], [], []
    resid = None
    for i in range(p["ada_w"].shape[0]):
        j = i // 3
        shift1, scale1, gate1, shift2, scale2, gate2 = mods[i]
        h = _normmod(x, p["norm1_g"][i], scale1, shift1, resid=resid, out_dtype=F32 if i % 3 == 2 else BF16)
        if resid is not None:
            x, h = h
        if i % 3 == 0:
            x, st = _mlstm_mixer(h, b, rows, tuple(s[:, j] for s in init_a), p["mlstm_w_in"][j], p["mlstm_conv_w"][j],
                                 p["mlstm_w_gate"][j], p["mlstm_b_gate"][j], p["mlstm_norm_g"][j], p["mlstm_w_out"][j],
                                 x, gate1)
            new_a.append(st)
        elif i % 3 == 1:
            x, st = _gla_mixer(h, b, rows, tuple(s[:, j] for s in init_b), p["gla_w_in"][j], p["gla_conv_w"][j],
                               p["gla_w_a1"][j], p["gla_w_a2"][j], p["gla_b_a"][j], p["gla_norm_g"][j], p["gla_w_out"][j],
                               x, gate1)
            new_b.append(st)
        else:
            s0 = (jnp.zeros((b, 2, RWKV_PAIRS, LANES, LANES), F32) if init_c is None
                  else _rwkv_state_to_pairs(init_c[0][:, j]))
            x, st = _rwkv_mixer(h, b, s0, j,p["rwkv_mu"], p["rwkv_w_rkv"], p["rwkv_w0"],
                                p["rwkv_w1"], p["rwkv_w2"], p["rwkv_a0"], p["rwkv_a1"], p["rwkv_a2"], p["rwkv_g1"],
                                p["rwkv_g2"], p["rwkv_k_k"], p["rwkv_k_a"], p["rwkv_r_k"], p["rwkv_ln_g"], p["rwkv_ln_b"],
                                p["rwkv_w_out"], x, gate1)
            new_c.append((_rwkv_state_from_pairs(st),))
        hff = _normmod(x, p["norm2_g"][i], scale2, shift2)
        comb, counts = _router(hff, p["moe_w_group"][i], p["moe_b_group"][i], p["moe_w_expert"][i], p["moe_b_expert"][i])
        resid = (_moe_sparse(hff, comb, counts, p["moe_w_gate"], p["moe_w_up"], p["moe_w_down"], i), gate2)
    stack = lambda per_layer: tuple(jnp.stack(parts, axis=1) for parts in zip(*per_layer))
    y = _resid_rmsnorm(x, resid[0], resid[1], p["final_norm_g"]).reshape(b, t_len, d)
    return y, stack(new_a), stack(new_b), stack(new_c)


def kernel(x_prompt, x_sample, state_mlstm_C, state_mlstm_n, state_mlstm_m, state_gla_S, state_rwkv_S, c, c_ctx, ada_w, ada_b, norm1_g, norm2_g, moe_w_group, moe_b_group, moe_w_expert, moe_b_expert, moe_w_gate, moe_w_up, moe_w_down, final_norm_g, mlstm_w_in, mlstm_conv_w, mlstm_w_gate, mlstm_b_gate, mlstm_norm_g, mlstm_w_out, gla_w_in, gla_conv_w, gla_w_a1, gla_w_a2, gla_b_a, gla_norm_g, gla_w_out, rwkv_mu, rwkv_w_rkv, rwkv_w0, rwkv_w1, rwkv_w2, rwkv_a0, rwkv_a1, rwkv_a2, rwkv_g1, rwkv_g2, rwkv_k_k, rwkv_k_a, rwkv_r_k, rwkv_ln_g, rwkv_ln_b, rwkv_w_out):
    p = dict(ada_w=ada_w, norm1_g=norm1_g, norm2_g=norm2_g, moe_w_group=moe_w_group, moe_b_group=moe_b_group,
             moe_w_expert=moe_w_expert, moe_b_expert=moe_b_expert, moe_w_gate=moe_w_gate, moe_w_up=moe_w_up,
             moe_w_down=moe_w_down, final_norm_g=final_norm_g, mlstm_w_in=mlstm_w_in, mlstm_conv_w=mlstm_conv_w,
             mlstm_w_gate=mlstm_w_gate, mlstm_b_gate=mlstm_b_gate, mlstm_norm_g=mlstm_norm_g, mlstm_w_out=mlstm_w_out,
             gla_w_in=gla_w_in, gla_conv_w=gla_conv_w, gla_w_a1=gla_w_a1, gla_w_a2=gla_w_a2, gla_b_a=gla_b_a,
             gla_norm_g=gla_norm_g, gla_w_out=gla_w_out, rwkv_mu=rwkv_mu, rwkv_w_rkv=rwkv_w_rkv, rwkv_w0=rwkv_w0,
             rwkv_w1=rwkv_w1, rwkv_w2=rwkv_w2, rwkv_a0=rwkv_a0, rwkv_a1=rwkv_a1, rwkv_a2=rwkv_a2, rwkv_g1=rwkv_g1,
             rwkv_g2=rwkv_g2, rwkv_k_k=rwkv_k_k, rwkv_k_a=rwkv_k_a, rwkv_r_k=rwkv_r_k, rwkv_ln_g=rwkv_ln_g,
             rwkv_ln_b=rwkv_ln_b, rwkv_w_out=rwkv_w_out)
    for name in ("mlstm_w_in", "mlstm_w_out", "gla_w_in", "gla_w_out",
                 "rwkv_w_rkv", "rwkv_w_out"):
        p[name] = p[name].astype(BF16)
    depth, d = ada_w.shape[0], ada_w.shape[1]
    n_dec = c.shape[0]
    cond8 = jnp.zeros((8, d), F32).at[0].set(c_ctx).at[1:1 + n_dec].set(c)
    mod = _ada(cond8, ada_w, ada_b)
    split = lambda rows: [tuple(rows[i][:, None, k * d:(k + 1) * d] for k in range(6)) for i in range(depth)]
    mods_ctx = split(mod[:, 0:1])
    mods_dec = split(mod[:, 1:1 + n_dec])

    bp = x_prompt.shape[0]
    zeros_like_ctx = lambda s: jnp.zeros((bp,) + s.shape[1:], F32)
    y_prompt, st_a, st_b, st_c = _run_trunk(
        x_prompt, mods_ctx, 1, tuple(zeros_like_ctx(s) for s in (state_mlstm_C, state_mlstm_n, state_mlstm_m)),
        (zeros_like_ctx(state_gla_S),), None, p)
    y_sample = _run_trunk(x_sample, mods_dec, x_sample.shape[1] // GRID_W,
                          (state_mlstm_C, state_mlstm_n, state_mlstm_m), (state_gla_S,), (state_rwkv_S,), p)[0]
    return (y_prompt, y_sample, st_a[0], st_a[1], st_a[2], st_b[0], st_c[0])
```

```python
import functools

import jax
import jax.numpy as jnp
from jax import lax
from jax.experimental import pallas as pl
from jax.experimental.pallas import tpu as pltpu

F32 = jnp.float32
BF16 = jnp.bfloat16

D_MODEL = 1024
GRID_W = 64
NORM_EPS = 1e-6

MLSTM_HEADS = 4
MLSTM_DK = 128
MLSTM_DV = 256
MLSTM_CHUNK = 128

GLA_HEADS = 4
GLA_DK = 128
GLA_DV = 256
GLA_TAU = 16.0
GLA_CHUNK = 16
GLA_BLOCK = 64

RWKV_N = 64
RWKV_HEADS = 16
RWKV_PAIRS = RWKV_HEADS // 2
RWKV_GN_EPS = 64e-5
RWKV_CHUNK = 32

MOE_GROUPS = 4
MOE_PER_GROUP = 4
MOE_EXPERTS = 16
ROUTER_EXPERT_LANE0 = 8
ROUTER_GROUP_LANE = 127
ROUTER_RANK_LANE = 126
MOE_BLOCK_TOKENS = 2048
MOE_ROW_TILE = 256

CONV_BLOCK_BYTES = 2 * 1024 * 1024
SCAN_BATCH_ROWS = 2

LANES = 128
VMEM_LIMIT_BYTES = 56 * 1024 * 1024


def _call(kernel, *, grid, in_specs, out_specs, out_shape, scratch=(), sem, name):
    return pl.pallas_call(
        kernel, grid=grid, in_specs=in_specs, out_specs=out_specs, out_shape=out_shape,
        scratch_shapes=list(scratch), name=name,
        compiler_params=pltpu.CompilerParams(dimension_semantics=sem, vmem_limit_bytes=VMEM_LIMIT_BYTES))


def _sigmoid(x):
    return 1.0 / (1.0 + jnp.exp(-x))


def _silu(x):
    return x * _sigmoid(x)


def _logsig(x):
    return jnp.minimum(x, 0.0) - jnp.log1p(jnp.exp(-jnp.abs(x)))


def _dot(a, b):
    return jnp.dot(a.astype(BF16), b.astype(BF16), preferred_element_type=F32)


def _dot_nt(a, b):
    return lax.dot_general(a.astype(BF16), b.astype(BF16), (((1,), (1,)), ((), ())), preferred_element_type=F32)


def _dot_tn(a, b):
    return lax.dot_general(a.astype(BF16), b.astype(BF16), (((0,), (0,)), ((), ())), preferred_element_type=F32)


def _split_bf16(x, parts):
    out = []
    for _ in range(parts):
        piece = x.astype(BF16)
        out.append(piece)
        x = x - piece.astype(F32)
    return out


def _dot_exact_lhs(a, b):
    a = a.astype(BF16)
    return sum(jnp.dot(a, piece, preferred_element_type=F32) for piece in _split_bf16(b, 3))


def _dot_3pass(a, b):
    (a_hi, a_lo), (b_hi, b_lo) = _split_bf16(a, 2), _split_bf16(b, 2)
    dot = lambda p, q: jnp.dot(p, q, preferred_element_type=F32)
    return dot(a_hi, b_hi) + (dot(a_hi, b_lo) + dot(a_lo, b_hi))


def _tri(n, kind, block=None):
    r = lax.broadcasted_iota(jnp.int32, (n, n), 0)
    c = lax.broadcasted_iota(jnp.int32, (n, n), 1)
    m = {"lower": c <= r, "upper": c >= r, "all": c >= 0}[kind]
    if block is not None:
        sh = block.bit_length() - 1
        m = jnp.logical_and(m, jnp.right_shift(r, sh) == jnp.right_shift(c, sh))
    return jnp.where(m, 1.0, 0.0).astype(F32)


def _mm(x, w, *, tm, tn, xlead=(), wlead=(), extras=(), epilogue=None, out_dtype=F32, name):
    m, k = x.shape[-2:]
    n = w.shape[-1]
    tm, tn = min(tm, m), min(tn, n)

    def kern(x_ref, w_ref, *rest):
        acc = _dot(x_ref[...], w_ref[...])
        if epilogue is not None:
            acc = epilogue(acc, *[r[...] for r in rest[:-1]])
        rest[-1][...] = acc.astype(out_dtype)

    def extra_spec(arr):
        if arr.ndim == 2:
            return pl.BlockSpec((tm, tn), lambda i, j: (i, j))
        tiles_per_row = (m // arr.shape[0]) // tm
        return pl.BlockSpec((1, 1, tn), lambda i, j: (i // tiles_per_row, 0, j))

    in_specs = [
        pl.BlockSpec((None,) * len(xlead) + (tm, k), lambda i, j: tuple(xlead) + (i, 0)),
        pl.BlockSpec((None,) * len(wlead) + (k, tn), lambda i, j: tuple(wlead) + (0, j)),
    ] + [extra_spec(a) for a in extras]
    return _call(kern, grid=(m // tm, n // tn), in_specs=in_specs,
                 out_specs=pl.BlockSpec((tm, tn), lambda i, j: (i, j)),
                 out_shape=jax.ShapeDtypeStruct((m, n), out_dtype),
                 sem=("parallel", "parallel"), name=name)(x, w, *extras)


def _residual_epilogue(acc, x, gate):
    return x + gate[0] * acc


def _ada_kernel(c_ref, w_ref, b_ref, o_ref):
    o_ref[0] = _dot(_silu(c_ref[...]), w_ref[0]) + b_ref[0]


def _ada(cond8, ada_w, ada_b):
    depth, d, n = ada_w.shape
    tn = 1536
    return _call(_ada_kernel, grid=(depth, n // tn),
                 in_specs=[pl.BlockSpec((8, d), lambda l, j: (0, 0)),
                           pl.BlockSpec((1, d, tn), lambda l, j: (l, 0, j)),
                           pl.BlockSpec((1, 1, tn), lambda l, j: (l, 0, j))],
                 out_specs=pl.BlockSpec((1, 8, tn), lambda l, j: (l, 0, j)),
                 out_shape=jax.ShapeDtypeStruct((depth, 8, n), F32),
                 sem=("parallel", "parallel"), name="ada")(cond8, ada_w, ada_b.reshape(depth, 1, n))


def _normmod_kernel(x_ref, g_ref, sc_ref, sh_ref, o_ref):
    x = x_ref[...]
    y = x * lax.rsqrt(jnp.mean(x * x, axis=-1, keepdims=True) + NORM_EPS)
    o_ref[...] = ((y * g_ref[...]) * (1.0 + sc_ref[0]) + sh_ref[0]).astype(o_ref.dtype)


def _resid_normmod_kernel(x_ref, y_ref, gate_ref, g_ref, sc_ref, sh_ref, xo_ref, o_ref):
    x = x_ref[...] + gate_ref[0] * y_ref[...]
    xo_ref[...] = x
    y = x * lax.rsqrt(jnp.mean(x * x, axis=-1, keepdims=True) + NORM_EPS)
    o_ref[...] = ((y * g_ref[...]) * (1.0 + sc_ref[0]) + sh_ref[0]).astype(o_ref.dtype)


def _normmod(x, g, scale, shift, *, resid=None, out_dtype=F32, tm=1024):
    m, d = x.shape
    tm = min(tm, m)
    tiles_per_row = (m // scale.shape[0]) // tm
    mod = pl.BlockSpec((1, 1, d), lambda i: (i // tiles_per_row, 0, 0))
    tok = pl.BlockSpec((tm, d), lambda i: (i, 0))
    par = pl.BlockSpec((1, d), lambda i: (0, 0))
    if resid is None:
        return _call(_normmod_kernel, grid=(m // tm,), in_specs=[tok, par, mod, mod], out_specs=tok,
                     out_shape=jax.ShapeDtypeStruct((m, d), out_dtype), sem=("parallel",), name="normmod")(
                         x, g.reshape(1, d), scale, shift)
    y, gate = resid
    return _call(_resid_normmod_kernel, grid=(m // tm,), in_specs=[tok, tok, mod, par, mod, mod], out_specs=[tok, tok],
                 out_shape=[jax.ShapeDtypeStruct((m, d), F32), jax.ShapeDtypeStruct((m, d), out_dtype)],
                 sem=("parallel",), name="resid_normmod")(x, y, gate, g.reshape(1, d), scale, shift)


def _resid_rmsnorm_kernel(x_ref, y_ref, gate_ref, g_ref, o_ref):
    x = x_ref[...] + gate_ref[0] * y_ref[...]
    o_ref[...] = x * lax.rsqrt(jnp.mean(x * x, axis=-1, keepdims=True) + NORM_EPS) * g_ref[...]


def _resid_rmsnorm(x, y, gate, g, *, tm=512):
    m, d = x.shape
    tm = min(tm, m)
    tiles_per_row = (m // gate.shape[0]) // tm
    tok = pl.BlockSpec((tm, d), lambda i: (i, 0))
    return _call(_resid_rmsnorm_kernel, grid=(m // tm,),
                 in_specs=[tok, tok, pl.BlockSpec((1, 1, d), lambda i: (i // tiles_per_row, 0, 0)),
                           pl.BlockSpec((1, d), lambda i: (0, 0))],
                 out_specs=tok, out_shape=jax.ShapeDtypeStruct((m, d), F32), sem=("parallel",), name="final_norm")(
                     x, y, gate, g.reshape(1, d))


def _conv_kernel(x_ref, w_ref, o_ref, *, rows, width):
    x = x_ref[0].astype(F32)
    t_len = x.shape[0]
    t = lax.broadcasted_iota(jnp.int32, x.shape, 0)
    col = jnp.bitwise_and(t, width - 1)
    row = jnp.right_shift(t, width.bit_length() - 1)
    acc = jnp.zeros_like(x)
    for dr in (-1, 0, 1):
        if rows == 1 and dr != 0:
            continue
        for dc in (-1, 0, 1):
            off = dr * width + dc
            xs = x if off == 0 else pltpu.roll(x, (-off) % t_len, axis=0)
            ok = jnp.where(jnp.logical_and(col + dc >= 0, col + dc < width), 1.0, 0.0)
            if dr != 0:
                ok = ok * jnp.where(jnp.logical_and(row + dr >= 0, row + dr < rows), 1.0, 0.0)
            tap = (dr + 1) * 3 + (dc + 1)
            acc = acc + (xs * ok) * w_ref[tap:tap + 1, :]
    o_ref[0] = _silu(acc)


def _conv_silu(big, conv_w, *, rows, width, channels):
    b, t_len, _ = big.shape
    assert width & (width - 1) == 0 and rows * width == t_len
    tc = max(2 * LANES, min(channels, CONV_BLOCK_BYTES // (4 * t_len)))
    return _call(functools.partial(_conv_kernel, rows=rows, width=width), grid=(b, channels // tc),
                 in_specs=[pl.BlockSpec((1, t_len, tc), lambda i, j: (i, 0, j)),
                           pl.BlockSpec((9, tc), lambda i, j: (0, j))],
                 out_specs=pl.BlockSpec((1, t_len, tc), lambda i, j: (i, 0, j)),
                 out_shape=jax.ShapeDtypeStruct((b, t_len, channels), F32),
                 sem=("parallel", "parallel"), name="conv_silu")(big, conv_w.reshape(9, channels))


def _mlstm_gate_kernel(h_ref, wi_ref, wf_ref, bi_ref, bf_ref, gb_ref, gw_ref, ga_ref):
    nh, L = MLSTM_HEADS, MLSTM_CHUNK
    h = h_ref[0]
    tg = h.shape[0]
    ig = _dot(h, wi_ref[...]) + bi_ref[...]
    lf = _logsig(_dot(h, wf_ref[...]) + bf_ref[...])
    lane = lax.broadcasted_iota(jnp.int32, ig.shape, 1)
    b = jnp.where(lane < nh, _dot_exact_lhs(_tri(tg, "lower", L), lf), _dot_exact_lhs(_tri(tg, "upper", L), lf))
    b_last = _dot_exact_lhs(_tri(tg, "all", L), lf)
    gb_ref[0] = b
    gw_ref[0] = b_last - b + ig
    ga_ref[0] = ig - b


def _mlstm_gates(h3, w_gate, b_gate, *, tg=256):
    b, t_len, d = h3.shape
    nh = MLSTM_HEADS
    tg = min(tg, t_len)
    wi = _pad_cols(jnp.concatenate([w_gate[0, :, :nh], w_gate[1, :, :nh]], axis=1), LANES)
    wf = _pad_cols(jnp.concatenate([w_gate[0, :, nh:], w_gate[1, :, nh:]], axis=1), LANES)
    bi = _pad_cols(jnp.concatenate([b_gate[0, :nh], b_gate[1, :nh]]).reshape(1, 2 * nh), LANES)
    bf = _pad_cols(jnp.concatenate([b_gate[0, nh:], b_gate[1, nh:]]).reshape(1, 2 * nh), LANES)
    full = lambda shape: pl.BlockSpec(shape, lambda i, c: (0,) * len(shape))
    tok = pl.BlockSpec((1, tg, LANES), lambda i, c: (i, c, 0))
    return _call(_mlstm_gate_kernel, grid=(b, t_len // tg),
                 in_specs=[pl.BlockSpec((1, tg, d), lambda i, c: (i, c, 0)),
                           full((d, LANES)), full((d, LANES)), full((1, LANES)), full((1, LANES))],
                 out_specs=[tok, tok, tok], out_shape=[jax.ShapeDtypeStruct((b, t_len, LANES), F32)] * 3,
                 sem=("parallel", "parallel"), name="mlstm_gates")(h3, wi, wf, bi, bf)


def _mlstm_scan_kernel(qkf_ref, qkb_ref, vf_ref, vb_ref, gbf_ref, gbb_ref, gwf_ref, gwb_ref, gaf_ref, gab_ref,
                       c0_ref, n0_ref, m0_ref, yf_ref, yb_ref, cn_ref, nn_ref, mn_ref, c_s, n_s, m_s):
    nh, L, dk, dv = MLSTM_HEADS, MLSTM_CHUNK, MLSTM_DK, MLSTM_DV
    c = pl.program_id(1)

    @pl.when(c == 0)
    def _():
        c_s[...] = c0_ref[...]
        n_s[...] = n0_ref[...]
        m_s[...] = m0_ref[...]

    ti = lax.broadcasted_iota(jnp.int32, (L, L), 0)
    si = lax.broadcasted_iota(jnp.int32, (L, L), 1)
    lane = lax.broadcasted_iota(jnp.int32, (L, LANES), 1)
    ones = jnp.ones((L, LANES), BF16)
    refs = ((qkf_ref, vf_ref, gbf_ref, gwf_ref, gaf_ref, yf_ref), (qkb_ref, vb_ref, gbb_ref, gwb_ref, gab_ref, yb_ref))
    chains = [(i, d, h) for i in range(c_s.shape[0]) for d in range(2) for h in range(nh)]
    st = {}
    for i, d, h in chains:
        qk_ref, v_ref, gb_ref, gw_ref, ga_ref, _ = refs[d]
        g = d * nh + h
        q = qk_ref[i, :, h * dk:(h + 1) * dk] * (dk ** -0.5)
        k = qk_ref[i, :, (nh + h) * dk:(nh + h + 1) * dk]
        v = v_ref[i, :, h * dv:(h + 1) * dv]
        a_row = sum(lax.dot_general(ones, piece, (((1,), (1,)), ((), ())), preferred_element_type=F32)
                    for piece in _split_bf16(jnp.where(lane == g, ga_ref[i], 0.0), 3))
        b_col = gb_ref[i, :, g:g + 1]
        wl_col = gw_ref[i, :, g:g + 1]
        st[i, d, h] = dict(q=q, k=k, v=v, a_row=a_row, b_col=b_col, wl_col=wl_col, qk=_dot_nt(q, k))
    for i, d, h in chains:
        z = st[i, d, h]
        m_st = m_s[i, d, h]
        mask = (si <= ti) if d == 0 else (si >= ti)
        log_d = jnp.where(mask, z["b_col"] + z["a_row"], -jnp.inf)
        m_t = jnp.maximum(z["b_col"] + m_st, jnp.max(log_d, axis=-1, keepdims=True))
        z.update(m_t=m_t, s=z["qk"] * jnp.exp(log_d - m_t), w_inter=jnp.exp(z["b_col"] + m_st - m_t))
    for i, d, h in chains:
        z = st[i, d, h]
        q, s, w_inter = z["q"], z["s"], z["w_inter"]
        num = _dot(s, z["v"]) + w_inter * _dot(q, c_s[i, d, h])
        den = jnp.sum(s, axis=-1, keepdims=True) + w_inter * jnp.sum(q * n_s[i, d, h], axis=-1, keepdims=True)
        refs[d][5][i, :, h * dv:(h + 1) * dv] = num / jnp.maximum(jnp.abs(den), jnp.exp(-z["m_t"]))
    for i, d, h in chains:
        z = st[i, d, h]
        last = L - 1 if d == 0 else 0
        m_st = m_s[i, d, h]
        b_last = z["b_col"][last:last + 1, :]
        m_new = jnp.maximum(b_last + m_st, jnp.max(z["wl_col"], axis=0, keepdims=True))
        ks = jnp.exp(z["wl_col"] - m_new) * z["k"]
        decay = jnp.exp(b_last + m_st - m_new)
        c_s[i, d, h] = decay * c_s[i, d, h] + _dot_tn(ks, z["v"])
        n_s[i, d, h] = decay * n_s[i, d, h] + jnp.sum(ks, axis=0, keepdims=True)
        m_s[i, d, h] = m_new

    @pl.when(c == pl.num_programs(1) - 1)
    def _():
        cn_ref[...] = c_s[...]
        nn_ref[...] = n_s[...]
        mn_ref[...] = m_s[...]


def _mlstm_scan(qk, big, gb, gw, ga, c0, n0, m0, *, nb=SCAN_BATCH_ROWS):
    b, t_len, _ = qk.shape
    nh, L, dk, dv = MLSTM_HEADS, MLSTM_CHUNK, MLSTM_DK, MLSTM_DV
    nc = t_len // L
    fwd = lambda i, c: (i, c, 0)
    bwd = lambda i, c: (i, nc - 1 - c, 0)
    state = lambda shape: pl.BlockSpec((nb,) + shape, lambda i, c: (i,) + (0,) * len(shape))
    st_shapes = [(2, nh, dk, dv), (2, nh, 1, dk), (2, nh, 1, 1)]
    return _call(
        _mlstm_scan_kernel, grid=(b // nb, nc),
        in_specs=[pl.BlockSpec((nb, L, 2 * nh * dk), fwd), pl.BlockSpec((nb, L, 2 * nh * dk), bwd),
                  pl.BlockSpec((nb, L, nh * dv), lambda i, c: (i, c, 1)),
                  pl.BlockSpec((nb, L, nh * dv), lambda i, c: (i, nc - 1 - c, 1)),
                  pl.BlockSpec((nb, L, LANES), fwd), pl.BlockSpec((nb, L, LANES), bwd),
                  pl.BlockSpec((nb, L, LANES), fwd), pl.BlockSpec((nb, L, LANES), bwd),
                  pl.BlockSpec((nb, L, LANES), fwd), pl.BlockSpec((nb, L, LANES), bwd)] + [state(s) for s in st_shapes],
        out_specs=[pl.BlockSpec((nb, L, nh * dv), fwd), pl.BlockSpec((nb, L, nh * dv), bwd)] + [state(s) for s in st_shapes],
        out_shape=[jax.ShapeDtypeStruct((b, t_len, nh * dv), F32)] * 2 + [jax.ShapeDtypeStruct((b,) + s, F32) for s in st_shapes],
        scratch=[pltpu.VMEM((nb,) + s, F32) for s in st_shapes],
        sem=("parallel", "arbitrary"), name="mlstm_scan")(qk, qk, big, big, gb, gb, gw, gw, ga, ga, c0, n0, m0)


def _mlstm_post_kernel(yf_ref, yb_ref, o_ref, g_ref, out_ref):
    dv = MLSTM_DV
    for h in range(MLSTM_HEADS):
        sl = slice(h * dv, (h + 1) * dv)
        y = yf_ref[:, sl] + yb_ref[:, sl]
        y = y - jnp.mean(y, axis=-1, keepdims=True)
        y = y * lax.rsqrt(jnp.mean(y * y, axis=-1, keepdims=True) + NORM_EPS)
        out_ref[:, sl] = (_sigmoid(o_ref[:, sl].astype(F32)) * (y * g_ref[:, sl])).astype(BF16)


def _mlstm_post(yf, yb, big, norm_g, *, tm=512):
    m, d = yf.shape
    tm = min(tm, m)
    row = lambda i: (i, 0)
    return _call(_mlstm_post_kernel, grid=(m // tm,),
                 in_specs=[pl.BlockSpec((tm, d), row), pl.BlockSpec((tm, d), row),
                           pl.BlockSpec((tm, d), lambda i: (i, 2)), pl.BlockSpec((1, d), lambda i: (0, 0))],
                 out_specs=pl.BlockSpec((tm, d), row), out_shape=jax.ShapeDtypeStruct((m, d), BF16),
                 sem=("parallel",), name="mlstm_post")(yf, yb, big, norm_g.reshape(1, d))


def _mlstm_mixer(h, b, rows, init, w_in, conv_w, w_gate, b_gate, norm_g, w_out, x, gate1):
    m, d = h.shape
    t_len = m // b
    nh, dk = MLSTM_HEADS, MLSTM_DK
    big = _mm(h, w_in, tm=1024, tn=1024, out_dtype=BF16, name="mlstm_in")
    big3 = big.reshape(b, t_len, 3 * d)
    qk = _conv_silu(big3, conv_w, rows=rows, width=t_len // rows, channels=2 * nh * dk)
    gb, gw, ga = _mlstm_gates(h.reshape(b, t_len, d), w_gate, b_gate)
    c0, n0, m0 = init
    yf, yb, cn, nn, mn = _mlstm_scan(qk, big3, gb, gw, ga, c0, n0.reshape(b, 2, nh, 1, dk), m0.reshape(b, 2, nh, 1, 1))
    y = _mlstm_post(yf.reshape(m, d), yb.reshape(m, d), big, norm_g)
    x = _mm(y, w_out, tm=512, tn=1024, extras=[x, gate1],
            epilogue=_residual_epilogue, name="mlstm_out")
    return x, (cn, nn.reshape(b, 2, nh, dk), mn.reshape(b, 2, nh))


def _gla_gate_kernel(h_ref, w1_ref, w2_ref, b_ref, gf_ref, gb_ref):
    half = GLA_HEADS * GLA_DK
    h = h_ref[0]
    z = _dot(_dot(h, w1_ref[...]), w2_ref[...]) + b_ref[...]
    lg = _logsig(z) * (1.0 / GLA_TAU)
    tg = h.shape[0]
    gf_ref[0] = _dot_exact_lhs(_tri(tg, "lower", GLA_CHUNK), lg[:, :half])
    gb_ref[0] = _dot_exact_lhs(_tri(tg, "upper", GLA_CHUNK), lg[:, half:])


def _gla_gates(h3, w_a1, w_a2, b_a, *, tg=256):
    b, t_len, d = h3.shape
    rank, half = w_a1.shape[-1], w_a2.shape[-1]
    w1 = jnp.zeros((d, LANES), F32).at[:, :rank].set(w_a1[0]).at[:, rank:2 * rank].set(w_a1[1])
    w2 = jnp.zeros((LANES, 2 * half), F32).at[:rank, :half].set(w_a2[0]).at[rank:2 * rank, half:].set(w_a2[1])
    bias = jnp.concatenate([b_a[0], b_a[1]]).reshape(1, 2 * half)
    tg = min(tg, t_len)
    full = lambda shape: pl.BlockSpec(shape, lambda i, c: (0,) * len(shape))
    tok = lambda n: pl.BlockSpec((1, tg, n), lambda i, c: (i, c, 0))
    return _call(_gla_gate_kernel, grid=(b, t_len // tg),
                 in_specs=[tok(d), full((d, LANES)), full((LANES, 2 * half)), full((1, 2 * half))],
                 out_specs=[tok(half), tok(half)],
                 out_shape=[jax.ShapeDtypeStruct((b, t_len, half), F32)] * 2,
                 sem=("parallel", "parallel"), name="gla_gates")(h3, w1, w2, bias)


def _gla_scan_kernel(xf_ref, xb_ref, gf_ref, gb_ref, s0_ref, yf_ref, yb_ref, sn_ref, s_s):
    nh, L, dk, dv = GLA_HEADS, GLA_CHUNK, GLA_DK, GLA_DV
    c = pl.program_id(1)

    @pl.when(c == 0)
    def _():
        s_s[...] = s0_ref[0]

    tcol = lax.broadcasted_iota(jnp.int32, (L, 1), 0)
    eye = jnp.where(lax.broadcasted_iota(jnp.int32, (dk, dk), 0) == lax.broadcasted_iota(jnp.int32, (dk, dk), 1), 1.0, 0.0)
    refs = ((xf_ref, gf_ref, yf_ref), (xb_ref, gb_ref, yb_ref))
    chains = [(d, h) for d in range(2) for h in range(nh)]
    n_sub = xf_ref.shape[1] // L
    for step in range(n_sub):
        st = {}
        for d, h in chains:
            x_ref, g_ref, _ = refs[d]
            rows = slice((step if d == 0 else n_sub - 1 - step) * L, (step + 1 if d == 0 else n_sub - step) * L)
            q = x_ref[0, rows, h * dk:(h + 1) * dk] * (dk ** -0.5)
            k = x_ref[0, rows, (nh + h) * dk:(nh + h + 1) * dk]
            v = x_ref[0, rows, 2 * nh * dk + h * dv:2 * nh * dk + (h + 1) * dv]
            g = g_ref[0, rows, h * dk:(h + 1) * dk]
            st[d, h] = dict(rows=rows, q=q, k=k, v=v, g=g, o=_dot(q * jnp.exp(g), s_s[d, h]))
        sub = 8
        for s in range(L):
            for d, h in chains:
                z = st[d, h]
                q, k, v, g, o = z["q"], z["k"], z["v"], z["g"], z["o"]
                lo, hi = ((s // sub) * sub, L) if d == 0 else (0, (s // sub + 1) * sub)
                dec = jnp.exp(jnp.minimum(g[lo:hi] - g[s:s + 1, :], 0.0))
                col = jnp.sum(q[lo:hi] * k[s:s + 1, :] * dec, axis=-1, keepdims=True)
                col = jnp.where((tcol[lo:hi] >= s) if d == 0 else (tcol[lo:hi] <= s), col, 0.0)
                parts = [o[:lo]] * (lo > 0) + [o[lo:hi] + col * v[s:s + 1, :]] + [o[hi:]] * (hi < L)
                z["o"] = parts[0] if len(parts) == 1 else jnp.concatenate(parts, axis=0)
        for d, h in chains:
            z = st[d, h]
            k, v, g = z["k"], z["v"], z["g"]
            last = L - 1 if d == 0 else 0
            refs[d][2][0, z["rows"], h * dv:(h + 1) * dv] = z["o"]
            g_last = g[last:last + 1, :]
            decay_col = jnp.sum(eye * jnp.exp(g_last), axis=-1, keepdims=True)
            s_s[d, h] = decay_col * s_s[d, h] + _dot_tn(k * jnp.exp(g_last - g), v)

    @pl.when(c == pl.num_programs(1) - 1)
    def _():
        sn_ref[0] = s_s[...]


def _gla_scan(qkv, gf, gb, s0):
    b, t_len, width = qkv.shape
    nh, L, dk, dv = GLA_HEADS, GLA_BLOCK, GLA_DK, GLA_DV
    nc = t_len // L
    fwd = lambda i, c: (i, c, 0)
    bwd = lambda i, c: (i, nc - 1 - c, 0)
    st = (2, nh, dk, dv)
    state = pl.BlockSpec((1,) + st, lambda i, c: (i, 0, 0, 0, 0))
    return _call(
        _gla_scan_kernel, grid=(b, nc),
        in_specs=[pl.BlockSpec((1, L, width), fwd), pl.BlockSpec((1, L, width), bwd),
                  pl.BlockSpec((1, L, nh * dk), fwd), pl.BlockSpec((1, L, nh * dk), bwd), state],
        out_specs=[pl.BlockSpec((1, L, nh * dv), fwd), pl.BlockSpec((1, L, nh * dv), bwd), state],
        out_shape=[jax.ShapeDtypeStruct((b, t_len, nh * dv), F32)] * 2 + [jax.ShapeDtypeStruct((b,) + st, F32)],
        scratch=[pltpu.VMEM(st, F32)], sem=("parallel", "arbitrary"), name="gla_scan")(qkv, qkv, gf, gb, s0)


def _gla_post_kernel(yf_ref, yb_ref, r_ref, g_ref, out_ref):
    dv = GLA_DV
    for h in range(GLA_HEADS):
        sl = slice(h * dv, (h + 1) * dv)
        y = yf_ref[:, sl] + yb_ref[:, sl]
        y = y * lax.rsqrt(jnp.mean(y * y, axis=-1, keepdims=True) + NORM_EPS)
        out_ref[:, sl] = ((y * g_ref[:, sl]) * _silu(r_ref[:, sl].astype(F32))).astype(BF16)


def _gla_post(yf, yb, big, norm_g, *, tm=512):
    m, d = yf.shape
    tm = min(tm, m)
    row = lambda i: (i, 0)
    return _call(_gla_post_kernel, grid=(m // tm,),
                 in_specs=[pl.BlockSpec((tm, d), row), pl.BlockSpec((tm, d), row),
                           pl.BlockSpec((tm, d), lambda i: (i, 2)), pl.BlockSpec((1, d), lambda i: (0, 0))],
                 out_specs=pl.BlockSpec((tm, d), row), out_shape=jax.ShapeDtypeStruct((m, d), BF16),
                 sem=("parallel",), name="gla_post")(yf, yb, big, norm_g.reshape(1, d))


def _gla_mixer(h, b, rows, init, w_in, conv_w, w_a1, w_a2, b_a, norm_g, w_out, x, gate1):
    m, d = h.shape
    t_len = m // b
    big = _mm(h, w_in, tm=1024, tn=1024, out_dtype=BF16, name="gla_in")
    big3 = big.reshape(b, t_len, 3 * d)
    qkv = _conv_silu(big3, conv_w, rows=rows, width=t_len // rows, channels=2 * d)
    gf, gb = _gla_gates(h.reshape(b, t_len, d), w_a1, w_a2, b_a)
    yf, yb, sn = _gla_scan(qkv, gf, gb, init[0])
    y = _gla_post(yf.reshape(m, d), yb.reshape(m, d), big, norm_g)
    x = _mm(y, w_out, tm=512, tn=1024, extras=[x, gate1],
            epilogue=_residual_epilogue, name="gla_out")
    return x, (sn,)


def _rwkv_mix_kernel(h_ref, hp_ref, hn_ref, mu_ref, o_ref):
    i = pl.program_id(1)
    h = h_ref[0]
    tm = h.shape[0]
    prev_row = jnp.where(i > 0, hp_ref[0, 7:8, :], 0.0)
    next_row = jnp.where(i < pl.num_programs(1) - 1, hn_ref[0, 0:1, :], 0.0)
    row = lax.broadcasted_iota(jnp.int32, (tm, 1), 0)
    x_prev = jnp.where(row == 0, prev_row, pltpu.roll(h, 1, axis=0))
    x_next = jnp.where(row == tm - 1, next_row, pltpu.roll(h, tm - 1, axis=0))
    xx = 0.5 * (x_prev + x_next) - h
    for j in range(6):
        o_ref[j, 0] = (h + xx * mu_ref[j:j + 1, :]).astype(BF16)


def _rwkv_mix(h3, mu, *, tm=256):
    b, t_len, d = h3.shape
    tm = min(tm, t_len)
    nt, n8 = t_len // tm, t_len // 8
    return _call(_rwkv_mix_kernel, grid=(b, nt),
                 in_specs=[pl.BlockSpec((1, tm, d), lambda i, j: (i, j, 0)),
                           pl.BlockSpec((1, 8, d), lambda i, j: (i, jnp.maximum(j * (tm // 8) - 1, 0), 0)),
                           pl.BlockSpec((1, 8, d), lambda i, j: (i, jnp.minimum((j + 1) * (tm // 8), n8 - 1), 0)),
                           pl.BlockSpec((6, d), lambda i, j: (0, 0))],
                 out_specs=pl.BlockSpec((6, 1, tm, d), lambda i, j: (0, i, j, 0)),
                 out_shape=jax.ShapeDtypeStruct((6, b, t_len, d), BF16),
                 sem=("parallel", "parallel"), name="rwkv_mix")(h3, h3, h3, mu)


def _segsum(x):
    lo = lax.broadcasted_iota(jnp.int32, x.shape, 1) < RWKV_N
    s0 = jnp.sum(jnp.where(lo, x, 0.0), axis=-1, keepdims=True)
    s1 = jnp.sum(jnp.where(lo, 0.0, x), axis=-1, keepdims=True)
    return jnp.where(lo, s0, s1)


def _rwkv_prep_kernel(r_ref, k_ref, v_ref, lora_ref, ap_ref, w0_ref, a0_ref, kk_w_ref, ka_w_ref, rk_w_ref,
                      kk_ref, kb_ref, k2_ref, bonus_ref, lw_ref):
    d = D_MODEL
    for j in range(d // LANES):
        sl = slice(j * LANES, (j + 1) * LANES)
        r, k, v = r_ref[:, sl], k_ref[:, sl], v_ref[:, sl]
        a = _sigmoid(a0_ref[:, sl] + ap_ref[:, sl])
        kk = k * kk_w_ref[:, sl]
        kk = kk / jnp.maximum(jnp.sqrt(_segsum(kk * kk)), 1e-12)
        k2 = k * (1.0 + (a - 1.0) * ka_w_ref[:, sl])
        kk_ref[:, sl] = kk
        kb_ref[:, sl] = kk * a
        k2_ref[:, sl] = k2
        bonus_ref[:, sl] = _segsum(r * k2 * rk_w_ref[:, sl]) * v
        for z in range(2):
            lw_ref[z, :, sl] = -jnp.exp(_logsig(w0_ref[z:z + 1, sl] + lora_ref[:, z * d + j * LANES:z * d + (j + 1) * LANES]) - 0.5)


def _rwkv_prep(r, k, v, lora, a_pre, w0, a0, k_k, k_a, r_k, *, tm=256):
    m, d = r.shape
    tm = min(tm, m)
    row = lambda i: (i, 0)
    tok = pl.BlockSpec((tm, d), row)
    par = pl.BlockSpec((1, d), lambda i: (0, 0))
    return _call(_rwkv_prep_kernel, grid=(m // tm,),
                 in_specs=[tok, tok, tok, pl.BlockSpec((tm, 2 * d), row), tok,
                           pl.BlockSpec((2, d), lambda i: (0, 0)), par, par, par, par],
                 out_specs=[tok, tok, tok, tok, pl.BlockSpec((2, tm, d), lambda i: (0, i, 0))],
                 out_shape=[jax.ShapeDtypeStruct((m, d), F32)] * 4 + [jax.ShapeDtypeStruct((2, m, d), F32)],
                 sem=("parallel",), name="rwkv_prep")(
                     r, k, v, lora, a_pre, w0, a0.reshape(1, d), k_k.reshape(1, d), k_a.reshape(1, d), r_k.reshape(1, d))


def _rwkv_scan_kernel(rf, kf, vf, kkf, kbf, lwf, rb, kb_, vb, kkb, kbb, lwb, s0_ref, yf_ref, yb_ref, sn_ref, s_s):
    L, pairs = RWKV_CHUNK, RWKV_PAIRS
    c = pl.program_id(1)

    @pl.when(c == 0)
    def _():
        s_s[...] = s0_ref[...]

    row = lax.broadcasted_iota(jnp.int32, (L, LANES), 0)
    lane = lax.broadcasted_iota(jnp.int32, (L, LANES), 1)
    lane_tok = jnp.bitwise_and(lane, L - 1)
    lane_head0 = lane - lane_tok
    lo = lane < RWKV_N
    sq_r = lax.broadcasted_iota(jnp.int32, (LANES, LANES), 0) < RWKV_N
    sq_c = lax.broadcasted_iota(jnp.int32, (LANES, LANES), 1) < RWKV_N
    same_head = sq_r == sq_c
    zeros64 = jnp.zeros((2 * L, LANES), F32)
    eye_cat = jnp.where(lane_tok == row, 1.0, 0.0)

    def head_rows(a):
        return jnp.concatenate([jnp.where(lo, a, 0.0), jnp.where(lo, 0.0, a)], axis=0)

    def group_rows(a0, a1):
        return jnp.concatenate([jnp.concatenate([head_rows(a0), zeros64], axis=1),
                                jnp.concatenate([zeros64, head_rows(a1)], axis=1)], axis=0)

    refs = ((rf, kf, vf, kkf, kbf, lwf, yf_ref), (rb, kb_, vb, kkb, kbb, lwb, yb_ref))
    groups = [(i, d, grp) for i in range(s_s.shape[0]) for d in range(2) for grp in range(pairs // 2)]
    st = {}
    for i, d, grp in groups:
        r_ref, k_ref, v_ref, kk_ref, kb_ref, lw_ref, _ = refs[d]
        tri = _tri(L, "lower" if d == 0 else "upper")
        last = L - 1 if d == 0 else 0
        rt, kt, kd, bd, v, e_last, m_b, m_k = [], [], [], [], [], [], 0.0, 0.0
        for q in range(2):
            sl = slice((2 * grp + q) * LANES, (2 * grp + q + 1) * LANES)
            r, k, kk, kb, lw = r_ref[i, :, sl], k_ref[i, :, sl], kk_ref[i, :, sl], kb_ref[i, :, sl], lw_ref[0, i, :, sl]
            cum = _dot_exact_lhs(tri, lw)
            c_last = cum[last:last + 1, :]
            e_neg, e_end = jnp.exp(-cum), jnp.exp(c_last - cum)
            rt.append(r * jnp.exp(cum))
            kt.append(kk * jnp.exp(cum - lw))
            kd.append(k * e_end)
            bd.append(kb * e_end)
            v.append(v_ref[i, :, sl])
            e_last.append(jnp.exp(c_last))
            lhs = jnp.concatenate([kt[q], rt[q]], axis=0)
            pad = lambda a: jnp.concatenate([head_rows(a), zeros64] if q == 0 else [zeros64, head_rows(a)], axis=0)
            m_b = m_b + _dot_nt(lhs, pad(kb * e_neg))
            m_k = m_k + _dot_nt(lhs, pad(k * e_neg))
        st[i, d, grp] = dict(rt=rt, kt=kt, kd=kd, bd=bd, v=v, e_last=e_last, m_b=m_b, m_k=m_k)
    for i, d, grp in groups:
        z = st[i, d, grp]
        strict = (lane_tok < row) if d == 0 else (lane_tok > row)
        incl = (lane_tok <= row) if d == 0 else (lane_tok >= row)
        m_b, m_k = z["m_b"], z["m_k"]
        z["n_kb"] = jnp.where(strict, m_b[:L], 0.0)
        z["m_rb"] = jnp.where(incl, m_b[L:], 0.0)
        m_kk_rk = jnp.concatenate([jnp.where(strict, m_k[:L], 0.0), jnp.where(incl, m_k[L:], 0.0)], axis=0)
        z["kv"] = _dot(m_kk_rk, group_rows(z["v"][0], z["v"][1]))
        z["x"] = eye_cat
    for step in range(L - 1):
        for i, d, grp in groups:
            z = st[i, d, grp]
            u = step if d == 0 else L - 1 - step
            coef = jnp.take_along_axis(z["n_kb"], lane_head0 + u, axis=1)
            z["x"] = z["x"] - coef * z["x"][u:u + 1, :]
    for i, d, grp in groups:
        z = st[i, d, grp]
        x, kt, kv = z["x"], z["kt"], z["kv"]
        x_hi = x.astype(BF16)
        x_lo = x - x_hi.astype(F32)
        rhs = jnp.concatenate([group_rows(kt[0], kt[1]), group_rows(kv[:L, :LANES], kv[:L, LANES:])], axis=1)
        z["w"] = _dot(x_hi, rhs) + _dot(x_lo, rhs)
    for i, d, grp in groups:
        z = st[i, d, grp]
        w = z["w"]
        z["sa"] = [_dot_nt(w[:, q * LANES:(q + 1) * LANES], s_s[i, d, 2 * grp + q]) + w[:, (2 + q) * LANES:(3 + q) * LANES]
                   for q in range(2)]
    for i, d, grp in groups:
        z = st[i, d, grp]
        rb_sa = _dot(z["m_rb"], group_rows(z["sa"][0], z["sa"][1]))
        for q in range(2):
            p = 2 * grp + q
            cols = slice(q * LANES, (q + 1) * LANES)
            refs[d][6][i, :, p * LANES:(p + 1) * LANES] = _dot_nt(z["rt"][q], s_s[i, d, p]) + z["kv"][L:, cols] - rb_sa[:, cols]
    for i, d, grp in groups:
        z = st[i, d, grp]
        for q in range(2):
            p = 2 * grp + q
            s_new = s_s[i, d, p] * z["e_last"][q] + _dot_tn(z["v"][q], z["kd"][q]) - _dot_tn(z["sa"][q], z["bd"][q])
            s_s[i, d, p] = jnp.where(same_head, s_new, 0.0)

    @pl.when(c == pl.num_programs(1) - 1)
    def _():
        sn_ref[...] = s_s[...]


def _rwkv_scan(r, k2, v, kk, kb, lw, s0, *, nb=SCAN_BATCH_ROWS):
    b, t_len, d = r.shape
    L, pairs = RWKV_CHUNK, RWKV_PAIRS
    nc = t_len // L
    fwd = pl.BlockSpec((nb, L, d), lambda i, c: (i, c, 0))
    bwd = pl.BlockSpec((nb, L, d), lambda i, c: (i, nc - 1 - c, 0))
    st = (2, pairs, LANES, LANES)
    state = pl.BlockSpec((nb,) + st, lambda i, c: (i, 0, 0, 0, 0))
    return _call(
        _rwkv_scan_kernel, grid=(b // nb, nc),
        in_specs=[fwd] * 5 + [pl.BlockSpec((1, nb, L, d), lambda i, c: (0, i, c, 0))]
        + [bwd] * 5 + [pl.BlockSpec((1, nb, L, d), lambda i, c: (1, i, nc - 1 - c, 0)), state],
        out_specs=[fwd, bwd, state],
        out_shape=[jax.ShapeDtypeStruct((b, t_len, d), F32)] * 2 + [jax.ShapeDtypeStruct((b,) + st, F32)],
        scratch=[pltpu.VMEM((nb,) + st, F32)], sem=("parallel", "arbitrary"), name="rwkv_scan")(
            r, k2, v, kk, kb, lw, r, k2, v, kk, kb, lw, s0)


def _rwkv_post_kernel(yf_ref, yb_ref, bonus_ref, g_ref, lng_ref, lnb_ref, out_ref):
    for j in range(D_MODEL // LANES):
        sl = slice(j * LANES, (j + 1) * LANES)
        y = yf_ref[:, sl] + yb_ref[:, sl]
        y = y - _segsum(y) * (1.0 / RWKV_N)
        y = y * lax.rsqrt(_segsum(y * y) * (1.0 / RWKV_N) + RWKV_GN_EPS)
        out_ref[:, sl] = ((y * lng_ref[:, sl] + lnb_ref[:, sl] + bonus_ref[:, sl]) * g_ref[:, sl]).astype(BF16)


def _rwkv_post(yf, yb, bonus, g, ln_g, ln_b, *, tm=256):
    m, d = yf.shape
    tm = min(tm, m)
    tok = pl.BlockSpec((tm, d), lambda i: (i, 0))
    par = pl.BlockSpec((1, d), lambda i: (0, 0))
    return _call(_rwkv_post_kernel, grid=(m // tm,), in_specs=[tok, tok, tok, tok, par, par], out_specs=tok,
                 out_shape=jax.ShapeDtypeStruct((m, d), BF16), sem=("parallel",), name="rwkv_post")(
                     yf, yb, bonus, g, ln_g.reshape(1, d), ln_b.reshape(1, d))


def _pad_cols(w, n):
    return jnp.zeros(w.shape[:-1] + (n,), w.dtype).at[..., :w.shape[-1]].set(w)


def _pad_rows(w, n):
    return jnp.zeros((n,) + w.shape[1:], w.dtype).at[:w.shape[0]].set(w)


def _rwkv_lowrank_kernel(xw_ref, xa_ref, xg_ref, w1_ref, w2_ref, a1_ref, a2_ref, g1_ref, g2_ref, lora_ref, ap_ref, g_ref):
    lora_ref[...] = _dot(jnp.tanh(_dot(xw_ref[...], w1_ref[...])), w2_ref[...])
    ap_ref[...] = _dot(_dot(xa_ref[...], a1_ref[...]), a2_ref[...])
    g_ref[...] = _dot(_sigmoid(_dot(xg_ref[...], g1_ref[...])), g2_ref[...])


def _rwkv_lowrank(xmix, w1, w2, a1, a2, g1, g2, *, tm=512):
    _, m, d = xmix.shape
    tm = min(tm, m)
    mix = lambda idx: pl.BlockSpec((None, tm, d), lambda i: (idx, i, 0))
    full = lambda w: pl.BlockSpec(w.shape, lambda i: (0, 0))
    tok = lambda n: pl.BlockSpec((tm, n), lambda i: (i, 0))
    return _call(_rwkv_lowrank_kernel, grid=(m // tm,),
                 in_specs=[mix(1), mix(4), mix(5)] + [full(w) for w in (w1, w2, a1, a2, g1, g2)],
                 out_specs=[tok(2 * d), tok(d), tok(d)],
                 out_shape=[jax.ShapeDtypeStruct((m, 2 * d), F32), jax.ShapeDtypeStruct((m, d), F32),
                            jax.ShapeDtypeStruct((m, d), F32)],
                 sem=("parallel",), name="rwkv_lowrank")(xmix, xmix, xmix, w1, w2, a1, a2, g1, g2)


def _rwkv_mixer(h, b, init, j, mu, w_rkv, w0, w1, w2, a0, a1, a2, g1, g2, k_k, k_a, r_k, ln_g, ln_b, w_out, x, gate1):
    m, d = h.shape
    t_len = m // b
    xmix = _rwkv_mix(h.reshape(b, t_len, d), mu[j]).reshape(6, m, d)
    proj = lambda idx, w, wlead, name, **kw: _mm(xmix, w, xlead=(idx,), wlead=wlead, tm=512, tn=1024, name=name, **kw)
    r = proj(0, w_rkv, (j, 0), "rwkv_r")
    k = proj(2, w_rkv, (j, 1), "rwkv_k")
    v = proj(3, w_rkv, (j, 2), "rwkv_v")
    rank_w = w1.shape[-1]
    w1cat = jnp.concatenate([w1[j, 0], w1[j, 1]], axis=1)
    w2bd = jnp.zeros((2 * rank_w, 2 * d), F32).at[:rank_w, :d].set(w2[j, 0]).at[rank_w:, d:].set(w2[j, 1])
    lora, a_pre, g = _rwkv_lowrank(xmix, w1cat, w2bd, _pad_cols(a1[j], LANES), _pad_rows(a2[j], LANES), g1[j], g2[j])
    kk, kb, k2, bonus, lw = _rwkv_prep(r, k, v, lora, a_pre, w0[j], a0[j], k_k[j], k_a[j], r_k[j].reshape(d))
    s3 = lambda z: z.reshape(b, t_len, d)
    yf, yb, sn = _rwkv_scan(s3(r), s3(k2), s3(v), s3(kk), s3(kb), lw.reshape(2, b, t_len, d), init)
    y = _rwkv_post(yf.reshape(m, d), yb.reshape(m, d), bonus, g, ln_g[j], ln_b[j])
    x = _mm(y, w_out, wlead=(j,), tm=512, tn=1024, extras=[x, gate1],
            epilogue=_residual_epilogue, name="rwkv_out")
    return x, sn


def _rwkv_state_to_pairs(s):
    b = s.shape[0]
    n = RWKV_N
    out = jnp.zeros((b, 2, RWKV_PAIRS, 2 * n, 2 * n), F32)
    return out.at[..., :n, :n].set(s[:, :, 0::2]).at[..., n:, n:].set(s[:, :, 1::2])


def _rwkv_state_from_pairs(sp):
    n = RWKV_N
    b = sp.shape[0]
    return jnp.stack([sp[..., :n, :n], sp[..., n:, n:]], axis=3).reshape(b, 2, RWKV_HEADS, n, n)


def _router_kernel(x_ref, w_ref, b_ref, o_ref, cnt_ref, seen, *, tiles_per_block):
    e0, per = ROUTER_EXPERT_LANE0, MOE_PER_GROUP

    @pl.when(pl.program_id(0) % tiles_per_block == 0)
    def _():
        seen[...] = jnp.zeros_like(seen)

    logits = _dot_3pass(x_ref[...], w_ref[...]) + b_ref[...]
    lane = lax.broadcasted_iota(jnp.int32, logits.shape, 1).astype(F32)
    big = float(LANES)
    rmax = lambda z: jnp.max(z, axis=-1, keepdims=True)
    rsum = lambda z: jnp.sum(z, axis=-1, keepdims=True)
    first = lambda m: jnp.min(jnp.where(m, lane, big), axis=-1, keepdims=True)
    is_g = lane < MOE_GROUPS
    gmax = rmax(jnp.where(is_g, logits, -jnp.inf))
    gsel = first(jnp.logical_and(is_g, logits >= gmax))
    gw = 1.0 / rsum(jnp.where(is_g, jnp.exp(logits - gmax), 0.0))
    in_grp = jnp.logical_and(lane >= e0 + gsel * per, lane < e0 + (gsel + 1) * per)
    emax = rmax(jnp.where(in_grp, logits, -jnp.inf))
    p = jnp.where(in_grp, jnp.exp(logits - emax), 0.0)
    p = p / rsum(p)
    p1 = rmax(jnp.where(in_grp, p, -1.0))
    i1 = first(jnp.logical_and(in_grp, p >= p1))
    rest = jnp.logical_and(in_grp, lane != i1)
    p2 = rmax(jnp.where(rest, p, -1.0))
    i2 = first(jnp.logical_and(rest, p >= p2))
    tot = p1 + p2
    comb = jnp.where(lane == i1, p1 / tot * gw, jnp.where(lane == i2, p2 / tot * gw, 0.0))
    onehot = jnp.where(lane == gsel, 1.0, 0.0)
    csum = seen[...] + jnp.dot(_tri(onehot.shape[0], "lower").astype(BF16), onehot.astype(BF16), preferred_element_type=F32)
    rank = rsum(onehot * csum) - 1.0
    seen[...] = csum[onehot.shape[0] - 1:, :]
    cnt_ref[0] = seen[...]
    o_ref[...] = jnp.where(lane == ROUTER_GROUP_LANE, gsel, jnp.where(lane == ROUTER_RANK_LANE, rank, comb))


def _router(x, w_group, b_group, w_expert, b_expert, *, tm=512):
    m, d = x.shape
    tm = min(tm, m)
    tiles_per_block = min(MOE_BLOCK_TOKENS, m) // tm
    e0 = ROUTER_EXPERT_LANE0
    w = jnp.zeros((d, LANES), F32).at[:, :MOE_GROUPS].set(w_group).at[:, e0:e0 + MOE_EXPERTS].set(w_expert)
    bias = jnp.zeros((1, LANES), F32).at[0, :MOE_GROUPS].set(b_group).at[0, e0:e0 + MOE_EXPERTS].set(b_expert)
    return _call(functools.partial(_router_kernel, tiles_per_block=tiles_per_block), grid=(m // tm,),
                 in_specs=[pl.BlockSpec((tm, d), lambda i: (i, 0)), pl.BlockSpec((d, LANES), lambda i: (0, 0)),
                           pl.BlockSpec((1, LANES), lambda i: (0, 0))],
                 out_specs=[pl.BlockSpec((tm, LANES), lambda i: (i, 0)),
                            pl.BlockSpec((1, 1, LANES), lambda i: (i // tiles_per_block, 0, 0))],
                 out_shape=[jax.ShapeDtypeStruct((m, LANES), F32),
                            jax.ShapeDtypeStruct((m // tm // tiles_per_block, 1, LANES), F32)],
                 scratch=[pltpu.VMEM((1, LANES), F32)], sem=("arbitrary",), name="moe_router")(x, w, bias)


def _moe_dispatch(comb, counts):
    n = comb.shape[0]
    tb, tr = min(MOE_BLOCK_TOKENS, n), MOE_ROW_TILE
    nblk = n // tb
    cnt = counts[:, 0, :MOE_GROUPS].astype(jnp.int32)
    padded = (cnt + tr - 1) // tr * tr
    start = jnp.cumsum(padded, axis=1) - padded
    grp = comb[:, ROUTER_GROUP_LANE].astype(jnp.int32).reshape(nblk, tb)
    rank = comb[:, ROUTER_RANK_LANE].astype(jnp.int32).reshape(nblk, tb)
    onehot = grp[..., None] == jnp.arange(MOE_GROUPS, dtype=jnp.int32)
    pos = jnp.sum(jnp.where(onehot, start[:, None, :], 0), axis=-1) + rank
    meta = jnp.concatenate([start, padded // tr], axis=1).astype(jnp.int32)
    return pos.reshape(-1), meta.reshape(-1)


def _moe_sparse_kernel(pos_ref, meta_ref, h_ref, comb_ref, wg_ref, wu_ref, wd_ref, y_ref, xs, cs, acc):
    blk, e = pl.program_id(0), pl.program_id(1)
    tr, per = MOE_ROW_TILE, MOE_PER_GROUP
    tb = h_ref.shape[0]
    g, i = e // per, e % per
    base, mbase = blk * tb, blk * 2 * MOE_GROUPS

    @pl.when(jnp.logical_and(blk == 0, e == 0))
    def _():
        xs[...] = jnp.zeros_like(xs)
        cs[...] = jnp.zeros_like(cs)

    @pl.when(e == 0)
    def _():
        def body(j, carry):
            t0 = pl.multiple_of(j * 8, 8)
            rows8, comb8 = h_ref[pl.ds(t0, 8), :], comb_ref[pl.ds(t0, 8), :]
            for u in range(8):
                r = pos_ref[base + t0 + u]
                xs[pl.ds(r, 1), :] = rows8[u:u + 1, :]
                cs[pl.ds(r, 1), :] = comb8[u:u + 1, :]
            return carry

        lax.fori_loop(0, tb // 8, body, 0)

    lo, n_tiles = meta_ref[mbase + g], meta_ref[mbase + MOE_GROUPS + g]
    lane = lax.broadcasted_iota(jnp.int32, (tr, LANES), 1)

    def tile_body(t, carry):
        rows = pl.ds(pl.multiple_of(lo + t * tr, tr), tr)
        x = xs[rows, :]
        ce = jnp.sum(jnp.where(lane == e + ROUTER_EXPERT_LANE0, cs[rows, :], 0.0), axis=-1, keepdims=True)
        hid = _silu(_dot(x, wg_ref[0, 0])) * _dot(x, wu_ref[0, 0])
        contrib = _dot(hid * ce, wd_ref[0, 0])

        @pl.when(i == 0)
        def _():
            acc[rows, :] = contrib

        @pl.when(i != 0)
        def _():
            acc[rows, :] += contrib

        return carry

    lax.fori_loop(0, n_tiles, tile_body, 0)

    @pl.when(e == pl.num_programs(1) - 1)
    def _():
        def body(j, carry):
            t0 = pl.multiple_of(j * 8, 8)
            for u in range(8):
                y_ref[pl.ds(t0 + u, 1), :] = acc[pl.ds(pos_ref[base + t0 + u], 1), :]
            return carry

        lax.fori_loop(0, tb // 8, body, 0)


def _moe_sparse(h, comb, counts, w_gate, w_up, w_down, layer):
    n, d = h.shape
    f = w_gate.shape[-1]
    tb, tr = min(MOE_BLOCK_TOKENS, n), MOE_ROW_TILE
    nblk, r_blk = n // tb, tb + MOE_GROUPS * tr
    tok, meta = _moe_dispatch(comb, counts)
    wspec = lambda shape: pl.BlockSpec((1, 1) + shape, lambda b, e, *_: (layer, e, 0, 0))
    once = pl.Buffered(1)
    grid_spec = pltpu.PrefetchScalarGridSpec(
        num_scalar_prefetch=2, grid=(nblk, MOE_EXPERTS),
        in_specs=[pl.BlockSpec((tb, d), lambda b, e, *_: (b, 0), pipeline_mode=once),
                  pl.BlockSpec((tb, LANES), lambda b, e, *_: (b, 0), pipeline_mode=once),
                  wspec((d, f)), wspec((d, f)), wspec((f, d))],
        out_specs=pl.BlockSpec((tb, d), lambda b, e, *_: (b, 0), pipeline_mode=once),
        scratch_shapes=[pltpu.VMEM((r_blk, d), F32), pltpu.VMEM((r_blk, LANES), F32), pltpu.VMEM((r_blk, d), F32)])
    return pl.pallas_call(
        _moe_sparse_kernel, grid_spec=grid_spec, out_shape=jax.ShapeDtypeStruct((n, d), F32), name="moe_sparse",
        compiler_params=pltpu.CompilerParams(dimension_semantics=("arbitrary", "arbitrary"),
                                             vmem_limit_bytes=VMEM_LIMIT_BYTES))(tok, meta, h, comb, w_gate, w_up, w_down)


def _run_trunk(x3, mods, rows, init_a, init_b, init_c, p):
    b, t_len, d = x3.shape
    m = b * t_len
    x = x3.reshape(m, d)
    new_a, new_b, new_c = [], [], []
    resid = None
    for i in range(p["ada_w"].shape[0]):
        j = i // 3
        shift1, scale1, gate1, shift2, scale2, gate2 = mods[i]
        h = _normmod(x, p["norm1_g"][i], scale1, shift1, resid=resid, out_dtype=F32 if i % 3 == 2 else BF16)
        if resid is not None:
            x, h = h
        if i % 3 == 0:
            x, st = _mlstm_mixer(h, b, rows, tuple(s[:, j] for s in init_a), p["mlstm_w_in"][j], p["mlstm_conv_w"][j],
                                 p["mlstm_w_gate"][j], p["mlstm_b_gate"][j], p["mlstm_norm_g"][j], p["mlstm_w_out"][j],
                                 x, gate1)
            new_a.append(st)
        elif i % 3 == 1:
            x, st = _gla_mixer(h, b, rows, tuple(s[:, j] for s in init_b), p["gla_w_in"][j], p["gla_conv_w"][j],
                               p["gla_w_a1"][j], p["gla_w_a2"][j], p["gla_b_a"][j], p["gla_norm_g"][j], p["gla_w_out"][j],
                               x, gate1)
            new_b.append(st)
        else:
            s0 = (jnp.zeros((b, 2, RWKV_PAIRS, LANES, LANES), F32) if init_c is None
                  else _rwkv_state_to_pairs(init_c[0][:, j]))
            x, st = _rwkv_mixer(h, b, s0, j,p["rwkv_mu"], p["rwkv_w_rkv"], p["rwkv_w0"],
                                p["rwkv_w1"], p["rwkv_w2"], p["rwkv_a0"], p["rwkv_a1"], p["rwkv_a2"], p["rwkv_g1"],
                                p["rwkv_g2"], p["rwkv_k_k"], p["rwkv_k_a"], p["rwkv_r_k"], p["rwkv_ln_g"], p["rwkv_ln_b"],
                                p["rwkv_w_out"], x, gate1)
            new_c.append((_rwkv_state_from_pairs(st),))
        hff = _normmod(x, p["norm2_g"][i], scale2, shift2)
        comb, counts = _router(hff, p["moe_w_group"][i], p["moe_b_group"][i], p["moe_w_expert"][i], p["moe_b_expert"][i])
        resid = (_moe_sparse(hff, comb, counts, p["moe_w_gate"], p["moe_w_up"], p["moe_w_down"], i), gate2)
    stack = lambda per_layer: tuple(jnp.stack(parts, axis=1) for parts in zip(*per_layer))
    y = _resid_rmsnorm(x, resid[0], resid[1], p["final_norm_g"]).reshape(b, t_len, d)
    return y, stack(new_a), stack(new_b), stack(new_c)


def kernel(x_prompt, x_sample, state_mlstm_C, state_mlstm_n, state_mlstm_m, state_gla_S, state_rwkv_S, c, c_ctx, ada_w, ada_b, norm1_g, norm2_g, moe_w_group, moe_b_group, moe_w_expert, moe_b_expert, moe_w_gate, moe_w_up, moe_w_down, final_norm_g, mlstm_w_in, mlstm_conv_w, mlstm_w_gate, mlstm_b_gate, mlstm_norm_g, mlstm_w_out, gla_w_in, gla_conv_w, gla_w_a1, gla_w_a2, gla_b_a, gla_norm_g, gla_w_out, rwkv_mu, rwkv_w_rkv, rwkv_w0, rwkv_w1, rwkv_w2, rwkv_a0, rwkv_a1, rwkv_a2, rwkv_g1, rwkv_g2, rwkv_k_k, rwkv_k_a, rwkv_r_k, rwkv_ln_g, rwkv_ln_b, rwkv_w_out):
    p = dict(ada_w=ada_w, norm1_g=norm1_g, norm2_g=norm2_g, moe_w_group=moe_w_group, moe_b_group=moe_b_group,
             moe_w_expert=moe_w_expert, moe_b_expert=moe_b_expert, moe_w_gate=moe_w_gate, moe_w_up=moe_w_up,
             moe_w_down=moe_w_down, final_norm_g=final_norm_g, mlstm_w_in=mlstm_w_in, mlstm_conv_w=mlstm_conv_w,
             mlstm_w_gate=mlstm_w_gate, mlstm_b_gate=mlstm_b_gate, mlstm_norm_g=mlstm_norm_g, mlstm_w_out=mlstm_w_out,
             gla_w_in=gla_w_in, gla_conv_w=gla_conv_w, gla_w_a1=gla_w_a1, gla_w_a2=gla_w_a2, gla_b_a=gla_b_a,
             gla_norm_g=gla_norm_g, gla_w_out=gla_w_out, rwkv_mu=rwkv_mu, rwkv_w_rkv=rwkv_w_rkv, rwkv_w0=rwkv_w0,
             rwkv_w1=rwkv_w1, rwkv_w2=rwkv_w2, rwkv_a0=rwkv_a0, rwkv_a1=rwkv_a1, rwkv_a2=rwkv_a2, rwkv_g1=rwkv_g1,
             rwkv_g2=rwkv_g2, rwkv_k_k=rwkv_k_k, rwkv_k_a=rwkv_k_a, rwkv_r_k=rwkv_r_k, rwkv_ln_g=rwkv_ln_g,
             rwkv_ln_b=rwkv_ln_b, rwkv_w_out=rwkv_w_out)
    for name in ("mlstm_w_in", "mlstm_w_out", "gla_w_in", "gla_w_out",
                 "rwkv_w_rkv", "rwkv_w_out"):
        p[name] = p[name].astype(BF16)
    depth, d = ada_w.shape[0], ada_w.shape[1]
    n_dec = c.shape[0]
    cond8 = jnp.zeros((8, d), F32).at[0].set(c_ctx).at[1:1 + n_dec].set(c)
    mod = _ada(cond8, ada_w, ada_b)
    split = lambda rows: [tuple(rows[i][:, None, k * d:(k + 1) * d] for k in range(6)) for i in range(depth)]
    mods_ctx = split(mod[:, 0:1])
    mods_dec = split(mod[:, 1:1 + n_dec])

    bp = x_prompt.shape[0]
    zeros_like_ctx = lambda s: jnp.zeros((bp,) + s.shape[1:], F32)
    y_prompt, st_a, st_b, st_c = _run_trunk(
        x_prompt, mods_ctx, 1, tuple(zeros_like_ctx(s) for s in (state_mlstm_C, state_mlstm_n, state_mlstm_m)),
        (zeros_like_ctx(state_gla_S),), None, p)
    y_sample = _run_trunk(x_sample, mods_dec, x_sample.shape[1] // GRID_W,
                          (state_mlstm_C, state_mlstm_n, state_mlstm_m), (state_gla_S,), (state_rwkv_S,), p)[0]
    return (y_prompt, y_sample, st_a[0], st_a[1], st_a[2], st_b[0], st_c[0])
```
